```python
import math
import jax, jax.numpy as jnp
from jax import lax
import numpy as np

D_MODEL = 2048
BATCH = 8
SEQ = 2048
DEPTH = 1

MEM_LEN = 256
MEM_HEADS = 4
MEM_HEAD_DIM = 128
CHUNK = 128
GMLP_HEADS = 8
GMLP_HEAD_DIM = 128
GMLP_WIDTH = GMLP_HEADS * GMLP_HEAD_DIM
MLA_HEADS = 8
Q_LORA = 512
KV_LORA = 256
QK_NOPE = 128
QK_ROPE = 64
V_DIM = 128
MLA_WIDTH = MLA_HEADS * V_DIM
Q_BLOCK = 128
ROPE_THETA = 10000.0
IN_WIDTH = 2 * GMLP_WIDTH + Q_LORA + KV_LORA + QK_ROPE
MIX_WIDTH = GMLP_WIDTH + MLA_WIDTH
N_GROUPS = 8
EXPERTS_PER_GROUP = 8
N_EXPERTS = N_GROUPS * EXPERTS_PER_GROUP
TOP_K = 2
D_EXPERT = 512
MOE_BLOCK = 128
EPS = 1e-6

kernel_name = "hybrid_gmlp_mla_memxattn_hiermoe"


def rms_norm(x, g):
    xf = x.astype(jnp.float32)
    y = xf * lax.rsqrt(jnp.mean(xf * xf, axis=-1, keepdims=True) + EPS)
    return (y * g.astype(jnp.float32)).astype(x.dtype)


def layer_norm(x, g, b):
    xf = x.astype(jnp.float32)
    mu = jnp.mean(xf, axis=-1, keepdims=True)
    var = jnp.mean(jnp.square(xf - mu), axis=-1, keepdims=True)
    y = (xf - mu) * lax.rsqrt(var + 1e-5)
    return (y * g.astype(jnp.float32) + b.astype(jnp.float32)).astype(x.dtype)


def rotary(t, cos, sin):
    half = t.shape[-1] // 2
    t1, t2 = t[..., :half], t[..., half:]
    return jnp.concatenate([t1 * cos - t2 * sin, t2 * cos + t1 * sin], axis=-1)


def gmlp_group(z_uv, g_v, b_v, w_spatial, b_spatial):
    B, S, _ = z_uv.shape
    uv = jax.nn.gelu(z_uv)
    u, v = uv[..., :GMLP_WIDTH], uv[..., GMLP_WIDTH:]
    v = layer_norm(v, g_v, b_v)
    v = v.reshape(B, S // CHUNK, CHUNK, GMLP_HEADS, GMLP_HEAD_DIM)
    causal = jnp.tril(jnp.ones((CHUNK, CHUNK), dtype=bool))
    ws = jnp.where(causal[None], w_spatial, 0.0).astype(v.dtype)
    sv = jnp.einsum('gts,bnsgc->bntgc', ws, v)
    sv = sv + b_spatial.T.astype(v.dtype)[None, None, :, :, None]
    return u * sv.reshape(B, S, GMLP_WIDTH)


def mla_group(c_q, c_kv, k_rope, g_q_lora, w_uq, g_kv_lora, w_ukv):
    B, S, _ = c_q.shape
    q = (rms_norm(c_q, g_q_lora) @ w_uq).reshape(B, S, MLA_HEADS, QK_NOPE + QK_ROPE)
    q_nope, q_rope = q[..., :QK_NOPE], q[..., QK_NOPE:]
    kv = (rms_norm(c_kv, g_kv_lora) @ w_ukv).reshape(B, S, MLA_HEADS, QK_NOPE + V_DIM)
    k_nope, v = kv[..., :QK_NOPE], kv[..., QK_NOPE:]

    pos = jnp.arange(S, dtype=jnp.float32)
    inv_freq = ROPE_THETA ** (-jnp.arange(0, QK_ROPE, 2, dtype=jnp.float32) / QK_ROPE)
    ang = pos[:, None] * inv_freq[None, :]
    cos, sin = jnp.cos(ang).astype(q.dtype), jnp.sin(ang).astype(q.dtype)
    q_rope = rotary(q_rope, cos[None, :, None, :], sin[None, :, None, :])
    k_rope = rotary(k_rope, cos[None], sin[None])

    nb = S // Q_BLOCK
    scale = 1.0 / math.sqrt(QK_NOPE + QK_ROPE)
    k_idx = jnp.arange(S)

    def to_blocks(t):
        return jnp.moveaxis(t.reshape(B, nb, Q_BLOCK, *t.shape[2:]), 1, 0)

    def attend(args):
        qn, qr, blk = args
        s = (jnp.einsum('bqhd,bkhd->bhqk', qn, k_nope)
             + jnp.einsum('bqhr,bkr->bhqk', qr, k_rope)).astype(jnp.float32) * scale
        q_idx = blk * Q_BLOCK + jnp.arange(Q_BLOCK)
        s = jnp.where(k_idx[None, :] <= q_idx[:, None], s, jnp.finfo(jnp.float32).min)
        p = jax.nn.softmax(s, axis=-1).astype(v.dtype)
        return jnp.einsum('bhqk,bkhd->bqhd', p, v)

    o = lax.map(attend, (to_blocks(q_nope), to_blocks(q_rope), jnp.arange(nb)))
    return jnp.moveaxis(o, 0, 1).reshape(B, S, MLA_WIDTH)


def memory_cross_attention(h, mem_n, w_mq, w_mk, w_mv, w_mo):
    B, S, _ = h.shape
    q = (h @ w_mq).reshape(B, S, MEM_HEADS, MEM_HEAD_DIM)
    k = (mem_n @ w_mk).reshape(B, -1, MEM_HEADS, MEM_HEAD_DIM)
    v = (mem_n @ w_mv).reshape(B, -1, MEM_HEADS, MEM_HEAD_DIM)
    s = jnp.einsum('bshd,bmhd->bhsm', q, k).astype(jnp.float32) / math.sqrt(MEM_HEAD_DIM)
    p = jax.nn.softmax(s, axis=-1).astype(v.dtype)
    o = jnp.einsum('bhsm,bmhd->bshd', p, v).reshape(B, S, MEM_HEADS * MEM_HEAD_DIM)
    return o @ w_mo


def hierarchical_moe(h, w_rg, b_rg, w_re, b_re, w_gate, w_up, w_down):
    B, S, D = h.shape
    T = B * S
    ht = h.reshape(T, D)
    g_logits = (ht @ w_rg + b_rg).astype(jnp.float32)
    g_prob = jax.nn.softmax(g_logits, axis=-1)
    grp = jnp.argmax(g_logits, axis=-1)
    p_grp = jnp.take_along_axis(g_prob, grp[:, None], axis=-1)
    e_logits = (ht @ w_re + b_re).astype(jnp.float32).reshape(T, N_GROUPS, EXPERTS_PER_GROUP)
    e_logits = jnp.take_along_axis(e_logits, grp[:, None, None], axis=1)[:, 0]
    e_prob = jax.nn.softmax(e_logits, axis=-1)
    top_p, top_i = lax.top_k(e_prob, TOP_K)
    top_p = top_p / jnp.sum(top_p, axis=-1, keepdims=True)
    gates = (p_grp * top_p).astype(h.dtype)
    expert_idx = grp[:, None] * EXPERTS_PER_GROUP + top_i

    A = T * TOP_K
    P = A + N_EXPERTS * MOE_BLOCK
    NB = P // MOE_BLOCK
    flat_e = expert_idx.reshape(A).astype(jnp.int32)
    flat_tok = jnp.repeat(jnp.arange(T, dtype=jnp.int32), TOP_K)
    flat_w = gates.reshape(A)
    order = jnp.argsort(flat_e)
    sorted_e = flat_e[order]
    counts = jnp.bincount(flat_e, length=N_EXPERTS)
    starts = jnp.cumsum(counts) - counts
    padded = ((counts + MOE_BLOCK - 1) // MOE_BLOCK) * MOE_BLOCK
    padded_ends = jnp.cumsum(padded)
    padded_starts = padded_ends - padded
    dest = padded_starts[sorted_e] + (jnp.arange(A) - starts[sorted_e])
    rows_tok = jnp.full((P,), T, jnp.int32).at[dest].set(flat_tok[order])
    rows_w = jnp.zeros((P,), h.dtype).at[dest].set(flat_w[order])
    block_expert = jnp.clip(jnp.searchsorted(padded_ends, jnp.arange(NB) * MOE_BLOCK, side='right'),
                            0, N_EXPERTS - 1)
    h_pad = jnp.concatenate([ht, jnp.zeros((1, D), ht.dtype)], axis=0)
    xs = h_pad[rows_tok].reshape(NB, MOE_BLOCK, D)

    def expert_block(args):
        xb, e = args
        return (jax.nn.silu(xb @ w_gate[e]) * (xb @ w_up[e])) @ w_down[e]

    ys = lax.map(expert_block, (xs, block_expert)).reshape(P, D) * rows_w[:, None]
    out = jnp.zeros((T + 1, D), h.dtype).at[rows_tok].add(ys)[:T]
    return out.reshape(B, S, D)


def setup_inputs(seed: int = 0) -> dict:
    key = jax.random.key(seed)
    ks = iter(jax.random.split(key, 40))

    def nrm(shape, scale):
        return jax.random.normal(next(ks), shape, jnp.float32) * scale

    def gain(n):
        return 1.0 + 0.05 * jax.random.normal(next(ks), (n,), jnp.float32)

    D = D_MODEL
    return {
        "x": nrm((BATCH, SEQ, D), 1.0),
        "mem": nrm((BATCH, MEM_LEN, D), 1.0),
        "g_norm_mix": gain(D),
        "w_in": nrm((D, IN_WIDTH), D ** -0.5),
        "g_v": gain(GMLP_WIDTH),
        "b_v": nrm((GMLP_WIDTH,), 0.02),
        "w_spatial": nrm((GMLP_HEADS, CHUNK, CHUNK), 0.5 * CHUNK ** -0.5),
        "b_spatial": 1.0 + nrm((GMLP_HEADS, CHUNK), 0.02),
        "g_q_lora": gain(Q_LORA),
        "w_uq": nrm((Q_LORA, MLA_HEADS * (QK_NOPE + QK_ROPE)), Q_LORA ** -0.5),
        "g_kv_lora": gain(KV_LORA),
        "w_ukv": nrm((KV_LORA, MLA_HEADS * (QK_NOPE + V_DIM)), KV_LORA ** -0.5),
        "g_out_gmlp": gain(GMLP_WIDTH),
        "g_out_mla": gain(MLA_WIDTH),
        "w_out": nrm((MIX_WIDTH, D), MIX_WIDTH ** -0.5),
        "g_norm_xattn": gain(D),
        "g_norm_mem": gain(D),
        "w_mq": nrm((D, MEM_HEADS * MEM_HEAD_DIM), D ** -0.5),
        "w_mk": nrm((D, MEM_HEADS * MEM_HEAD_DIM), D ** -0.5),
        "w_mv": nrm((D, MEM_HEADS * MEM_HEAD_DIM), D ** -0.5),
        "w_mo": nrm((MEM_HEADS * MEM_HEAD_DIM, D), (MEM_HEADS * MEM_HEAD_DIM) ** -0.5),
        "g_norm_moe": gain(D),
        "w_router_group": nrm((D, N_GROUPS), D ** -0.5),
        "b_router_group": nrm((N_GROUPS,), 0.01),
        "w_router_expert": nrm((D, N_EXPERTS), D ** -0.5),
        "b_router_expert": nrm((N_EXPERTS,), 0.01),
        "w_exp_gate": nrm((N_EXPERTS, D, D_EXPERT), D ** -0.5),
        "w_exp_up": nrm((N_EXPERTS, D, D_EXPERT), D ** -0.5),
        "w_exp_down": nrm((N_EXPERTS, D_EXPERT, D), D_EXPERT ** -0.5),
        "g_final": gain(D),
    }


def reference(x, mem, g_norm_mix, w_in, g_v, b_v, w_spatial, b_spatial, g_q_lora, w_uq,
              g_kv_lora, w_ukv, g_out_gmlp, g_out_mla, w_out, g_norm_xattn, g_norm_mem,
              w_mq, w_mk, w_mv, w_mo, g_norm_moe, w_router_group, b_router_group,
              w_router_expert, b_router_expert, w_exp_gate, w_exp_up, w_exp_down, g_final):
    c1 = 2 * GMLP_WIDTH
    c2 = c1 + Q_LORA
    c3 = c2 + KV_LORA
    for _ in range(DEPTH):
        z = rms_norm(x, g_norm_mix) @ w_in
        a = gmlp_group(z[..., :c1], g_v, b_v, w_spatial, b_spatial)
        m = mla_group(z[..., c1:c2], z[..., c2:c3], z[..., c3:], g_q_lora, w_uq, g_kv_lora, w_ukv)
        merged = jnp.concatenate([rms_norm(a, g_out_gmlp), rms_norm(m, g_out_mla)], axis=-1)
        x = x + merged @ w_out
        x = x + memory_cross_attention(rms_norm(x, g_norm_xattn), rms_norm(mem, g_norm_mem),
                                       w_mq, w_mk, w_mv, w_mo)
        x = x + hierarchical_moe(rms_norm(x, g_norm_moe), w_router_group, b_router_group,
                                 w_router_expert, b_router_expert, w_exp_gate, w_exp_up, w_exp_down)
    return rms_norm(x, g_final)
```

```python
import functools
import math

import jax
import jax.numpy as jnp
import numpy as np
from jax import lax
from jax.experimental import pallas as pl
from jax.experimental.pallas import tpu as pltpu

D_MODEL = 2048
CHUNK = 128
GMLP_HEADS = 8
GMLP_WIDTH = 1024
MLA_HEADS = 8
Q_LORA = 512
KV_LORA = 256
QK_NOPE = 128
QK_ROPE = 64
V_DIM = 128
MLA_WIDTH = MLA_HEADS * V_DIM
ROPE_THETA = 10000.0
MEM_HEADS = 4
MEM_HEAD_DIM = 128
MEM_WIDTH = MEM_HEADS * MEM_HEAD_DIM
N_GROUPS = 8
EXPERTS_PER_GROUP = 8
N_EXPERTS = 64
TOP_K = 2
D_EXPERT = 512
EPS = 1e-6
LN_EPS = 1e-5

LANES = 128
QK_PAD = 256
VMEM_LIMIT = 56 * 1024 * 1024

C_UV = 0
C_Q = 2 * GMLP_WIDTH
C_KV = C_Q + Q_LORA
C_KR = C_KV + KV_LORA
IN_EXT = C_KR + LANES

TM_FRONT = 512
TM_MID = 256
TQ = 256
BM = 256
TM_ROW = 256

F32 = jnp.float32
BF16 = jnp.bfloat16
U32 = jnp.uint32
NEG = float(np.finfo(np.float32).min)


def _cparams(sem):
    return pltpu.CompilerParams(dimension_semantics=sem, vmem_limit_bytes=VMEM_LIMIT)


def _const_spec(shape):
    n = len(shape)
    return pl.BlockSpec(shape, lambda *_: (0,) * n, pipeline_mode=pl.Buffered(1))


def _rms(x, g):
    ms = jnp.mean(x * x, axis=-1, keepdims=True)
    return x * lax.rsqrt(ms + EPS) * g


def _gelu_tanh(x):
    c = math.sqrt(2.0 / math.pi)
    return 0.5 * x * (1.0 + jnp.tanh(c * (x + 0.044715 * (x * x * x))))


def _dot(a, b):
    return jnp.dot(a, b, preferred_element_type=F32)


def _dot_nt(a, b):
    return lax.dot_general(a, b, (((1,), (1,)), ((), ())), preferred_element_type=F32)


def _pack_pair(hi, lo):
    hb = pltpu.bitcast(hi.astype(BF16).astype(F32), U32)
    lb = pltpu.bitcast(lo.astype(BF16).astype(F32), U32)
    return hb | (lb >> 16)


def _unpack_pair(w):
    hi = pltpu.bitcast(w & jnp.uint32(0xFFFF0000), F32)
    lo = pltpu.bitcast(w << 16, F32)
    return hi, lo


def _front_kernel(x_ref, gmix_ref, win_ref, gv_ref, bv_ref, ws_ref, bsp_ref, goutg_ref,
                  gq_ref, wuq_ref, gkv_ref, wukv_ref, cs_ref,
                  an_ref, q_ref, k_ref, v_ref):
    tm = x_ref.shape[0]
    xn = _rms(x_ref[...], gmix_ref[...]).astype(BF16)
    z = _dot(xn, win_ref[...])

    u = _gelu_tanh(z[:, :GMLP_WIDTH])
    v = _gelu_tanh(z[:, GMLP_WIDTH:2 * GMLP_WIDTH])
    mu = jnp.mean(v, axis=-1, keepdims=True)
    vc = v - mu
    var = jnp.mean(vc * vc, axis=-1, keepdims=True)
    vn = (vc * lax.rsqrt(var + LN_EPS) * gv_ref[...] + bv_ref[...]).astype(BF16)
    a_chunks = []
    for c in range(tm // CHUNK):
        r0 = c * CHUNK
        cols = []
        for g in range(GMLP_HEADS):
            c0 = g * LANES
            sv = _dot(ws_ref[g], vn[r0:r0 + CHUNK, c0:c0 + LANES])
            cols.append(sv)
        sv_all = jnp.concatenate(cols, axis=1) + bsp_ref[...]
        a_chunks.append(u[r0:r0 + CHUNK, :] * sv_all)
    a = jnp.concatenate(a_chunks, axis=0)
    an_ref[...] = _rms(a, goutg_ref[...]).astype(BF16)

    cs = cs_ref[...]
    scale = 1.0 / math.sqrt(QK_NOPE + QK_ROPE)
    cqn = _rms(z[:, C_Q:C_KV], gq_ref[...]).astype(BF16)
    q = _dot(cqn, wuq_ref[...]) * scale
    ckvn = _rms(z[:, C_KV:C_KR], gkv_ref[...]).astype(BF16)
    kv = _dot(ckvn, wukv_ref[...])
    lane = lax.broadcasted_iota(jnp.int32, (tm, LANES), 1)
    kr = z[:, C_KR:IN_EXT] * cs
    kr = jnp.where(lane < QK_ROPE, kr + pltpu.roll(kr, QK_ROPE, 1), 0.0).astype(BF16)
    for h in range(MLA_HEADS):
        b0 = h * QK_PAD
        qr = q[:, b0 + LANES:b0 + QK_PAD] * cs
        qr = qr + pltpu.roll(qr, QK_ROPE, 1)
        q_ref[:, b0:b0 + LANES] = q[:, b0:b0 + LANES].astype(BF16)
        q_ref[:, b0 + LANES:b0 + QK_PAD] = qr.astype(BF16)
        k_ref[:, b0:b0 + LANES] = kv[:, b0:b0 + LANES].astype(BF16)
        k_ref[:, b0 + LANES:b0 + QK_PAD] = kr
        v_ref[:, h * V_DIM:(h + 1) * V_DIM] = kv[:, b0 + LANES:b0 + QK_PAD].astype(BF16)


def _front(x2d, gmix, win_ext, gv, bv, ws, bsp, goutg, gq, wuq_ext, gkv, wukv, cs, seq):
    t = x2d.shape[0]
    tm = TM_FRONT
    nseq = seq // tm
    row = lambda i: (i, 0)
    return pl.pallas_call(
        _front_kernel,
        grid=(t // tm,),
        in_specs=[
            pl.BlockSpec((tm, D_MODEL), row),
            _const_spec((1, D_MODEL)),
            _const_spec((D_MODEL, IN_EXT)),
            _const_spec((1, GMLP_WIDTH)),
            _const_spec((1, GMLP_WIDTH)),
            _const_spec((GMLP_HEADS, CHUNK, CHUNK)),
            _const_spec((CHUNK, GMLP_WIDTH)),
            _const_spec((1, GMLP_WIDTH)),
            _const_spec((1, Q_LORA)),
            _const_spec((Q_LORA, MLA_HEADS * QK_PAD)),
            _const_spec((1, KV_LORA)),
            _const_spec((KV_LORA, MLA_HEADS * QK_PAD)),
            pl.BlockSpec((tm, LANES), lambda i: (i % nseq, 0)),
        ],
        out_specs=[
            pl.BlockSpec((tm, GMLP_WIDTH), row),
            pl.BlockSpec((tm, MLA_HEADS * QK_PAD), row),
            pl.BlockSpec((tm, MLA_HEADS * QK_PAD), row),
            pl.BlockSpec((tm, MLA_WIDTH), row),
        ],
        out_shape=[
            jax.ShapeDtypeStruct((t, GMLP_WIDTH), BF16),
            jax.ShapeDtypeStruct((t, MLA_HEADS * QK_PAD), BF16),
            jax.ShapeDtypeStruct((t, MLA_HEADS * QK_PAD), BF16),
            jax.ShapeDtypeStruct((t, MLA_WIDTH), BF16),
        ],
        compiler_params=_cparams(("arbitrary",)),
        name="front",
    )(x2d, gmix, win_ext, gv, bv, ws, bsp, goutg, gq, wuq_ext, gkv, wukv, cs)


def _attn_kernel(q_ref, k_ref, v_ref, o_ref):
    s_len = q_ref.shape[0]
    row = lax.broadcasted_iota(jnp.int32, (TQ, TQ), 0)
    col = lax.broadcasted_iota(jnp.int32, (TQ, TQ), 1)
    causal = col <= row
    for qi in range(s_len // TQ):
        r0 = qi * TQ
        qb = q_ref[r0:r0 + TQ, :]
        sd = jnp.where(causal, _dot_nt(qb, k_ref[r0:r0 + TQ, :]), NEG)
        m = jnp.max(sd, axis=-1, keepdims=True)
        if qi > 0:
            so = _dot_nt(qb, k_ref[0:r0, :])
            m = jnp.maximum(m, jnp.max(so, axis=-1, keepdims=True))
        pd = jnp.exp(sd - m)
        l = jnp.sum(pd, axis=-1, keepdims=True)
        acc = _dot(pd.astype(BF16), v_ref[r0:r0 + TQ, :])
        if qi > 0:
            po = jnp.exp(so - m)
            l = l + jnp.sum(po, axis=-1, keepdims=True)
            acc = acc + _dot(po.astype(BF16), v_ref[0:r0, :])
        o_ref[r0:r0 + TQ, :] = (acc / l).astype(o_ref.dtype)


def _mla_attn(q, k, v, batch, seq):
    q3 = q.reshape(batch, seq, MLA_HEADS * QK_PAD)
    k3 = k.reshape(batch, seq, MLA_HEADS * QK_PAD)
    v3 = v.reshape(batch, seq, MLA_WIDTH)
    hmap = lambda b, h: (b, 0, h)
    out = pl.pallas_call(
        _attn_kernel,
        grid=(batch, MLA_HEADS),
        in_specs=[
            pl.BlockSpec((None, seq, QK_PAD), hmap),
            pl.BlockSpec((None, seq, QK_PAD), hmap),
            pl.BlockSpec((None, seq, V_DIM), hmap),
        ],
        out_specs=pl.BlockSpec((None, seq, V_DIM), hmap),
        out_shape=jax.ShapeDtypeStruct((batch, seq, MLA_WIDTH), BF16),
        compiler_params=_cparams(("arbitrary", "arbitrary")),
        name="mla_attn",
    )(q3, k3, v3)
    return out.reshape(batch * seq, MLA_WIDTH)


def _memkv_kernel(mem_ref, g_ref, wk_ref, wv_ref, k_ref, v_ref):
    mn = _rms(mem_ref[...], g_ref[...]).astype(BF16)
    k_ref[...] = _dot(mn, wk_ref[...]).astype(BF16)
    v_ref[...] = _dot(mn, wv_ref[...]).astype(BF16)


def _mem_kv(mem, g, wk, wv):
    b, m, d = mem.shape
    bmap = lambda i: (i, 0, 0)
    return pl.pallas_call(
        _memkv_kernel,
        grid=(b,),
        in_specs=[
            pl.BlockSpec((None, m, d), bmap),
            _const_spec((1, d)),
            _const_spec((d, MEM_WIDTH)),
            _const_spec((d, MEM_WIDTH)),
        ],
        out_specs=[pl.BlockSpec((None, m, MEM_WIDTH), bmap)] * 2,
        out_shape=[jax.ShapeDtypeStruct((b, m, MEM_WIDTH), BF16)] * 2,
        compiler_params=_cparams(("arbitrary",)),
        name="mem_kv",
    )(mem, g, wk, wv)


def _route(lg):
    tm = lg.shape[0]
    lane_i = lax.broadcasted_iota(jnp.int32, (tm, LANES), 1)
    lane = lane_i.astype(F32)
    big = float(LANES)
    gmask = lane_i < N_GROUPS
    gl = jnp.where(gmask, lg, NEG)
    gmax = jnp.max(gl, axis=-1, keepdims=True)
    grp = jnp.min(jnp.where(gl == gmax, lane, big), axis=-1, keepdims=True)
    gsum = jnp.sum(jnp.where(gmask, jnp.exp(gl - gmax), 0.0), axis=-1, keepdims=True)
    p_grp = 1.0 / gsum

    lo = (grp + 1.0) * EXPERTS_PER_GROUP
    emask = jnp.logical_and(lane >= lo, lane < lo + EXPERTS_PER_GROUP)
    el = jnp.where(emask, lg, NEG)
    emax = jnp.max(el, axis=-1, keepdims=True)
    ee = jnp.where(emask, jnp.exp(el - emax), 0.0)
    ep = ee / jnp.sum(ee, axis=-1, keepdims=True)
    ep = jnp.where(emask, ep, -1.0)
    p1 = jnp.max(ep, axis=-1, keepdims=True)
    i1 = jnp.min(jnp.where(ep == p1, lane, big), axis=-1, keepdims=True)
    ep2 = jnp.where(lane == i1, -1.0, ep)
    p2 = jnp.max(ep2, axis=-1, keepdims=True)
    i2 = jnp.min(jnp.where(ep2 == p2, lane, big), axis=-1, keepdims=True)
    den = p1 + p2
    g1 = p_grp * (p1 / den)
    g2 = p_grp * (p2 / den)
    e1 = (i1 - N_GROUPS).astype(jnp.int32)
    e2 = (i2 - N_GROUPS).astype(jnp.int32)
    idx = jnp.where(lane_i == 0, e1, jnp.where(lane_i == 1, e2, 0))
    gate = jnp.where(lane_i == 0, g1, jnp.where(lane_i == 1, g2, 0.0))
    return idx, gate


def _mid_kernel(an_ref, m_ref, x_ref, wout_ref, goutm_ref, gxa_ref, wmq_ref, kmem_ref, vmem_ref,
                wmo_ref, gmoe_ref, wr_ref, br_ref,
                x2_ref, hp_ref, idx_ref, gate_ref):
    mn = _rms(m_ref[...].astype(F32), goutm_ref[...]).astype(BF16)
    x1 = (x_ref[...] + _dot(an_ref[...], wout_ref[:GMLP_WIDTH, :])
          + _dot(mn, wout_ref[GMLP_WIDTH:, :]))

    h2 = _rms(x1, gxa_ref[...]).astype(BF16)
    q2 = (_dot(h2, wmq_ref[...]) * (1.0 / math.sqrt(MEM_HEAD_DIM))).astype(BF16)
    outs = []
    for h in range(MEM_HEADS):
        c0 = h * MEM_HEAD_DIM
        s = _dot_nt(q2[:, c0:c0 + MEM_HEAD_DIM], kmem_ref[:, c0:c0 + MEM_HEAD_DIM])
        m = jnp.max(s, axis=-1, keepdims=True)
        p = jnp.exp(s - m)
        l = jnp.sum(p, axis=-1, keepdims=True)
        outs.append(_dot(p.astype(BF16), vmem_ref[:, c0:c0 + MEM_HEAD_DIM]) / l)
    o = jnp.concatenate(outs, axis=1).astype(BF16)
    x2 = x1 + _dot(o, wmo_ref[...])
    x2_ref[...] = x2

    h3 = _rms(x2, gmoe_ref[...])
    half = D_MODEL // 2
    hp_ref[...] = _pack_pair(h3[:, :half], h3[:, half:])
    lg = _dot(h3.astype(BF16), wr_ref[...]) + br_ref[...]
    idx, gate = _route(lg)
    idx_ref[...] = idx
    gate_ref[...] = gate


def _mid(an, m, x2d, wout, goutm, gxa, wmq, kmem, vmem, wmo, gmoe, wr, br, seq):
    t = x2d.shape[0]
    tm = TM_MID
    nseq = seq // tm
    mem_len = kmem.shape[1]
    row = lambda i: (i, 0)
    bmap = lambda i: (i // nseq, 0, 0)
    return pl.pallas_call(
        _mid_kernel,
        grid=(t // tm,),
        in_specs=[
            pl.BlockSpec((tm, GMLP_WIDTH), row),
            pl.BlockSpec((tm, MLA_WIDTH), row),
            pl.BlockSpec((tm, D_MODEL), row),
            _const_spec((D_MODEL, D_MODEL)),
            _const_spec((1, MLA_WIDTH)),
            _const_spec((1, D_MODEL)),
            _const_spec((D_MODEL, MEM_WIDTH)),
            pl.BlockSpec((None, mem_len, MEM_WIDTH), bmap),
            pl.BlockSpec((None, mem_len, MEM_WIDTH), bmap),
            _const_spec((MEM_WIDTH, D_MODEL)),
            _const_spec((1, D_MODEL)),
            _const_spec((D_MODEL, LANES)),
            _const_spec((1, LANES)),
        ],
        out_specs=[
            pl.BlockSpec((tm, D_MODEL), row),
            pl.BlockSpec((tm, D_MODEL // 2), row),
            pl.BlockSpec((tm, LANES), row),
            pl.BlockSpec((tm, LANES), row),
        ],
        out_shape=[
            jax.ShapeDtypeStruct((t, D_MODEL), F32),
            jax.ShapeDtypeStruct((t, D_MODEL // 2), U32),
            jax.ShapeDtypeStruct((t, LANES), jnp.int32),
            jax.ShapeDtypeStruct((t, LANES), F32),
        ],
        compiler_params=_cparams(("arbitrary",)),
        name="mid",
    )(an, m, x2d, wout, goutm, gxa, wmq, kmem, vmem, wmo, gmoe, wr, br)


def _dispatch_kernel(dest_ref, hp_ref, xs_in_ref, xs_ref, sem):
    del xs_in_ref
    tm = hp_ref.shape[0]

    def issue(t, carry):
        for k in range(TOP_K):
            d = dest_ref[0, 0, t * TOP_K + k]
            pltpu.make_async_copy(hp_ref.at[pl.ds(t, 1), :], xs_ref.at[pl.ds(d, 1), :], sem).start()
        return carry

    lax.fori_loop(0, tm, issue, 0)
    for k in range(TOP_K):
        pltpu.make_async_copy(hp_ref, xs_ref.at[pl.ds(0, tm), :], sem).wait()


def _dispatch(dest, hp, xs0):
    t, w = hp.shape
    tm = TM_ROW
    nb = t // tm
    dest3 = dest.reshape(nb, 1, tm * TOP_K)
    return pl.pallas_call(
        _dispatch_kernel,
        grid=(nb,),
        in_specs=[
            pl.BlockSpec((1, 1, tm * TOP_K), lambda i: (i, 0, 0), memory_space=pltpu.SMEM),
            pl.BlockSpec((tm, w), lambda i: (i, 0)),
            pl.BlockSpec(memory_space=pl.ANY),
        ],
        out_specs=pl.BlockSpec(memory_space=pl.ANY),
        out_shape=jax.ShapeDtypeStruct(xs0.shape, xs0.dtype),
        scratch_shapes=[pltpu.SemaphoreType.DMA(())],
        input_output_aliases={2: 0},
        compiler_params=pltpu.CompilerParams(dimension_semantics=("arbitrary",),
                                             vmem_limit_bytes=VMEM_LIMIT, has_side_effects=True),
        name="dispatch",
    )(dest3, hp, xs0)


def _expert_kernel(be_ref, nused_ref, xs_ref, wg_ref, wu_ref, wd_ref, ys_ref, wg_s, wu_s, wd_s):
    i = pl.program_id(0)
    prev = be_ref[jnp.maximum(i - 1, 0)]
    changed = jnp.logical_or(i == 0, be_ref[i] != prev)
    used = i < nused_ref[0]

    @pl.when(jnp.logical_and(changed, used))
    def _():
        wg_s[...] = wg_ref[...].astype(BF16)
        wu_s[...] = wu_ref[...].astype(BF16)
        wd_s[...] = wd_ref[...].astype(BF16)

    @pl.when(used)
    def _():
        half = D_MODEL // 2
        hi, lo = _unpack_pair(xs_ref[...])
        hi = hi.astype(BF16)
        lo = lo.astype(BF16)
        g = _dot(hi, wg_s[:half, :]) + _dot(lo, wg_s[half:, :])
        u = _dot(hi, wu_s[:half, :]) + _dot(lo, wu_s[half:, :])
        hm = (g * jax.nn.sigmoid(g) * u).astype(BF16)
        y = _dot(hm, wd_s[...])
        ys_ref[...] = _pack_pair(y[:, :half], y[:, half:])

    @pl.when(jnp.logical_not(used))
    def _():
        ys_ref[...] = jnp.zeros(ys_ref.shape, ys_ref.dtype)


def _experts(block_expert, n_used, xs, wg, wu, wd):
    p, w = xs.shape
    nb = p // BM
    grid_spec = pltpu.PrefetchScalarGridSpec(
        num_scalar_prefetch=2,
        grid=(nb,),
        in_specs=[
            pl.BlockSpec((BM, w), lambda i, be, nu: (i, 0)),
            pl.BlockSpec((None, D_MODEL, D_EXPERT), lambda i, be, nu: (be[i], 0, 0)),
            pl.BlockSpec((None, D_MODEL, D_EXPERT), lambda i, be, nu: (be[i], 0, 0)),
            pl.BlockSpec((None, D_EXPERT, D_MODEL), lambda i, be, nu: (be[i], 0, 0)),
        ],
        out_specs=pl.BlockSpec((BM, w), lambda i, be, nu: (i, 0)),
        scratch_shapes=[
            pltpu.VMEM((D_MODEL, D_EXPERT), BF16),
            pltpu.VMEM((D_MODEL, D_EXPERT), BF16),
            pltpu.VMEM((D_EXPERT, D_MODEL), BF16),
        ],
    )
    return pl.pallas_call(
        _expert_kernel,
        grid_spec=grid_spec,
        out_shape=jax.ShapeDtypeStruct((p, w), U32),
        compiler_params=_cparams(("arbitrary",)),
        name="experts",
    )(block_expert, n_used, xs, wg, wu, wd)


def _combine_kernel(dest_ref, x2_ref, gate_ref, gfin_ref, ys_ref, o_ref, buf, sem):
    tm = x2_ref.shape[0]

    def issue(t, carry):
        for k in range(TOP_K):
            d = dest_ref[0, 0, t * TOP_K + k]
            pltpu.make_async_copy(ys_ref.at[pl.ds(d, 1), :], buf.at[k, pl.ds(t, 1), :], sem).start()
        return carry

    lax.fori_loop(0, tm, issue, 0)
    for k in range(TOP_K):
        pltpu.make_async_copy(ys_ref.at[pl.ds(0, tm), :], buf.at[k], sem).wait()

    gate = gate_ref[...]
    half = D_MODEL // 2
    y_hi = x2_ref[:, :half]
    y_lo = x2_ref[:, half:]
    for k in range(TOP_K):
        hi, lo = _unpack_pair(buf[k])
        gk = gate[:, k:k + 1]
        y_hi = y_hi + gk * hi
        y_lo = y_lo + gk * lo
    ms = (jnp.sum(y_hi * y_hi, axis=-1, keepdims=True)
          + jnp.sum(y_lo * y_lo, axis=-1, keepdims=True)) * (1.0 / D_MODEL)
    r = lax.rsqrt(ms + EPS)
    o_ref[:, :half] = y_hi * r * gfin_ref[:, :half]
    o_ref[:, half:] = y_lo * r * gfin_ref[:, half:]


def _combine(dest, x2, gate, gfin, ys):
    t, d = x2.shape
    tm = TM_ROW
    nb = t // tm
    dest3 = dest.reshape(nb, 1, tm * TOP_K)
    row = lambda i: (i, 0)
    return pl.pallas_call(
        _combine_kernel,
        grid=(nb,),
        in_specs=[
            pl.BlockSpec((1, 1, tm * TOP_K), lambda i: (i, 0, 0), memory_space=pltpu.SMEM),
            pl.BlockSpec((tm, d), row),
            pl.BlockSpec((tm, LANES), row),
            _const_spec((1, d)),
            pl.BlockSpec(memory_space=pl.ANY),
        ],
        out_specs=pl.BlockSpec((tm, d), row),
        out_shape=jax.ShapeDtypeStruct((t, d), F32),
        scratch_shapes=[
            pltpu.VMEM((TOP_K, tm, d // 2), U32),
            pltpu.SemaphoreType.DMA(()),
        ],
        compiler_params=_cparams(("arbitrary",)),
        name="combine",
    )(dest3, x2, gate, gfin, ys)


def _rot_half_cols(w):
    half = QK_ROPE // 2
    return jnp.concatenate([-w[..., half:], w[..., :half]], axis=-1)


def kernel(x, mem, g_norm_mix, w_in, g_v, b_v, w_spatial, b_spatial, g_q_lora, w_uq, g_kv_lora, w_ukv, g_out_gmlp, g_out_mla, w_out, g_norm_xattn, g_norm_mem, w_mq, w_mk, w_mv, w_mo, g_norm_moe, w_router_group, b_router_group, w_router_expert, b_router_expert, w_exp_gate, w_exp_up, w_exp_down, g_final):
    batch, seq, d = x.shape
    t = batch * seq
    x2d = x.reshape(t, d)
    r2 = lambda a: a.reshape(1, -1)

    w_kr = w_in[:, C_KR:C_KR + QK_ROPE]
    win_ext = jnp.concatenate([w_in, _rot_half_cols(w_kr)], axis=1).astype(BF16)
    wq3 = w_uq.reshape(Q_LORA, MLA_HEADS, QK_NOPE + QK_ROPE)
    wq_rope = wq3[..., QK_NOPE:]
    wuq_ext = jnp.concatenate([wq3, _rot_half_cols(wq_rope)], axis=-1)
    wuq_ext = wuq_ext.reshape(Q_LORA, MLA_HEADS * QK_PAD).astype(BF16)
    wukv = w_ukv.astype(BF16)
    causal = jnp.tril(jnp.ones((CHUNK, CHUNK), dtype=bool))
    ws = jnp.where(causal[None], w_spatial, 0.0).astype(BF16)
    bsp = jnp.repeat(b_spatial.T, LANES, axis=1)
    wr = jnp.concatenate(
        [w_router_group, w_router_expert,
         jnp.zeros((d, LANES - N_GROUPS - N_EXPERTS), F32)], axis=1).astype(BF16)
    br = jnp.concatenate(
        [b_router_group, b_router_expert, jnp.zeros((LANES - N_GROUPS - N_EXPERTS,), F32)]).reshape(1, LANES)

    pos = jnp.arange(seq, dtype=F32)
    inv_freq = ROPE_THETA ** (-jnp.arange(0, QK_ROPE, 2, dtype=F32) / QK_ROPE)
    ang = pos[:, None] * inv_freq[None, :]
    cos, sin = jnp.cos(ang), jnp.sin(ang)
    cs = jnp.concatenate([cos, cos, sin, sin], axis=1)

    an, q, k, v = _front(x2d, r2(g_norm_mix), win_ext, r2(g_v), r2(b_v), ws, bsp, r2(g_out_gmlp),
                         r2(g_q_lora), wuq_ext, r2(g_kv_lora), wukv, cs, seq)
    m = _mla_attn(q, k, v, batch, seq)
    kmem, vmem = _mem_kv(mem, r2(g_norm_mem), w_mk.astype(BF16), w_mv.astype(BF16))
    x2, hp, idx, gate = _mid(an, m, x2d, w_out.astype(BF16), r2(g_out_mla), r2(g_norm_xattn),
                             w_mq.astype(BF16), kmem, vmem, w_mo.astype(BF16), r2(g_norm_moe),
                             wr, br, seq)

    n_assign = t * TOP_K
    p_rows = n_assign + N_EXPERTS * BM
    flat_e = idx[:, :TOP_K].reshape(n_assign)
    onehot = (flat_e[:, None] == jnp.arange(N_EXPERTS, dtype=jnp.int32)[None, :]).astype(jnp.int32)
    csum = jnp.cumsum(onehot, axis=0)
    rank = jnp.sum(onehot * csum, axis=1) - 1
    counts = csum[-1]
    padded = ((counts + BM - 1) // BM) * BM
    padded_ends = jnp.cumsum(padded)
    padded_starts = padded_ends - padded
    dest = (jnp.sum(onehot * padded_starts[None, :], axis=1) + rank).astype(jnp.int32)
    block_expert = jnp.clip(
        jnp.searchsorted(padded_ends, jnp.arange(p_rows // BM, dtype=jnp.int32) * BM, side='right'),
        0, N_EXPERTS - 1).astype(jnp.int32)

    n_used = (padded_ends[-1:] // BM).astype(jnp.int32)

    xs = _dispatch(dest, hp, jnp.zeros((p_rows, d // 2), U32))
    ys = _experts(block_expert, n_used, xs, w_exp_gate, w_exp_up, w_exp_down)
    out = _combine(dest, x2, gate, r2(g_final), ys)
    return out.reshape(batch, seq, d)
```

```python
import math

import jax
import jax.numpy as jnp
import numpy as np
from jax import lax
from jax.experimental import pallas as pl
from jax.experimental.pallas import tpu as pltpu

D_MODEL = 2048
CHUNK = 128
GMLP_HEADS = 8
GMLP_WIDTH = 1024
MLA_HEADS = 8
Q_LORA = 512
KV_LORA = 256
QK_NOPE = 128
QK_ROPE = 64
V_DIM = 128
MLA_WIDTH = MLA_HEADS * V_DIM
ROPE_THETA = 10000.0
MEM_HEADS = 4
MEM_HEAD_DIM = 128
MEM_WIDTH = MEM_HEADS * MEM_HEAD_DIM
N_GROUPS = 8
EXPERTS_PER_GROUP = 8
N_EXPERTS = 64
TOP_K = 2
D_EXPERT = 512
EPS = 1e-6
LN_EPS = 1e-5

LANES = 128
QK_PAD = 256
VMEM_LIMIT = 56 * 1024 * 1024

C_UV = 0
C_Q = 2 * GMLP_WIDTH
C_KV = C_Q + Q_LORA
C_KR = C_KV + KV_LORA
IN_EXT = C_KR + LANES

TM_FRONT = 512
TM_MID = 256
TQ = 256
BM = 256
TM_ROW = 256

F32 = jnp.float32
BF16 = jnp.bfloat16
U32 = jnp.uint32
NEG = float(np.finfo(np.float32).min)


def _cparams(sem):
    return pltpu.CompilerParams(dimension_semantics=sem, vmem_limit_bytes=VMEM_LIMIT)


def _const_spec(shape):
    n = len(shape)
    return pl.BlockSpec(shape, lambda *_: (0,) * n, pipeline_mode=pl.Buffered(1))


def _rms(x, g):
    ms = jnp.mean(x * x, axis=-1, keepdims=True)
    return x * lax.rsqrt(ms + EPS) * g


def _gelu_tanh(x):
    c = math.sqrt(2.0 / math.pi)
    return 0.5 * x * (1.0 + jnp.tanh(c * (x + 0.044715 * (x * x * x))))


def _dot(a, b):
    return jnp.dot(a, b, preferred_element_type=F32)


def _dot_nt(a, b):
    return lax.dot_general(a, b, (((1,), (1,)), ((), ())), preferred_element_type=F32)


def _pack_pair(hi, lo):
    hb = pltpu.bitcast(hi.astype(BF16).astype(F32), U32)
    lb = pltpu.bitcast(lo.astype(BF16).astype(F32), U32)
    return hb | (lb >> 16)


def _unpack_pair(w):
    hi = pltpu.bitcast(w & jnp.uint32(0xFFFF0000), F32)
    lo = pltpu.bitcast(w << 16, F32)
    return hi, lo


def _front_kernel(x_ref, gmix_ref, win_ref, gv_ref, bv_ref, ws_ref, bsp_ref, goutg_ref,
                  gq_ref, wuq_ref, gkv_ref, wukv_ref, cs_ref,
                  an_ref, q_ref, k_ref, v_ref):
    tm = x_ref.shape[0]
    xn = _rms(x_ref[...], gmix_ref[...]).astype(BF16)
    z = _dot(xn, win_ref[...])

    u = _gelu_tanh(z[:, :GMLP_WIDTH])
    v = _gelu_tanh(z[:, GMLP_WIDTH:2 * GMLP_WIDTH])
    mu = jnp.mean(v, axis=-1, keepdims=True)
    vc = v - mu
    var = jnp.mean(vc * vc, axis=-1, keepdims=True)
    vn = (vc * lax.rsqrt(var + LN_EPS) * gv_ref[...] + bv_ref[...]).astype(BF16)
    a_chunks = []
    for c in range(tm // CHUNK):
        r0 = c * CHUNK
        cols = []
        for g in range(GMLP_HEADS):
            c0 = g * LANES
            sv = _dot(ws_ref[g], vn[r0:r0 + CHUNK, c0:c0 + LANES])
            cols.append(sv)
        sv_all = jnp.concatenate(cols, axis=1) + bsp_ref[...]
        a_chunks.append(u[r0:r0 + CHUNK, :] * sv_all)
    a = jnp.concatenate(a_chunks, axis=0)
    an_ref[...] = _rms(a, goutg_ref[...]).astype(BF16)

    cs = cs_ref[...]
    scale = 1.0 / math.sqrt(QK_NOPE + QK_ROPE)
    cqn = _rms(z[:, C_Q:C_KV], gq_ref[...]).astype(BF16)
    q = _dot(cqn, wuq_ref[...]) * scale
    ckvn = _rms(z[:, C_KV:C_KR], gkv_ref[...]).astype(BF16)
    kv = _dot(ckvn, wukv_ref[...])
    lane = lax.broadcasted_iota(jnp.int32, (tm, LANES), 1)
    kr = z[:, C_KR:IN_EXT] * cs
    kr = jnp.where(lane < QK_ROPE, kr + pltpu.roll(kr, QK_ROPE, 1), 0.0).astype(BF16)
    for h in range(MLA_HEADS):
        b0 = h * QK_PAD
        qr = q[:, b0 + LANES:b0 + QK_PAD] * cs
        qr = qr + pltpu.roll(qr, QK_ROPE, 1)
        q_ref[:, b0:b0 + LANES] = q[:, b0:b0 + LANES].astype(BF16)
        q_ref[:, b0 + LANES:b0 + QK_PAD] = qr.astype(BF16)
        k_ref[:, b0:b0 + LANES] = kv[:, b0:b0 + LANES].astype(BF16)
        k_ref[:, b0 + LANES:b0 + QK_PAD] = kr
        v_ref[:, h * V_DIM:(h + 1) * V_DIM] = kv[:, b0 + LANES:b0 + QK_PAD].astype(BF16)


def _front(x2d, gmix, win_ext, gv, bv, ws, bsp, goutg, gq, wuq_ext, gkv, wukv, cs, seq):
    t = x2d.shape[0]
    tm = TM_FRONT
    nseq = seq // tm
    row = lambda i: (i, 0)
    return pl.pallas_call(
        _front_kernel,
        grid=(t // tm,),
        in_specs=[
            pl.BlockSpec((tm, D_MODEL), row),
            _const_spec((1, D_MODEL)),
            _const_spec((D_MODEL, IN_EXT)),
            _const_spec((1, GMLP_WIDTH)),
            _const_spec((1, GMLP_WIDTH)),
            _const_spec((GMLP_HEADS, CHUNK, CHUNK)),
            _const_spec((CHUNK, GMLP_WIDTH)),
            _const_spec((1, GMLP_WIDTH)),
            _const_spec((1, Q_LORA)),
            _const_spec((Q_LORA, MLA_HEADS * QK_PAD)),
            _const_spec((1, KV_LORA)),
            _const_spec((KV_LORA, MLA_HEADS * QK_PAD)),
            pl.BlockSpec((tm, LANES), lambda i: (i % nseq, 0)),
        ],
        out_specs=[
            pl.BlockSpec((tm, GMLP_WIDTH), row),
            pl.BlockSpec((tm, MLA_HEADS * QK_PAD), row),
            pl.BlockSpec((tm, MLA_HEADS * QK_PAD), row),
            pl.BlockSpec((tm, MLA_WIDTH), row),
        ],
        out_shape=[
            jax.ShapeDtypeStruct((t, GMLP_WIDTH), BF16),
            jax.ShapeDtypeStruct((t, MLA_HEADS * QK_PAD), BF16),
            jax.ShapeDtypeStruct((t, MLA_HEADS * QK_PAD), BF16),
            jax.ShapeDtypeStruct((t, MLA_WIDTH), BF16),
        ],
        compiler_params=_cparams(("arbitrary",)),
        name="front",
    )(x2d, gmix, win_ext, gv, bv, ws, bsp, goutg, gq, wuq_ext, gkv, wukv, cs)


def _attn_kernel(q_ref, k_ref, v_ref, o_ref):
    s_len = q_ref.shape[0]
    row = lax.broadcasted_iota(jnp.int32, (TQ, TQ), 0)
    col = lax.broadcasted_iota(jnp.int32, (TQ, TQ), 1)
    causal = col <= row
    for qi in range(s_len // TQ):
        r0 = qi * TQ
        qb = q_ref[r0:r0 + TQ, :]
        sd = jnp.where(causal, _dot_nt(qb, k_ref[r0:r0 + TQ, :]), NEG)
        m = jnp.max(sd, axis=-1, keepdims=True)
        if qi > 0:
            so = _dot_nt(qb, k_ref[0:r0, :])
            m = jnp.maximum(m, jnp.max(so, axis=-1, keepdims=True))
        pd = jnp.exp(sd - m)
        l = jnp.sum(pd, axis=-1, keepdims=True)
        acc = _dot(pd.astype(BF16), v_ref[r0:r0 + TQ, :])
        if qi > 0:
            po = jnp.exp(so - m)
            l = l + jnp.sum(po, axis=-1, keepdims=True)
            acc = acc + _dot(po.astype(BF16), v_ref[0:r0, :])
        o_ref[r0:r0 + TQ, :] = (acc / l).astype(o_ref.dtype)


def _mla_attn(q, k, v, batch, seq):
    q3 = q.reshape(batch, seq, MLA_HEADS * QK_PAD)
    k3 = k.reshape(batch, seq, MLA_HEADS * QK_PAD)
    v3 = v.reshape(batch, seq, MLA_WIDTH)
    hmap = lambda b, h: (b, 0, h)
    out = pl.pallas_call(
        _attn_kernel,
        grid=(batch, MLA_HEADS),
        in_specs=[
            pl.BlockSpec((None, seq, QK_PAD), hmap),
            pl.BlockSpec((None, seq, QK_PAD), hmap),
            pl.BlockSpec((None, seq, V_DIM), hmap),
        ],
        out_specs=pl.BlockSpec((None, seq, V_DIM), hmap),
        out_shape=jax.ShapeDtypeStruct((batch, seq, MLA_WIDTH), BF16),
        compiler_params=_cparams(("arbitrary", "arbitrary")),
        name="mla_attn",
    )(q3, k3, v3)
    return out.reshape(batch * seq, MLA_WIDTH)


def _memkv_kernel(mem_ref, g_ref, wk_ref, wv_ref, k_ref, v_ref):
    mn = _rms(mem_ref[...], g_ref[...]).astype(BF16)
    k_ref[...] = _dot(mn, wk_ref[...]).astype(BF16)
    v_ref[...] = _dot(mn, wv_ref[...]).astype(BF16)


def _mem_kv(mem, g, wk, wv):
    b, m, d = mem.shape
    bmap = lambda i: (i, 0, 0)
    return pl.pallas_call(
        _memkv_kernel,
        grid=(b,),
        in_specs=[
            pl.BlockSpec((None, m, d), bmap),
            _const_spec((1, d)),
            _const_spec((d, MEM_WIDTH)),
            _const_spec((d, MEM_WIDTH)),
        ],
        out_specs=[pl.BlockSpec((None, m, MEM_WIDTH), bmap)] * 2,
        out_shape=[jax.ShapeDtypeStruct((b, m, MEM_WIDTH), BF16)] * 2,
        compiler_params=_cparams(("arbitrary",)),
        name="mem_kv",
    )(mem, g, wk, wv)


def _route(lg):
    tm = lg.shape[0]
    lane_i = lax.broadcasted_iota(jnp.int32, (tm, LANES), 1)
    lane = lane_i.astype(F32)
    big = float(LANES)
    gmask = lane_i < N_GROUPS
    gl = jnp.where(gmask, lg, NEG)
    gmax = jnp.max(gl, axis=-1, keepdims=True)
    grp = jnp.min(jnp.where(gl == gmax, lane, big), axis=-1, keepdims=True)
    gsum = jnp.sum(jnp.where(gmask, jnp.exp(gl - gmax), 0.0), axis=-1, keepdims=True)
    p_grp = 1.0 / gsum

    lo = (grp + 1.0) * EXPERTS_PER_GROUP
    emask = jnp.logical_and(lane >= lo, lane < lo + EXPERTS_PER_GROUP)
    el = jnp.where(emask, lg, NEG)
    emax = jnp.max(el, axis=-1, keepdims=True)
    ee = jnp.where(emask, jnp.exp(el - emax), 0.0)
    ep = ee / jnp.sum(ee, axis=-1, keepdims=True)
    ep = jnp.where(emask, ep, -1.0)
    p1 = jnp.max(ep, axis=-1, keepdims=True)
    i1 = jnp.min(jnp.where(ep == p1, lane, big), axis=-1, keepdims=True)
    ep2 = jnp.where(lane == i1, -1.0, ep)
    p2 = jnp.max(ep2, axis=-1, keepdims=True)
    i2 = jnp.min(jnp.where(ep2 == p2, lane, big), axis=-1, keepdims=True)
    den = p1 + p2
    g1 = p_grp * (p1 / den)
    g2 = p_grp * (p2 / den)
    e1 = (i1 - N_GROUPS).astype(jnp.int32)
    e2 = (i2 - N_GROUPS).astype(jnp.int32)
    idx = jnp.where(lane_i == 0, e1, jnp.where(lane_i == 1, e2, 0))
    gate = jnp.where(lane_i == 0, g1, jnp.where(lane_i == 1, g2, 0.0))
    return idx, gate


def _mid_kernel(an_ref, m_ref, x_ref, wout_ref, goutm_ref, gxa_ref, wmq_ref, kmem_ref, vmem_ref,
                wmo_ref, gmoe_ref, wr_ref, br_ref,
                x2_ref, hp_ref, idx_ref, gate_ref):
    mn = _rms(m_ref[...].astype(F32), goutm_ref[...]).astype(BF16)
    x1 = (x_ref[...] + _dot(an_ref[...], wout_ref[:GMLP_WIDTH, :])
          + _dot(mn, wout_ref[GMLP_WIDTH:, :]))

    h2 = _rms(x1, gxa_ref[...]).astype(BF16)
    q2 = (_dot(h2, wmq_ref[...]) * (1.0 / math.sqrt(MEM_HEAD_DIM))).astype(BF16)
    outs = []
    for h in range(MEM_HEADS):
        c0 = h * MEM_HEAD_DIM
        s = _dot_nt(q2[:, c0:c0 + MEM_HEAD_DIM], kmem_ref[:, c0:c0 + MEM_HEAD_DIM])
        m = jnp.max(s, axis=-1, keepdims=True)
        p = jnp.exp(s - m)
        l = jnp.sum(p, axis=-1, keepdims=True)
        outs.append(_dot(p.astype(BF16), vmem_ref[:, c0:c0 + MEM_HEAD_DIM]) / l)
    o = jnp.concatenate(outs, axis=1).astype(BF16)
    x2 = x1 + _dot(o, wmo_ref[...])
    x2_ref[...] = x2

    h3 = _rms(x2, gmoe_ref[...])
    half = D_MODEL // 2
    hp_ref[...] = _pack_pair(h3[:, :half], h3[:, half:])
    lg = _dot(h3.astype(BF16), wr_ref[...]) + br_ref[...]
    idx, gate = _route(lg)
    idx_ref[...] = idx
    gate_ref[...] = gate


def _mid(an, m, x2d, wout, goutm, gxa, wmq, kmem, vmem, wmo, gmoe, wr, br, seq):
    t = x2d.shape[0]
    tm = TM_MID
    nseq = seq // tm
    mem_len = kmem.shape[1]
    row = lambda i: (i, 0)
    bmap = lambda i: (i // nseq, 0, 0)
    return pl.pallas_call(
        _mid_kernel,
        grid=(t // tm,),
        in_specs=[
            pl.BlockSpec((tm, GMLP_WIDTH), row),
            pl.BlockSpec((tm, MLA_WIDTH), row),
            pl.BlockSpec((tm, D_MODEL), row),
            _const_spec((D_MODEL, D_MODEL)),
            _const_spec((1, MLA_WIDTH)),
            _const_spec((1, D_MODEL)),
            _const_spec((D_MODEL, MEM_WIDTH)),
            pl.BlockSpec((None, mem_len, MEM_WIDTH), bmap),
            pl.BlockSpec((None, mem_len, MEM_WIDTH), bmap),
            _const_spec((MEM_WIDTH, D_MODEL)),
            _const_spec((1, D_MODEL)),
            _const_spec((D_MODEL, LANES)),
            _const_spec((1, LANES)),
        ],
        out_specs=[
            pl.BlockSpec((tm, D_MODEL), row),
            pl.BlockSpec((tm, D_MODEL // 2), row),
            pl.BlockSpec((tm, LANES), row),
            pl.BlockSpec((tm, LANES), row),
        ],
        out_shape=[
            jax.ShapeDtypeStruct((t, D_MODEL), F32),
            jax.ShapeDtypeStruct((t, D_MODEL // 2), U32),
            jax.ShapeDtypeStruct((t, LANES), jnp.int32),
            jax.ShapeDtypeStruct((t, LANES), F32),
        ],
        compiler_params=_cparams(("arbitrary",)),
        name="mid",
    )(an, m, x2d, wout, goutm, gxa, wmq, kmem, vmem, wmo, gmoe, wr, br)


def _row_gather_start(src_hbm, idx_ref, dst_vmem, sem):
    for r in range(BM):
        pltpu.make_async_copy(src_hbm.at[pl.ds(idx_ref[0, 0, r], 1), :],
                              dst_vmem.at[pl.ds(r, 1), :], sem).start()


def _row_scatter_start(src_vmem, idx_ref, dst_hbm, sem):
    for r in range(BM):
        pltpu.make_async_copy(src_vmem.at[pl.ds(r, 1), :],
                              dst_hbm.at[pl.ds(idx_ref[0, 0, r], 1), :], sem).start()


def _gather_wait(src_hbm, dst_vmem, sem):
    pltpu.make_async_copy(src_hbm.at[pl.ds(0, BM), :], dst_vmem, sem).wait()


def _scatter_wait(src_vmem, dst_hbm, sem):
    pltpu.make_async_copy(src_vmem, dst_hbm.at[pl.ds(0, BM), :], sem).wait()


def _expert_kernel(be_ref, nxt_ref, nused_ref,
                   tok0_ref, tokn_ref, dstp_ref, hp_ref, wg_ref, wu_ref, wd_ref,
                   out_ref,
                   wg_st, wu_st, wd_st, wg_s, wu_s, wd_s, xbuf0, xbuf1, ybuf0, ybuf1,
                   wsem, gsem, ssem):
    i = pl.program_id(0)
    nused = nused_ref[0]
    used = i < nused
    par = i % 2
    e = be_ref[jnp.minimum(i, be_ref.shape[0] - 1)]
    first = jnp.logical_or(i == 0, e != be_ref[jnp.maximum(i - 1, 0)])

    def start_weights(ex):
        pltpu.make_async_copy(wg_ref.at[ex], wg_st, wsem.at[0]).start()
        pltpu.make_async_copy(wu_ref.at[ex], wu_st, wsem.at[1]).start()
        pltpu.make_async_copy(wd_ref.at[ex], wd_st, wsem.at[2]).start()

    @pl.when(i == 0)
    def _():
        start_weights(e)
        _row_gather_start(hp_ref, tok0_ref, xbuf0, gsem.at[0])
        ybuf1[...] = jnp.zeros(ybuf1.shape, ybuf1.dtype)

    @pl.when(jnp.logical_and(first, used))
    def _():
        pltpu.make_async_copy(wg_ref.at[0], wg_st, wsem.at[0]).wait()
        pltpu.make_async_copy(wu_ref.at[0], wu_st, wsem.at[1]).wait()
        pltpu.make_async_copy(wd_ref.at[0], wd_st, wsem.at[2]).wait()
        wg_s[...] = wg_st[...].astype(BF16)
        wu_s[...] = wu_st[...].astype(BF16)
        wd_s[...] = wd_st[...].astype(BF16)
        nxt = nxt_ref[jnp.minimum(i, nxt_ref.shape[0] - 1)]

        @pl.when(nxt >= 0)
        def _():
            start_weights(nxt)

    def step(xb, xo, yb, yo, s):
        o = 1 - s
        _gather_wait(hp_ref, xb, gsem.at[s])

        @pl.when(i >= 1)
        def _():
            _scatter_wait(yb, out_ref, ssem.at[s])

        _row_gather_start(hp_ref, tokn_ref, xo, gsem.at[o])
        _row_scatter_start(yo, dstp_ref, out_ref, ssem.at[o])
        half = D_MODEL // 2
        hi, lo = _unpack_pair(xb[...])
        hi = hi.astype(BF16)
        lo = lo.astype(BF16)
        g = _dot(hi, wg_s[:half, :]) + _dot(lo, wg_s[half:, :])
        u = _dot(hi, wu_s[:half, :]) + _dot(lo, wu_s[half:, :])
        hm = (g * jax.nn.sigmoid(g) * u).astype(BF16)
        y = _dot(hm, wd_s[...])
        yb[...] = _pack_pair(y[:, :half], y[:, half:])

    @pl.when(jnp.logical_and(used, par == 0))
    def _():
        step(xbuf0, xbuf1, ybuf0, ybuf1, 0)

    @pl.when(jnp.logical_and(used, par == 1))
    def _():
        step(xbuf1, xbuf0, ybuf1, ybuf0, 1)

    def drain(xb, yb, yo, s):
        o = 1 - s
        _gather_wait(hp_ref, xb, gsem.at[s])
        _scatter_wait(yb, out_ref, ssem.at[s])
        _row_scatter_start(yo, dstp_ref, out_ref, ssem.at[o])
        _scatter_wait(yo, out_ref, ssem.at[o])

    @pl.when(jnp.logical_and(i == nused, par == 0))
    def _():
        drain(xbuf0, ybuf0, ybuf1, 0)

    @pl.when(jnp.logical_and(i == nused, par == 1))
    def _():
        drain(xbuf1, ybuf1, ybuf0, 1)


def _experts(block_expert, next_expert, n_used, rows_tok, rows_dst, hp, wg, wu, wd, n_out_rows):
    w = hp.shape[1]
    nb = block_expert.shape[0]
    tok3 = rows_tok.reshape(nb, 1, BM)
    dst3 = rows_dst.reshape(nb + 1, 1, BM)
    smem_blk = lambda f: pl.BlockSpec((1, 1, BM), f, memory_space=pltpu.SMEM)
    last = nb - 1
    grid_spec = pltpu.PrefetchScalarGridSpec(
        num_scalar_prefetch=3,
        grid=(nb + 1,),
        in_specs=[
            smem_blk(lambda i, *_: (0, 0, 0)),
            smem_blk(lambda i, *_: (jnp.minimum(i + 1, last), 0, 0)),
            smem_blk(lambda i, *_: (i, 0, 0)),
            pl.BlockSpec(memory_space=pl.ANY),
            pl.BlockSpec(memory_space=pl.ANY),
            pl.BlockSpec(memory_space=pl.ANY),
            pl.BlockSpec(memory_space=pl.ANY),
        ],
        out_specs=pl.BlockSpec(memory_space=pl.ANY),
        scratch_shapes=[
            pltpu.VMEM((D_MODEL, D_EXPERT), F32),
            pltpu.VMEM((D_MODEL, D_EXPERT), F32),
            pltpu.VMEM((D_EXPERT, D_MODEL), F32),
            pltpu.VMEM((D_MODEL, D_EXPERT), BF16),
            pltpu.VMEM((D_MODEL, D_EXPERT), BF16),
            pltpu.VMEM((D_EXPERT, D_MODEL), BF16),
            pltpu.VMEM((BM, w), U32),
            pltpu.VMEM((BM, w), U32),
            pltpu.VMEM((BM, w), U32),
            pltpu.VMEM((BM, w), U32),
            pltpu.SemaphoreType.DMA((3,)),
            pltpu.SemaphoreType.DMA((2,)),
            pltpu.SemaphoreType.DMA((2,)),
        ],
    )
    return pl.pallas_call(
        _expert_kernel,
        grid_spec=grid_spec,
        out_shape=jax.ShapeDtypeStruct((n_out_rows, w), U32),
        compiler_params=_cparams(("arbitrary",)),
        name="experts",
    )(block_expert, next_expert, n_used, tok3, tok3, dst3, hp, wg, wu, wd)


def _combine_kernel(x2_ref, gate_ref, gfin_ref, y0_ref, y1_ref, o_ref):
    gate = gate_ref[...]
    half = D_MODEL // 2
    y_hi = x2_ref[:, :half]
    y_lo = x2_ref[:, half:]
    for k, y_ref in enumerate((y0_ref, y1_ref)):
        hi, lo = _unpack_pair(y_ref[...])
        gk = gate[:, k:k + 1]
        y_hi = y_hi + gk * hi
        y_lo = y_lo + gk * lo
    ms = (jnp.sum(y_hi * y_hi, axis=-1, keepdims=True)
          + jnp.sum(y_lo * y_lo, axis=-1, keepdims=True)) * (1.0 / D_MODEL)
    r = lax.rsqrt(ms + EPS)
    o_ref[:, :half] = y_hi * r * gfin_ref[:, :half]
    o_ref[:, half:] = y_lo * r * gfin_ref[:, half:]


def _combine(x2, gate, gfin, out2):
    t, d = x2.shape
    tm = TM_ROW
    nb = t // tm
    row = lambda i: (i, 0)
    return pl.pallas_call(
        _combine_kernel,
        grid=(nb,),
        in_specs=[
            pl.BlockSpec((tm, d), row),
            pl.BlockSpec((tm, LANES), row),
            _const_spec((1, d)),
            pl.BlockSpec((tm, d // 2), row),
            pl.BlockSpec((tm, d // 2), lambda i: (nb + i, 0)),
        ],
        out_specs=pl.BlockSpec((tm, d), row),
        out_shape=jax.ShapeDtypeStruct((t, d), F32),
        compiler_params=_cparams(("arbitrary",)),
        name="combine",
    )(x2, gate, gfin, out2, out2)


def _rot_half_cols(w):
    half = QK_ROPE // 2
    return jnp.concatenate([-w[..., half:], w[..., :half]], axis=-1)


def kernel(x, mem, g_norm_mix, w_in, g_v, b_v, w_spatial, b_spatial, g_q_lora, w_uq, g_kv_lora, w_ukv, g_out_gmlp, g_out_mla, w_out, g_norm_xattn, g_norm_mem, w_mq, w_mk, w_mv, w_mo, g_norm_moe, w_router_group, b_router_group, w_router_expert, b_router_expert, w_exp_gate, w_exp_up, w_exp_down, g_final):
    batch, seq, d = x.shape
    t = batch * seq
    x2d = x.reshape(t, d)
    r2 = lambda a: a.reshape(1, -1)

    w_kr = w_in[:, C_KR:C_KR + QK_ROPE]
    win_ext = jnp.concatenate([w_in, _rot_half_cols(w_kr)], axis=1).astype(BF16)
    wq3 = w_uq.reshape(Q_LORA, MLA_HEADS, QK_NOPE + QK_ROPE)
    wq_rope = wq3[..., QK_NOPE:]
    wuq_ext = jnp.concatenate([wq3, _rot_half_cols(wq_rope)], axis=-1)
    wuq_ext = wuq_ext.reshape(Q_LORA, MLA_HEADS * QK_PAD).astype(BF16)
    wukv = w_ukv.astype(BF16)
    causal = jnp.tril(jnp.ones((CHUNK, CHUNK), dtype=bool))
    ws = jnp.where(causal[None], w_spatial, 0.0).astype(BF16)
    bsp = jnp.repeat(b_spatial.T, LANES, axis=1)
    wr = jnp.concatenate(
        [w_router_group, w_router_expert,
         jnp.zeros((d, LANES - N_GROUPS - N_EXPERTS), F32)], axis=1).astype(BF16)
    br = jnp.concatenate(
        [b_router_group, b_router_expert, jnp.zeros((LANES - N_GROUPS - N_EXPERTS,), F32)]).reshape(1, LANES)

    pos = jnp.arange(seq, dtype=F32)
    inv_freq = ROPE_THETA ** (-jnp.arange(0, QK_ROPE, 2, dtype=F32) / QK_ROPE)
    ang = pos[:, None] * inv_freq[None, :]
    cos, sin = jnp.cos(ang), jnp.sin(ang)
    cs = jnp.concatenate([cos, cos, sin, sin], axis=1)

    an, q, k, v = _front(x2d, r2(g_norm_mix), win_ext, r2(g_v), r2(b_v), ws, bsp, r2(g_out_gmlp),
                         r2(g_q_lora), wuq_ext, r2(g_kv_lora), wukv, cs, seq)
    m = _mla_attn(q, k, v, batch, seq)
    kmem, vmem = _mem_kv(mem, r2(g_norm_mem), w_mk.astype(BF16), w_mv.astype(BF16))
    x2, hp, idx, gate = _mid(an, m, x2d, w_out.astype(BF16), r2(g_out_mla), r2(g_norm_xattn),
                             w_mq.astype(BF16), kmem, vmem, w_mo.astype(BF16), r2(g_norm_moe),
                             wr, br, seq)

    n_assign = t * TOP_K
    p_rows = n_assign + N_EXPERTS * BM
    flat_e = idx[:, :TOP_K].reshape(n_assign)
    onehot = (flat_e[:, None] == jnp.arange(N_EXPERTS, dtype=jnp.int32)[None, :]).astype(jnp.int32)
    csum = jnp.cumsum(onehot, axis=0)
    rank = jnp.sum(onehot * csum, axis=1) - 1
    counts = csum[-1]
    padded = ((counts + BM - 1) // BM) * BM
    padded_ends = jnp.cumsum(padded)
    padded_starts = padded_ends - padded
    dest = (jnp.sum(onehot * padded_starts[None, :], axis=1) + rank).astype(jnp.int32)
    block_expert = jnp.clip(
        jnp.searchsorted(padded_ends, jnp.arange(p_rows // BM, dtype=jnp.int32) * BM, side='right'),
        0, N_EXPERTS - 1).astype(jnp.int32)

    nb = p_rows // BM
    n_used = (padded_ends[-1:] // BM).astype(jnp.int32)
    nxt_blk = padded_ends[block_expert] // BM
    next_expert = jnp.where(nxt_blk < n_used[0],
                            block_expert[jnp.minimum(nxt_blk, nb - 1)], -1).astype(jnp.int32)
    rows_a = jnp.full((p_rows,), -1, jnp.int32).at[dest].set(jnp.arange(n_assign, dtype=jnp.int32))
    prow = jnp.arange(p_rows, dtype=jnp.int32)
    rows_tok = jnp.where(rows_a >= 0, rows_a >> 1, 0)
    spare = n_assign + (prow % BM)
    rows_dst = jnp.where(rows_a >= 0, (rows_a & 1) * t + (rows_a >> 1), spare)
    rows_dst = jnp.concatenate([spare[:BM], rows_dst])

    out2 = _experts(block_expert, next_expert, n_used, rows_tok, rows_dst, hp,
                    w_exp_gate, w_exp_up, w_exp_down, n_assign + BM)
    out = _combine(x2, gate, r2(g_final), out2)
    return out.reshape(batch, seq, d)
```

```python
import math

import jax
import jax.numpy as jnp
import numpy as np
from jax import lax
from jax.experimental import pallas as pl
from jax.experimental.pallas import tpu as pltpu

D_MODEL = 2048
CHUNK = 128
GMLP_HEADS = 8
GMLP_WIDTH = 1024
MLA_HEADS = 8
Q_LORA = 512
KV_LORA = 256
QK_NOPE = 128
QK_ROPE = 64
V_DIM = 128
MLA_WIDTH = MLA_HEADS * V_DIM
ROPE_THETA = 10000.0
MEM_HEADS = 4
MEM_HEAD_DIM = 128
MEM_WIDTH = MEM_HEADS * MEM_HEAD_DIM
N_GROUPS = 8
EXPERTS_PER_GROUP = 8
N_EXPERTS = 64
TOP_K = 2
D_EXPERT = 512
EPS = 1e-6
LN_EPS = 1e-5

LANES = 128
QK_PAD = 256
VMEM_LIMIT = 56 * 1024 * 1024

C_UV = 0
C_Q = 2 * GMLP_WIDTH
C_KV = C_Q + Q_LORA
C_KR = C_KV + KV_LORA
IN_EXT = C_KR + LANES

TM_FRONT = 512
TM_MID = 256
TQ = 256
BM = 256
TM_ROW = 256

F32 = jnp.float32
BF16 = jnp.bfloat16
U32 = jnp.uint32
NEG = float(np.finfo(np.float32).min)


def _cparams(sem):
    return pltpu.CompilerParams(dimension_semantics=sem, vmem_limit_bytes=VMEM_LIMIT)


def _const_spec(shape):
    n = len(shape)
    return pl.BlockSpec(shape, lambda *_: (0,) * n, pipeline_mode=pl.Buffered(1))


def _rms(x, g):
    ms = jnp.mean(x * x, axis=-1, keepdims=True)
    return x * lax.rsqrt(ms + EPS) * g


def _gelu_tanh(x):
    c = math.sqrt(2.0 / math.pi)
    return 0.5 * x * (1.0 + jnp.tanh(c * (x + 0.044715 * (x * x * x))))


def _dot(a, b):
    return jnp.dot(a, b, preferred_element_type=F32)


def _dot_nt(a, b):
    return lax.dot_general(a, b, (((1,), (1,)), ((), ())), preferred_element_type=F32)


def _pack_pair(hi, lo):
    hb = pltpu.bitcast(hi.astype(BF16).astype(F32), U32)
    lb = pltpu.bitcast(lo.astype(BF16).astype(F32), U32)
    return hb | (lb >> 16)


def _unpack_pair(w):
    hi = pltpu.bitcast(w & jnp.uint32(0xFFFF0000), F32)
    lo = pltpu.bitcast(w << 16, F32)
    return hi, lo


ROW_SUB = 8
ROW_WORDS = ROW_SUB * LANES


def _store_row_tiles(ref, packed):
    m = packed.shape[0]
    for j in range(ROW_SUB):
        ref[pl.ds(j, m, stride=ROW_SUB), :] = packed[:, j * LANES:(j + 1) * LANES]


def _load_row_tiles(ref):
    m = ref.shape[0] // ROW_SUB
    return [ref[pl.ds(j, m, stride=ROW_SUB), :] for j in range(ROW_SUB)]


def _front_kernel(x_ref, gmix_ref, win_ref, gv_ref, bv_ref, ws_ref, bsp_ref, goutg_ref,
                  gq_ref, wuq_ref, gkv_ref, wukv_ref, cs_ref,
                  an_ref, q_ref, k_ref, v_ref):
    tm = x_ref.shape[0]
    xn = _rms(x_ref[...], gmix_ref[...]).astype(BF16)
    z = _dot(xn, win_ref[...])

    u = _gelu_tanh(z[:, :GMLP_WIDTH])
    v = _gelu_tanh(z[:, GMLP_WIDTH:2 * GMLP_WIDTH])
    mu = jnp.mean(v, axis=-1, keepdims=True)
    vc = v - mu
    var = jnp.mean(vc * vc, axis=-1, keepdims=True)
    vn = (vc * lax.rsqrt(var + LN_EPS) * gv_ref[...] + bv_ref[...]).astype(BF16)
    a_chunks = []
    for c in range(tm // CHUNK):
        r0 = c * CHUNK
        cols = []
        for g in range(GMLP_HEADS):
            c0 = g * LANES
            sv = _dot(ws_ref[g], vn[r0:r0 + CHUNK, c0:c0 + LANES])
            cols.append(sv)
        sv_all = jnp.concatenate(cols, axis=1) + bsp_ref[...]
        a_chunks.append(u[r0:r0 + CHUNK, :] * sv_all)
    a = jnp.concatenate(a_chunks, axis=0)
    an_ref[...] = _rms(a, goutg_ref[...]).astype(BF16)

    cs = cs_ref[...]
    scale = 1.0 / math.sqrt(QK_NOPE + QK_ROPE)
    cqn = _rms(z[:, C_Q:C_KV], gq_ref[...]).astype(BF16)
    q = _dot(cqn, wuq_ref[...]) * scale
    ckvn = _rms(z[:, C_KV:C_KR], gkv_ref[...]).astype(BF16)
    kv = _dot(ckvn, wukv_ref[...])
    lane = lax.broadcasted_iota(jnp.int32, (tm, LANES), 1)
    kr = z[:, C_KR:IN_EXT] * cs
    kr = jnp.where(lane < QK_ROPE, kr + pltpu.roll(kr, QK_ROPE, 1), 0.0).astype(BF16)
    for h in range(MLA_HEADS):
        b0 = h * QK_PAD
        qr = q[:, b0 + LANES:b0 + QK_PAD] * cs
        qr = qr + pltpu.roll(qr, QK_ROPE, 1)
        q_ref[:, b0:b0 + LANES] = q[:, b0:b0 + LANES].astype(BF16)
        q_ref[:, b0 + LANES:b0 + QK_PAD] = qr.astype(BF16)
        k_ref[:, b0:b0 + LANES] = kv[:, b0:b0 + LANES].astype(BF16)
        k_ref[:, b0 + LANES:b0 + QK_PAD] = kr
        v_ref[:, h * V_DIM:(h + 1) * V_DIM] = kv[:, b0 + LANES:b0 + QK_PAD].astype(BF16)


def _front(x2d, gmix, win_ext, gv, bv, ws, bsp, goutg, gq, wuq_ext, gkv, wukv, cs, seq):
    t = x2d.shape[0]
    tm = TM_FRONT
    nseq = seq // tm
    row = lambda i: (i, 0)
    return pl.pallas_call(
        _front_kernel,
        grid=(t // tm,),
        in_specs=[
            pl.BlockSpec((tm, D_MODEL), row),
            _const_spec((1, D_MODEL)),
            _const_spec((D_MODEL, IN_EXT)),
            _const_spec((1, GMLP_WIDTH)),
            _const_spec((1, GMLP_WIDTH)),
            _const_spec((GMLP_HEADS, CHUNK, CHUNK)),
            _const_spec((CHUNK, GMLP_WIDTH)),
            _const_spec((1, GMLP_WIDTH)),
            _const_spec((1, Q_LORA)),
            _const_spec((Q_LORA, MLA_HEADS * QK_PAD)),
            _const_spec((1, KV_LORA)),
            _const_spec((KV_LORA, MLA_HEADS * QK_PAD)),
            pl.BlockSpec((tm, LANES), lambda i: (i % nseq, 0)),
        ],
        out_specs=[
            pl.BlockSpec((tm, GMLP_WIDTH), row),
            pl.BlockSpec((tm, MLA_HEADS * QK_PAD), row),
            pl.BlockSpec((tm, MLA_HEADS * QK_PAD), row),
            pl.BlockSpec((tm, MLA_WIDTH), row),
        ],
        out_shape=[
            jax.ShapeDtypeStruct((t, GMLP_WIDTH), BF16),
            jax.ShapeDtypeStruct((t, MLA_HEADS * QK_PAD), BF16),
            jax.ShapeDtypeStruct((t, MLA_HEADS * QK_PAD), BF16),
            jax.ShapeDtypeStruct((t, MLA_WIDTH), BF16),
        ],
        compiler_params=_cparams(("arbitrary",)),
        name="front",
    )(x2d, gmix, win_ext, gv, bv, ws, bsp, goutg, gq, wuq_ext, gkv, wukv, cs)


def _attn_kernel(q_ref, k_ref, v_ref, o_ref):
    s_len = q_ref.shape[0]
    row = lax.broadcasted_iota(jnp.int32, (TQ, TQ), 0)
    col = lax.broadcasted_iota(jnp.int32, (TQ, TQ), 1)
    causal = col <= row
    for qi in range(s_len // TQ):
        r0 = qi * TQ
        qb = q_ref[r0:r0 + TQ, :]
        sd = jnp.where(causal, _dot_nt(qb, k_ref[r0:r0 + TQ, :]), NEG)
        m = jnp.max(sd, axis=-1, keepdims=True)
        if qi > 0:
            so = _dot_nt(qb, k_ref[0:r0, :])
            m = jnp.maximum(m, jnp.max(so, axis=-1, keepdims=True))
        pd = jnp.exp(sd - m)
        l = jnp.sum(pd, axis=-1, keepdims=True)
        acc = _dot(pd.astype(BF16), v_ref[r0:r0 + TQ, :])
        if qi > 0:
            po = jnp.exp(so - m)
            l = l + jnp.sum(po, axis=-1, keepdims=True)
            acc = acc + _dot(po.astype(BF16), v_ref[0:r0, :])
        o_ref[r0:r0 + TQ, :] = (acc / l).astype(o_ref.dtype)


def _mla_attn(q, k, v, batch, seq):
    q3 = q.reshape(batch, seq, MLA_HEADS * QK_PAD)
    k3 = k.reshape(batch, seq, MLA_HEADS * QK_PAD)
    v3 = v.reshape(batch, seq, MLA_WIDTH)
    hmap = lambda b, h: (b, 0, h)
    out = pl.pallas_call(
        _attn_kernel,
        grid=(batch, MLA_HEADS),
        in_specs=[
            pl.BlockSpec((None, seq, QK_PAD), hmap),
            pl.BlockSpec((None, seq, QK_PAD), hmap),
            pl.BlockSpec((None, seq, V_DIM), hmap),
        ],
        out_specs=pl.BlockSpec((None, seq, V_DIM), hmap),
        out_shape=jax.ShapeDtypeStruct((batch, seq, MLA_WIDTH), BF16),
        compiler_params=_cparams(("arbitrary", "arbitrary")),
        name="mla_attn",
    )(q3, k3, v3)
    return out.reshape(batch * seq, MLA_WIDTH)


def _memkv_kernel(mem_ref, g_ref, wk_ref, wv_ref, k_ref, v_ref):
    mn = _rms(mem_ref[...], g_ref[...]).astype(BF16)
    k_ref[...] = _dot(mn, wk_ref[...]).astype(BF16)
    v_ref[...] = _dot(mn, wv_ref[...]).astype(BF16)


def _mem_kv(mem, g, wk, wv):
    b, m, d = mem.shape
    bmap = lambda i: (i, 0, 0)
    return pl.pallas_call(
        _memkv_kernel,
        grid=(b,),
        in_specs=[
            pl.BlockSpec((None, m, d), bmap),
            _const_spec((1, d)),
            _const_spec((d, MEM_WIDTH)),
            _const_spec((d, MEM_WIDTH)),
        ],
        out_specs=[pl.BlockSpec((None, m, MEM_WIDTH), bmap)] * 2,
        out_shape=[jax.ShapeDtypeStruct((b, m, MEM_WIDTH), BF16)] * 2,
        compiler_params=_cparams(("arbitrary",)),
        name="mem_kv",
    )(mem, g, wk, wv)


def _route(lg):
    tm = lg.shape[0]
    lane_i = lax.broadcasted_iota(jnp.int32, (tm, LANES), 1)
    lane = lane_i.astype(F32)
    big = float(LANES)
    gmask = lane_i < N_GROUPS
    gl = jnp.where(gmask, lg, NEG)
    gmax = jnp.max(gl, axis=-1, keepdims=True)
    grp = jnp.min(jnp.where(gl == gmax, lane, big), axis=-1, keepdims=True)
    gsum = jnp.sum(jnp.where(gmask, jnp.exp(gl - gmax), 0.0), axis=-1, keepdims=True)
    p_grp = 1.0 / gsum

    lo = (grp + 1.0) * EXPERTS_PER_GROUP
    emask = jnp.logical_and(lane >= lo, lane < lo + EXPERTS_PER_GROUP)
    el = jnp.where(emask, lg, NEG)
    emax = jnp.max(el, axis=-1, keepdims=True)
    ee = jnp.where(emask, jnp.exp(el - emax), 0.0)
    ep = ee / jnp.sum(ee, axis=-1, keepdims=True)
    ep = jnp.where(emask, ep, -1.0)
    p1 = jnp.max(ep, axis=-1, keepdims=True)
    i1 = jnp.min(jnp.where(ep == p1, lane, big), axis=-1, keepdims=True)
    ep2 = jnp.where(lane == i1, -1.0, ep)
    p2 = jnp.max(ep2, axis=-1, keepdims=True)
    i2 = jnp.min(jnp.where(ep2 == p2, lane, big), axis=-1, keepdims=True)
    den = p1 + p2
    g1 = p_grp * (p1 / den)
    g2 = p_grp * (p2 / den)
    e1 = (i1 - N_GROUPS).astype(jnp.int32)
    e2 = (i2 - N_GROUPS).astype(jnp.int32)
    idx = jnp.where(lane_i == 0, e1, jnp.where(lane_i == 1, e2, 0))
    gate = jnp.where(lane_i == 0, g1, jnp.where(lane_i == 1, g2, 0.0))
    return idx, gate


def _mid_kernel(an_ref, m_ref, x_ref, wout_ref, goutm_ref, gxa_ref, wmq_ref, kmem_ref, vmem_ref,
                wmo_ref, gmoe_ref, wr_ref, br_ref,
                x2_ref, hp_ref, idx_ref, gate_ref):
    mn = _rms(m_ref[...].astype(F32), goutm_ref[...]).astype(BF16)
    x1 = (x_ref[...] + _dot(an_ref[...], wout_ref[:GMLP_WIDTH, :])
          + _dot(mn, wout_ref[GMLP_WIDTH:, :]))

    h2 = _rms(x1, gxa_ref[...]).astype(BF16)
    q2 = (_dot(h2, wmq_ref[...]) * (1.0 / math.sqrt(MEM_HEAD_DIM))).astype(BF16)
    outs = []
    for h in range(MEM_HEADS):
        c0 = h * MEM_HEAD_DIM
        s = _dot_nt(q2[:, c0:c0 + MEM_HEAD_DIM], kmem_ref[:, c0:c0 + MEM_HEAD_DIM])
        m = jnp.max(s, axis=-1, keepdims=True)
        p = jnp.exp(s - m)
        l = jnp.sum(p, axis=-1, keepdims=True)
        outs.append(_dot(p.astype(BF16), vmem_ref[:, c0:c0 + MEM_HEAD_DIM]) / l)
    o = jnp.concatenate(outs, axis=1).astype(BF16)
    x2 = x1 + _dot(o, wmo_ref[...])
    x2_ref[...] = x2

    h3 = _rms(x2, gmoe_ref[...])
    half = D_MODEL // 2
    _store_row_tiles(hp_ref, _pack_pair(h3[:, :half], h3[:, half:]))
    lg = _dot(h3.astype(BF16), wr_ref[...]) + br_ref[...]
    idx, gate = _route(lg)
    idx_ref[...] = idx
    gate_ref[...] = gate


def _mid(an, m, x2d, wout, goutm, gxa, wmq, kmem, vmem, wmo, gmoe, wr, br, seq):
    t = x2d.shape[0]
    tm = TM_MID
    nseq = seq // tm
    mem_len = kmem.shape[1]
    row = lambda i: (i, 0)
    bmap = lambda i: (i // nseq, 0, 0)
    return pl.pallas_call(
        _mid_kernel,
        grid=(t // tm,),
        in_specs=[
            pl.BlockSpec((tm, GMLP_WIDTH), row),
            pl.BlockSpec((tm, MLA_WIDTH), row),
            pl.BlockSpec((tm, D_MODEL), row),
            _const_spec((D_MODEL, D_MODEL)),
            _const_spec((1, MLA_WIDTH)),
            _const_spec((1, D_MODEL)),
            _const_spec((D_MODEL, MEM_WIDTH)),
            pl.BlockSpec((None, mem_len, MEM_WIDTH), bmap),
            pl.BlockSpec((None, mem_len, MEM_WIDTH), bmap),
            _const_spec((MEM_WIDTH, D_MODEL)),
            _const_spec((1, D_MODEL)),
            _const_spec((D_MODEL, LANES)),
            _const_spec((1, LANES)),
        ],
        out_specs=[
            pl.BlockSpec((tm, D_MODEL), row),
            pl.BlockSpec((tm * ROW_SUB, LANES), row),
            pl.BlockSpec((tm, LANES), row),
            pl.BlockSpec((tm, LANES), row),
        ],
        out_shape=[
            jax.ShapeDtypeStruct((t, D_MODEL), F32),
            jax.ShapeDtypeStruct((t * ROW_SUB, LANES), U32),
            jax.ShapeDtypeStruct((t, LANES), jnp.int32),
            jax.ShapeDtypeStruct((t, LANES), F32),
        ],
        compiler_params=_cparams(("arbitrary",)),
        name="mid",
    )(an, m, x2d, wout, goutm, gxa, wmq, kmem, vmem, wmo, gmoe, wr, br)


def _row_gather_start(src_hbm, idx_ref, dst_vmem, sem):
    for r in range(BM):
        s0 = pl.multiple_of(idx_ref[0, 0, r], ROW_SUB)
        pltpu.make_async_copy(src_hbm.at[pl.ds(s0, ROW_SUB), :],
                              dst_vmem.at[pl.ds(r * ROW_SUB, ROW_SUB), :], sem).start()


def _row_scatter_start(src_vmem, idx_ref, dst_hbm, sem):
    for r in range(BM):
        d0 = pl.multiple_of(idx_ref[0, 0, r], ROW_SUB)
        pltpu.make_async_copy(src_vmem.at[pl.ds(r * ROW_SUB, ROW_SUB), :],
                              dst_hbm.at[pl.ds(d0, ROW_SUB), :], sem).start()


def _gather_wait(src_hbm, dst_vmem, sem):
    pltpu.make_async_copy(src_hbm.at[pl.ds(0, BM * ROW_SUB), :], dst_vmem, sem).wait()


def _scatter_wait(src_vmem, dst_hbm, sem):
    pltpu.make_async_copy(src_vmem, dst_hbm.at[pl.ds(0, BM * ROW_SUB), :], sem).wait()


def _cast_pair_rows(dst_s, src_st):
    half = D_MODEL // 2
    for j in range(ROW_SUB):
        a = j * LANES
        dst_s[2 * a:2 * a + LANES, :] = src_st[a:a + LANES, :].astype(BF16)
        dst_s[2 * a + LANES:2 * a + 2 * LANES, :] = src_st[half + a:half + a + LANES, :].astype(BF16)


def _expert_kernel(be_ref, nxt_ref, nused_ref,
                   tok0_ref, tokn_ref, dstp_ref, hp_ref, wg_ref, wu_ref, wd_ref,
                   out_ref,
                   wg_st, wu_st, wd_st, wg_s, wu_s, wd_s, xbuf0, xbuf1, ybuf0, ybuf1,
                   wsem, gsem, ssem):
    i = pl.program_id(0)
    nused = nused_ref[0]
    used = i < nused
    par = i % 2
    e = be_ref[jnp.minimum(i, be_ref.shape[0] - 1)]
    first = jnp.logical_or(i == 0, e != be_ref[jnp.maximum(i - 1, 0)])

    def start_weights(ex):
        pltpu.make_async_copy(wg_ref.at[ex], wg_st, wsem.at[0]).start()
        pltpu.make_async_copy(wu_ref.at[ex], wu_st, wsem.at[1]).start()
        pltpu.make_async_copy(wd_ref.at[ex], wd_st, wsem.at[2]).start()

    @pl.when(i == 0)
    def _():
        start_weights(e)
        _row_gather_start(hp_ref, tok0_ref, xbuf0, gsem.at[0])
        ybuf1[...] = jnp.zeros(ybuf1.shape, ybuf1.dtype)

    @pl.when(jnp.logical_and(first, used))
    def _():
        pltpu.make_async_copy(wg_ref.at[0], wg_st, wsem.at[0]).wait()
        pltpu.make_async_copy(wu_ref.at[0], wu_st, wsem.at[1]).wait()
        pltpu.make_async_copy(wd_ref.at[0], wd_st, wsem.at[2]).wait()
        _cast_pair_rows(wg_s, wg_st)
        _cast_pair_rows(wu_s, wu_st)
        wd_s[...] = wd_st[...].astype(BF16)
        nxt = nxt_ref[jnp.minimum(i, nxt_ref.shape[0] - 1)]

        @pl.when(nxt >= 0)
        def _():
            start_weights(nxt)

    def step(xb, xo, yb, yo, s):
        o = 1 - s
        _gather_wait(hp_ref, xb, gsem.at[s])

        @pl.when(i >= 1)
        def _():
            _scatter_wait(yb, out_ref, ssem.at[s])

        _row_gather_start(hp_ref, tokn_ref, xo, gsem.at[o])
        _row_scatter_start(yo, dstp_ref, out_ref, ssem.at[o])
        half = D_MODEL // 2
        cols = []
        for w in _load_row_tiles(xb):
            hi, lo = _unpack_pair(w)
            cols += [hi.astype(BF16), lo.astype(BF16)]
        xrow = jnp.concatenate(cols, axis=1)
        g = _dot(xrow, wg_s[...])
        u = _dot(xrow, wu_s[...])
        hm = (g * jax.nn.sigmoid(g) * u).astype(BF16)
        y = _dot(hm, wd_s[...])
        _store_row_tiles(yb, _pack_pair(y[:, :half], y[:, half:]))

    @pl.when(jnp.logical_and(used, par == 0))
    def _():
        step(xbuf0, xbuf1, ybuf0, ybuf1, 0)

    @pl.when(jnp.logical_and(used, par == 1))
    def _():
        step(xbuf1, xbuf0, ybuf1, ybuf0, 1)

    def drain(xb, yb, yo, s):
        o = 1 - s
        _gather_wait(hp_ref, xb, gsem.at[s])
        _scatter_wait(yb, out_ref, ssem.at[s])
        _row_scatter_start(yo, dstp_ref, out_ref, ssem.at[o])
        _scatter_wait(yo, out_ref, ssem.at[o])

    @pl.when(jnp.logical_and(i == nused, par == 0))
    def _():
        drain(xbuf0, ybuf0, ybuf1, 0)

    @pl.when(jnp.logical_and(i == nused, par == 1))
    def _():
        drain(xbuf1, ybuf1, ybuf0, 1)


def _experts(block_expert, next_expert, n_used, rows_tok, rows_dst, hp, wg, wu, wd, n_out_rows):
    nb = block_expert.shape[0]
    tok3 = rows_tok.reshape(nb, 1, BM)
    dst3 = rows_dst.reshape(nb + 1, 1, BM)
    smem_blk = lambda f: pl.BlockSpec((1, 1, BM), f, memory_space=pltpu.SMEM)
    last = nb - 1
    grid_spec = pltpu.PrefetchScalarGridSpec(
        num_scalar_prefetch=3,
        grid=(nb + 1,),
        in_specs=[
            smem_blk(lambda i, *_: (0, 0, 0)),
            smem_blk(lambda i, *_: (jnp.minimum(i + 1, last), 0, 0)),
            smem_blk(lambda i, *_: (i, 0, 0)),
            pl.BlockSpec(memory_space=pl.ANY),
            pl.BlockSpec(memory_space=pl.ANY),
            pl.BlockSpec(memory_space=pl.ANY),
            pl.BlockSpec(memory_space=pl.ANY),
        ],
        out_specs=pl.BlockSpec(memory_space=pl.ANY),
        scratch_shapes=[
            pltpu.VMEM((D_MODEL, D_EXPERT), F32),
            pltpu.VMEM((D_MODEL, D_EXPERT), F32),
            pltpu.VMEM((D_EXPERT, D_MODEL), F32),
            pltpu.VMEM((D_MODEL, D_EXPERT), BF16),
            pltpu.VMEM((D_MODEL, D_EXPERT), BF16),
            pltpu.VMEM((D_EXPERT, D_MODEL), BF16),
            pltpu.VMEM((BM * ROW_SUB, LANES), U32),
            pltpu.VMEM((BM * ROW_SUB, LANES), U32),
            pltpu.VMEM((BM * ROW_SUB, LANES), U32),
            pltpu.VMEM((BM * ROW_SUB, LANES), U32),
            pltpu.SemaphoreType.DMA((3,)),
            pltpu.SemaphoreType.DMA((2,)),
            pltpu.SemaphoreType.DMA((2,)),
        ],
    )
    return pl.pallas_call(
        _expert_kernel,
        grid_spec=grid_spec,
        out_shape=jax.ShapeDtypeStruct((n_out_rows * ROW_SUB, LANES), U32),
        compiler_params=_cparams(("arbitrary",)),
        name="experts",
    )(block_expert, next_expert, n_used, tok3, tok3, dst3, hp, wg, wu, wd)


def _combine_kernel(x2_ref, gate_ref, gfin_ref, y0_ref, y1_ref, o_ref):
    gate = gate_ref[...]
    half = D_MODEL // 2
    y_hi = x2_ref[:, :half]
    y_lo = x2_ref[:, half:]
    for k, y_ref in enumerate((y0_ref, y1_ref)):
        pairs = [_unpack_pair(w) for w in _load_row_tiles(y_ref)]
        hi = jnp.concatenate([p[0] for p in pairs], axis=1)
        lo = jnp.concatenate([p[1] for p in pairs], axis=1)
        gk = gate[:, k:k + 1]
        y_hi = y_hi + gk * hi
        y_lo = y_lo + gk * lo
    ms = (jnp.sum(y_hi * y_hi, axis=-1, keepdims=True)
          + jnp.sum(y_lo * y_lo, axis=-1, keepdims=True)) * (1.0 / D_MODEL)
    r = lax.rsqrt(ms + EPS)
    o_ref[:, :half] = y_hi * r * gfin_ref[:, :half]
    o_ref[:, half:] = y_lo * r * gfin_ref[:, half:]


def _combine(x2, gate, gfin, out2):
    t, d = x2.shape
    tm = TM_ROW
    nb = t // tm
    row = lambda i: (i, 0)
    return pl.pallas_call(
        _combine_kernel,
        grid=(nb,),
        in_specs=[
            pl.BlockSpec((tm, d), row),
            pl.BlockSpec((tm, LANES), row),
            _const_spec((1, d)),
            pl.BlockSpec((tm * ROW_SUB, LANES), row),
            pl.BlockSpec((tm * ROW_SUB, LANES), lambda i: (nb + i, 0)),
        ],
        out_specs=pl.BlockSpec((tm, d), row),
        out_shape=jax.ShapeDtypeStruct((t, d), F32),
        compiler_params=_cparams(("arbitrary",)),
        name="combine",
    )(x2, gate, gfin, out2, out2)


def _rot_half_cols(w):
    half = QK_ROPE // 2
    return jnp.concatenate([-w[..., half:], w[..., :half]], axis=-1)


def kernel(x, mem, g_norm_mix, w_in, g_v, b_v, w_spatial, b_spatial, g_q_lora, w_uq, g_kv_lora, w_ukv, g_out_gmlp, g_out_mla, w_out, g_norm_xattn, g_norm_mem, w_mq, w_mk, w_mv, w_mo, g_norm_moe, w_router_group, b_router_group, w_router_expert, b_router_expert, w_exp_gate, w_exp_up, w_exp_down, g_final):
    batch, seq, d = x.shape
    t = batch * seq
    x2d = x.reshape(t, d)
    r2 = lambda a: a.reshape(1, -1)

    w_kr = w_in[:, C_KR:C_KR + QK_ROPE]
    win_ext = jnp.concatenate([w_in, _rot_half_cols(w_kr)], axis=1).astype(BF16)
    wq3 = w_uq.reshape(Q_LORA, MLA_HEADS, QK_NOPE + QK_ROPE)
    wq_rope = wq3[..., QK_NOPE:]
    wuq_ext = jnp.concatenate([wq3, _rot_half_cols(wq_rope)], axis=-1)
    wuq_ext = wuq_ext.reshape(Q_LORA, MLA_HEADS * QK_PAD).astype(BF16)
    wukv = w_ukv.astype(BF16)
    causal = jnp.tril(jnp.ones((CHUNK, CHUNK), dtype=bool))
    ws = jnp.where(causal[None], w_spatial, 0.0).astype(BF16)
    bsp = jnp.repeat(b_spatial.T, LANES, axis=1)
    wr = jnp.concatenate(
        [w_router_group, w_router_expert,
         jnp.zeros((d, LANES - N_GROUPS - N_EXPERTS), F32)], axis=1).astype(BF16)
    br = jnp.concatenate(
        [b_router_group, b_router_expert, jnp.zeros((LANES - N_GROUPS - N_EXPERTS,), F32)]).reshape(1, LANES)

    pos = jnp.arange(seq, dtype=F32)
    inv_freq = ROPE_THETA ** (-jnp.arange(0, QK_ROPE, 2, dtype=F32) / QK_ROPE)
    ang = pos[:, None] * inv_freq[None, :]
    cos, sin = jnp.cos(ang), jnp.sin(ang)
    cs = jnp.concatenate([cos, cos, sin, sin], axis=1)

    an, q, k, v = _front(x2d, r2(g_norm_mix), win_ext, r2(g_v), r2(b_v), ws, bsp, r2(g_out_gmlp),
                         r2(g_q_lora), wuq_ext, r2(g_kv_lora), wukv, cs, seq)
    m = _mla_attn(q, k, v, batch, seq)
    kmem, vmem = _mem_kv(mem, r2(g_norm_mem), w_mk.astype(BF16), w_mv.astype(BF16))
    x2, hp, idx, gate = _mid(an, m, x2d, w_out.astype(BF16), r2(g_out_mla), r2(g_norm_xattn),
                             w_mq.astype(BF16), kmem, vmem, w_mo.astype(BF16), r2(g_norm_moe),
                             wr, br, seq)

    n_assign = t * TOP_K
    p_rows = n_assign + N_EXPERTS * BM
    flat_e = idx[:, :TOP_K].reshape(n_assign)
    onehot = (flat_e[:, None] == jnp.arange(N_EXPERTS, dtype=jnp.int32)[None, :]).astype(jnp.int32)
    csum = jnp.cumsum(onehot, axis=0)
    rank = jnp.sum(onehot * csum, axis=1) - 1
    counts = csum[-1]
    padded = ((counts + BM - 1) // BM) * BM
    padded_ends = jnp.cumsum(padded)
    padded_starts = padded_ends - padded
    dest = (jnp.sum(onehot * padded_starts[None, :], axis=1) + rank).astype(jnp.int32)
    block_expert = jnp.clip(
        jnp.searchsorted(padded_ends, jnp.arange(p_rows // BM, dtype=jnp.int32) * BM, side='right'),
        0, N_EXPERTS - 1).astype(jnp.int32)

    nb = p_rows // BM
    n_used = (padded_ends[-1:] // BM).astype(jnp.int32)
    nxt_blk = padded_ends[block_expert] // BM
    next_expert = jnp.where(nxt_blk < n_used[0],
                            block_expert[jnp.minimum(nxt_blk, nb - 1)], -1).astype(jnp.int32)
    rows_a = jnp.full((p_rows,), -1, jnp.int32).at[dest].set(jnp.arange(n_assign, dtype=jnp.int32))
    prow = jnp.arange(p_rows, dtype=jnp.int32)
    rows_tok = jnp.where(rows_a >= 0, rows_a >> 1, 0)
    spare = n_assign + (prow % BM)
    rows_dst = jnp.where(rows_a >= 0, (rows_a & 1) * t + (rows_a >> 1), spare)
    rows_dst = jnp.concatenate([spare[:BM], rows_dst])

    out2 = _experts(block_expert, next_expert, n_used, rows_tok * ROW_SUB, rows_dst * ROW_SUB, hp,
                    w_exp_gate, w_exp_up, w_exp_down, n_assign + BM)
    out = _combine(x2, gate, r2(g_final), out2)
    return out.reshape(batch, seq, d)
```

```python
import math

import jax
import jax.numpy as jnp
import numpy as np
from jax import lax
from jax.experimental import pallas as pl
from jax.experimental.pallas import tpu as pltpu

D_MODEL = 2048
CHUNK = 128
GMLP_HEADS = 8
GMLP_WIDTH = 1024
MLA_HEADS = 8
Q_LORA = 512
KV_LORA = 256
QK_NOPE = 128
QK_ROPE = 64
V_DIM = 128
MLA_WIDTH = MLA_HEADS * V_DIM
ROPE_THETA = 10000.0
MEM_HEADS = 4
MEM_HEAD_DIM = 128
MEM_WIDTH = MEM_HEADS * MEM_HEAD_DIM
N_GROUPS = 8
EXPERTS_PER_GROUP = 8
N_EXPERTS = 64
TOP_K = 2
D_EXPERT = 512
EPS = 1e-6
LN_EPS = 1e-5

LANES = 128
QK_PAD = 256
VMEM_LIMIT = 56 * 1024 * 1024

C_UV = 0
C_Q = 2 * GMLP_WIDTH
C_KV = C_Q + Q_LORA
C_KR = C_KV + KV_LORA
IN_EXT = C_KR + LANES

TM_FRONT = 512
TM_MID = 256
TQ = 256
BM = 256
TM_ROW = 256
TM_DISPATCH = 512

F32 = jnp.float32
BF16 = jnp.bfloat16
U32 = jnp.uint32
NEG = float(np.finfo(np.float32).min)


def _cparams(sem):
    return pltpu.CompilerParams(dimension_semantics=sem, vmem_limit_bytes=VMEM_LIMIT)


def _const_spec(shape):
    n = len(shape)
    return pl.BlockSpec(shape, lambda *_: (0,) * n, pipeline_mode=pl.Buffered(1))


def _rms(x, g):
    ms = jnp.mean(x * x, axis=-1, keepdims=True)
    return x * lax.rsqrt(ms + EPS) * g


def _gelu_tanh(x):
    c = math.sqrt(2.0 / math.pi)
    return 0.5 * x * (1.0 + jnp.tanh(c * (x + 0.044715 * (x * x * x))))


def _dot(a, b):
    return jnp.dot(a, b, preferred_element_type=F32)


def _dot_nt(a, b):
    return lax.dot_general(a, b, (((1,), (1,)), ((), ())), preferred_element_type=F32)


def _pack_pair(hi, lo):
    hb = pltpu.bitcast(hi.astype(BF16).astype(F32), U32)
    lb = pltpu.bitcast(lo.astype(BF16).astype(F32), U32)
    return hb | (lb >> 16)


def _unpack_pair(w):
    hi = pltpu.bitcast(w & jnp.uint32(0xFFFF0000), F32)
    lo = pltpu.bitcast(w << 16, F32)
    return hi, lo


ROW_SUB = 8
ROW_WORDS = ROW_SUB * LANES


def _store_row_tiles(ref, packed):
    m = packed.shape[0]
    for j in range(ROW_SUB):
        ref[pl.ds(j, m, stride=ROW_SUB), :] = packed[:, j * LANES:(j + 1) * LANES]


def _load_row_tiles(ref):
    m = ref.shape[0] // ROW_SUB
    return [ref[pl.ds(j, m, stride=ROW_SUB), :] for j in range(ROW_SUB)]


def _front_kernel(x_ref, gmix_ref, win_ref, gv_ref, bv_ref, ws_ref, bsp_ref, goutg_ref,
                  gq_ref, wuq_ref, gkv_ref, wukv_ref, cs_ref,
                  an_ref, q_ref, k_ref, v_ref):
    tm = x_ref.shape[0]
    xn = _rms(x_ref[...], gmix_ref[...]).astype(BF16)
    z = _dot(xn, win_ref[...])

    u = _gelu_tanh(z[:, :GMLP_WIDTH])
    v = _gelu_tanh(z[:, GMLP_WIDTH:2 * GMLP_WIDTH])
    mu = jnp.mean(v, axis=-1, keepdims=True)
    vc = v - mu
    var = jnp.mean(vc * vc, axis=-1, keepdims=True)
    vn = (vc * lax.rsqrt(var + LN_EPS) * gv_ref[...] + bv_ref[...]).astype(BF16)
    a_chunks = []
    for c in range(tm // CHUNK):
        r0 = c * CHUNK
        cols = []
        for g in range(GMLP_HEADS):
            c0 = g * LANES
            sv = _dot(ws_ref[g], vn[r0:r0 + CHUNK, c0:c0 + LANES])
            cols.append(sv)
        sv_all = jnp.concatenate(cols, axis=1) + bsp_ref[...]
        a_chunks.append(u[r0:r0 + CHUNK, :] * sv_all)
    a = jnp.concatenate(a_chunks, axis=0)
    an_ref[...] = _rms(a, goutg_ref[...]).astype(BF16)

    cs = cs_ref[...]
    scale = 1.0 / math.sqrt(QK_NOPE + QK_ROPE)
    cqn = _rms(z[:, C_Q:C_KV], gq_ref[...]).astype(BF16)
    q = _dot(cqn, wuq_ref[...]) * scale
    ckvn = _rms(z[:, C_KV:C_KR], gkv_ref[...]).astype(BF16)
    kv = _dot(ckvn, wukv_ref[...])
    lane = lax.broadcasted_iota(jnp.int32, (tm, LANES), 1)
    kr = z[:, C_KR:IN_EXT] * cs
    kr = jnp.where(lane < QK_ROPE, kr + pltpu.roll(kr, QK_ROPE, 1), 0.0).astype(BF16)
    for h in range(MLA_HEADS):
        b0 = h * QK_PAD
        qr = q[:, b0 + LANES:b0 + QK_PAD] * cs
        qr = qr + pltpu.roll(qr, QK_ROPE, 1)
        q_ref[:, b0:b0 + LANES] = q[:, b0:b0 + LANES].astype(BF16)
        q_ref[:, b0 + LANES:b0 + QK_PAD] = qr.astype(BF16)
        k_ref[:, b0:b0 + LANES] = kv[:, b0:b0 + LANES].astype(BF16)
        k_ref[:, b0 + LANES:b0 + QK_PAD] = kr
        v_ref[:, h * V_DIM:(h + 1) * V_DIM] = kv[:, b0 + LANES:b0 + QK_PAD].astype(BF16)


def _front(x2d, gmix, win_ext, gv, bv, ws, bsp, goutg, gq, wuq_ext, gkv, wukv, cs, seq):
    t = x2d.shape[0]
    tm = TM_FRONT
    nseq = seq // tm
    row = lambda i: (i, 0)
    return pl.pallas_call(
        _front_kernel,
        grid=(t // tm,),
        in_specs=[
            pl.BlockSpec((tm, D_MODEL), row),
            _const_spec((1, D_MODEL)),
            _const_spec((D_MODEL, IN_EXT)),
            _const_spec((1, GMLP_WIDTH)),
            _const_spec((1, GMLP_WIDTH)),
            _const_spec((GMLP_HEADS, CHUNK, CHUNK)),
            _const_spec((CHUNK, GMLP_WIDTH)),
            _const_spec((1, GMLP_WIDTH)),
            _const_spec((1, Q_LORA)),
            _const_spec((Q_LORA, MLA_HEADS * QK_PAD)),
            _const_spec((1, KV_LORA)),
            _const_spec((KV_LORA, MLA_HEADS * QK_PAD)),
            pl.BlockSpec((tm, LANES), lambda i: (i % nseq, 0)),
        ],
        out_specs=[
            pl.BlockSpec((tm, GMLP_WIDTH), row),
            pl.BlockSpec((tm, MLA_HEADS * QK_PAD), row),
            pl.BlockSpec((tm, MLA_HEADS * QK_PAD), row),
            pl.BlockSpec((tm, MLA_WIDTH), row),
        ],
        out_shape=[
            jax.ShapeDtypeStruct((t, GMLP_WIDTH), BF16),
            jax.ShapeDtypeStruct((t, MLA_HEADS * QK_PAD), BF16),
            jax.ShapeDtypeStruct((t, MLA_HEADS * QK_PAD), BF16),
            jax.ShapeDtypeStruct((t, MLA_WIDTH), BF16),
        ],
        compiler_params=_cparams(("arbitrary",)),
        name="front",
    )(x2d, gmix, win_ext, gv, bv, ws, bsp, goutg, gq, wuq_ext, gkv, wukv, cs)


def _attn_kernel(q_ref, k_ref, v_ref, o_ref):
    s_len = q_ref.shape[0]
    row = lax.broadcasted_iota(jnp.int32, (TQ, TQ), 0)
    col = lax.broadcasted_iota(jnp.int32, (TQ, TQ), 1)
    causal = col <= row
    for qi in range(s_len // TQ):
        r0 = qi * TQ
        qb = q_ref[r0:r0 + TQ, :]
        sd = jnp.where(causal, _dot_nt(qb, k_ref[r0:r0 + TQ, :]), NEG)
        m = jnp.max(sd, axis=-1, keepdims=True)
        if qi > 0:
            so = _dot_nt(qb, k_ref[0:r0, :])
            m = jnp.maximum(m, jnp.max(so, axis=-1, keepdims=True))
        pd = jnp.exp(sd - m)
        l = jnp.sum(pd, axis=-1, keepdims=True)
        acc = _dot(pd.astype(BF16), v_ref[r0:r0 + TQ, :])
        if qi > 0:
            po = jnp.exp(so - m)
            l = l + jnp.sum(po, axis=-1, keepdims=True)
            acc = acc + _dot(po.astype(BF16), v_ref[0:r0, :])
        o_ref[r0:r0 + TQ, :] = (acc / l).astype(o_ref.dtype)


def _mla_attn(q, k, v, batch, seq):
    q3 = q.reshape(batch, seq, MLA_HEADS * QK_PAD)
    k3 = k.reshape(batch, seq, MLA_HEADS * QK_PAD)
    v3 = v.reshape(batch, seq, MLA_WIDTH)
    hmap = lambda b, h: (b, 0, h)
    out = pl.pallas_call(
        _attn_kernel,
        grid=(batch, MLA_HEADS),
        in_specs=[
            pl.BlockSpec((None, seq, QK_PAD), hmap),
            pl.BlockSpec((None, seq, QK_PAD), hmap),
            pl.BlockSpec((None, seq, V_DIM), hmap),
        ],
        out_specs=pl.BlockSpec((None, seq, V_DIM), hmap),
        out_shape=jax.ShapeDtypeStruct((batch, seq, MLA_WIDTH), BF16),
        compiler_params=_cparams(("arbitrary", "arbitrary")),
        name="mla_attn",
    )(q3, k3, v3)
    return out.reshape(batch * seq, MLA_WIDTH)


def _memkv_kernel(mem_ref, g_ref, wk_ref, wv_ref, k_ref, v_ref):
    mn = _rms(mem_ref[...], g_ref[...]).astype(BF16)
    k_ref[...] = _dot(mn, wk_ref[...]).astype(BF16)
    v_ref[...] = _dot(mn, wv_ref[...]).astype(BF16)


def _mem_kv(mem, g, wk, wv):
    b, m, d = mem.shape
    bmap = lambda i: (i, 0, 0)
    return pl.pallas_call(
        _memkv_kernel,
        grid=(b,),
        in_specs=[
            pl.BlockSpec((None, m, d), bmap),
            _const_spec((1, d)),
            _const_spec((d, MEM_WIDTH)),
            _const_spec((d, MEM_WIDTH)),
        ],
        out_specs=[pl.BlockSpec((None, m, MEM_WIDTH), bmap)] * 2,
        out_shape=[jax.ShapeDtypeStruct((b, m, MEM_WIDTH), BF16)] * 2,
        compiler_params=_cparams(("arbitrary",)),
        name="mem_kv",
    )(mem, g, wk, wv)


def _route(lg):
    tm = lg.shape[0]
    lane_i = lax.broadcasted_iota(jnp.int32, (tm, LANES), 1)
    lane = lane_i.astype(F32)
    big = float(LANES)
    gmask = lane_i < N_GROUPS
    gl = jnp.where(gmask, lg, NEG)
    gmax = jnp.max(gl, axis=-1, keepdims=True)
    grp = jnp.min(jnp.where(gl == gmax, lane, big), axis=-1, keepdims=True)
    gsum = jnp.sum(jnp.where(gmask, jnp.exp(gl - gmax), 0.0), axis=-1, keepdims=True)
    p_grp = 1.0 / gsum

    lo = (grp + 1.0) * EXPERTS_PER_GROUP
    emask = jnp.logical_and(lane >= lo, lane < lo + EXPERTS_PER_GROUP)
    el = jnp.where(emask, lg, NEG)
    emax = jnp.max(el, axis=-1, keepdims=True)
    ee = jnp.where(emask, jnp.exp(el - emax), 0.0)
    ep = ee / jnp.sum(ee, axis=-1, keepdims=True)
    ep = jnp.where(emask, ep, -1.0)
    p1 = jnp.max(ep, axis=-1, keepdims=True)
    i1 = jnp.min(jnp.where(ep == p1, lane, big), axis=-1, keepdims=True)
    ep2 = jnp.where(lane == i1, -1.0, ep)
    p2 = jnp.max(ep2, axis=-1, keepdims=True)
    i2 = jnp.min(jnp.where(ep2 == p2, lane, big), axis=-1, keepdims=True)
    den = p1 + p2
    g1 = p_grp * (p1 / den)
    g2 = p_grp * (p2 / den)
    e1 = (i1 - N_GROUPS).astype(jnp.int32)
    e2 = (i2 - N_GROUPS).astype(jnp.int32)
    idx = jnp.where(lane_i == 0, e1, jnp.where(lane_i == 1, e2, 0))
    gate = jnp.where(lane_i == 0, g1, jnp.where(lane_i == 1, g2, 0.0))
    return idx, gate


def _mid_kernel(an_ref, m_ref, x_ref, wout_ref, goutm_ref, gxa_ref, wmq_ref, kmem_ref, vmem_ref,
                wmo_ref, gmoe_ref, wr_ref, br_ref,
                x2_ref, hp_ref, idx_ref, gate_ref):
    mn = _rms(m_ref[...].astype(F32), goutm_ref[...]).astype(BF16)
    x1 = (x_ref[...] + _dot(an_ref[...], wout_ref[:GMLP_WIDTH, :])
          + _dot(mn, wout_ref[GMLP_WIDTH:, :]))

    h2 = _rms(x1, gxa_ref[...]).astype(BF16)
    q2 = (_dot(h2, wmq_ref[...]) * (1.0 / math.sqrt(MEM_HEAD_DIM))).astype(BF16)
    outs = []
    for h in range(MEM_HEADS):
        c0 = h * MEM_HEAD_DIM
        s = _dot_nt(q2[:, c0:c0 + MEM_HEAD_DIM], kmem_ref[:, c0:c0 + MEM_HEAD_DIM])
        m = jnp.max(s, axis=-1, keepdims=True)
        p = jnp.exp(s - m)
        l = jnp.sum(p, axis=-1, keepdims=True)
        outs.append(_dot(p.astype(BF16), vmem_ref[:, c0:c0 + MEM_HEAD_DIM]) / l)
    o = jnp.concatenate(outs, axis=1).astype(BF16)
    x2 = x1 + _dot(o, wmo_ref[...])
    x2_ref[...] = x2

    h3 = _rms(x2, gmoe_ref[...])
    half = D_MODEL // 2
    _store_row_tiles(hp_ref, _pack_pair(h3[:, :half], h3[:, half:]))
    lg = _dot(h3.astype(BF16), wr_ref[...]) + br_ref[...]
    idx, gate = _route(lg)
    idx_ref[...] = idx
    gate_ref[...] = gate


def _mid(an, m, x2d, wout, goutm, gxa, wmq, kmem, vmem, wmo, gmoe, wr, br, seq):
    t = x2d.shape[0]
    tm = TM_MID
    nseq = seq // tm
    mem_len = kmem.shape[1]
    row = lambda i: (i, 0)
    bmap = lambda i: (i // nseq, 0, 0)
    return pl.pallas_call(
        _mid_kernel,
        grid=(t // tm,),
        in_specs=[
            pl.BlockSpec((tm, GMLP_WIDTH), row),
            pl.BlockSpec((tm, MLA_WIDTH), row),
            pl.BlockSpec((tm, D_MODEL), row),
            _const_spec((D_MODEL, D_MODEL)),
            _const_spec((1, MLA_WIDTH)),
            _const_spec((1, D_MODEL)),
            _const_spec((D_MODEL, MEM_WIDTH)),
            pl.BlockSpec((None, mem_len, MEM_WIDTH), bmap),
            pl.BlockSpec((None, mem_len, MEM_WIDTH), bmap),
            _const_spec((MEM_WIDTH, D_MODEL)),
            _const_spec((1, D_MODEL)),
            _const_spec((D_MODEL, LANES)),
            _const_spec((1, LANES)),
        ],
        out_specs=[
            pl.BlockSpec((tm, D_MODEL), row),
            pl.BlockSpec((tm * ROW_SUB, LANES), row),
            pl.BlockSpec((tm, LANES), row),
            pl.BlockSpec((tm, LANES), row),
        ],
        out_shape=[
            jax.ShapeDtypeStruct((t, D_MODEL), F32),
            jax.ShapeDtypeStruct((t * ROW_SUB, LANES), U32),
            jax.ShapeDtypeStruct((t, LANES), jnp.int32),
            jax.ShapeDtypeStruct((t, LANES), F32),
        ],
        compiler_params=_cparams(("arbitrary",)),
        name="mid",
    )(an, m, x2d, wout, goutm, gxa, wmq, kmem, vmem, wmo, gmoe, wr, br)


def _dispatch_kernel(dest_ref, hp_ref, xs_in_ref, xs_ref, sem):
    del xs_in_ref
    tm = hp_ref.shape[0] // ROW_SUB
    for t in range(tm):
        for k in range(TOP_K):
            d0 = pl.multiple_of(dest_ref[0, 0, t * TOP_K + k], ROW_SUB)
            pltpu.make_async_copy(hp_ref.at[pl.ds(t * ROW_SUB, ROW_SUB), :],
                                  xs_ref.at[pl.ds(d0, ROW_SUB), :], sem).start()
    for k in range(TOP_K):
        pltpu.make_async_copy(hp_ref, xs_ref.at[pl.ds(0, tm * ROW_SUB), :], sem).wait()


def _dispatch(dest8, hp, xs0):
    tm = TM_DISPATCH
    nb = hp.shape[0] // (tm * ROW_SUB)
    dest3 = dest8.reshape(nb, 1, tm * TOP_K)
    return pl.pallas_call(
        _dispatch_kernel,
        grid=(nb,),
        in_specs=[
            pl.BlockSpec((1, 1, tm * TOP_K), lambda i: (i, 0, 0), memory_space=pltpu.SMEM),
            pl.BlockSpec((tm * ROW_SUB, LANES), lambda i: (i, 0)),
            pl.BlockSpec(memory_space=pl.ANY),
        ],
        out_specs=pl.BlockSpec(memory_space=pl.ANY),
        out_shape=jax.ShapeDtypeStruct(xs0.shape, xs0.dtype),
        scratch_shapes=[pltpu.SemaphoreType.DMA(())],
        input_output_aliases={2: 0},
        compiler_params=_cparams(("arbitrary",)),
        name="dispatch",
    )(dest3, hp, xs0)


def _row_scatter_start(src_vmem, idx_ref, dst_hbm, sem):
    for r in range(BM):
        d0 = pl.multiple_of(idx_ref[0, 0, r], ROW_SUB)
        pltpu.make_async_copy(src_vmem.at[pl.ds(r * ROW_SUB, ROW_SUB), :],
                              dst_hbm.at[pl.ds(d0, ROW_SUB), :], sem).start()


def _scatter_wait(src_vmem, dst_hbm, sem):
    pltpu.make_async_copy(src_vmem, dst_hbm.at[pl.ds(0, BM * ROW_SUB), :], sem).wait()


def _cast_pair_rows(dst_s, src_st):
    half = D_MODEL // 2
    for j in range(ROW_SUB):
        a = j * LANES
        dst_s[2 * a:2 * a + LANES, :] = src_st[a:a + LANES, :].astype(BF16)
        dst_s[2 * a + LANES:2 * a + 2 * LANES, :] = src_st[half + a:half + a + LANES, :].astype(BF16)


def _expert_kernel(be_ref, nxt_ref, nused_ref,
                   dstp_ref, xs_ref, wg_ref, wu_ref, wd_ref,
                   out_ref,
                   wg_st, wu_st, wd_st, wg_s, wu_s, wd_s, ybuf0, ybuf1,
                   wsem, ssem):
    i = pl.program_id(0)
    nused = nused_ref[0]
    used = i < nused
    par = i % 2
    e = be_ref[jnp.minimum(i, be_ref.shape[0] - 1)]
    first = jnp.logical_or(i == 0, e != be_ref[jnp.maximum(i - 1, 0)])

    def start_weights(ex):
        pltpu.make_async_copy(wg_ref.at[ex], wg_st, wsem.at[0]).start()
        pltpu.make_async_copy(wu_ref.at[ex], wu_st, wsem.at[1]).start()
        pltpu.make_async_copy(wd_ref.at[ex], wd_st, wsem.at[2]).start()

    @pl.when(i == 0)
    def _():
        start_weights(e)
        ybuf1[...] = jnp.zeros(ybuf1.shape, ybuf1.dtype)

    @pl.when(jnp.logical_and(first, used))
    def _():
        pltpu.make_async_copy(wg_ref.at[0], wg_st, wsem.at[0]).wait()
        pltpu.make_async_copy(wu_ref.at[0], wu_st, wsem.at[1]).wait()
        pltpu.make_async_copy(wd_ref.at[0], wd_st, wsem.at[2]).wait()
        _cast_pair_rows(wg_s, wg_st)
        _cast_pair_rows(wu_s, wu_st)
        wd_s[...] = wd_st[...].astype(BF16)
        nxt = nxt_ref[jnp.minimum(i, nxt_ref.shape[0] - 1)]

        @pl.when(nxt >= 0)
        def _():
            start_weights(nxt)

    def step(yb, yo, s):
        o = 1 - s

        @pl.when(i >= 1)
        def _():
            _scatter_wait(yb, out_ref, ssem.at[s])

        _row_scatter_start(yo, dstp_ref, out_ref, ssem.at[o])
        half = D_MODEL // 2
        cols = []
        for w in _load_row_tiles(xs_ref):
            hi, lo = _unpack_pair(w)
            cols += [hi.astype(BF16), lo.astype(BF16)]
        xrow = jnp.concatenate(cols, axis=1)
        g = _dot(xrow, wg_s[...])
        u = _dot(xrow, wu_s[...])
        hm = (g * jax.nn.sigmoid(g) * u).astype(BF16)
        y = _dot(hm, wd_s[...])
        _store_row_tiles(yb, _pack_pair(y[:, :half], y[:, half:]))

    @pl.when(jnp.logical_and(used, par == 0))
    def _():
        step(ybuf0, ybuf1, 0)

    @pl.when(jnp.logical_and(used, par == 1))
    def _():
        step(ybuf1, ybuf0, 1)

    def drain(yb, yo, s):
        o = 1 - s
        _scatter_wait(yb, out_ref, ssem.at[s])
        _row_scatter_start(yo, dstp_ref, out_ref, ssem.at[o])
        _scatter_wait(yo, out_ref, ssem.at[o])

    @pl.when(jnp.logical_and(i == nused, par == 0))
    def _():
        drain(ybuf0, ybuf1, 0)

    @pl.when(jnp.logical_and(i == nused, par == 1))
    def _():
        drain(ybuf1, ybuf0, 1)


def _experts(block_expert, next_expert, n_used, rows_dst, xs, wg, wu, wd, n_out_rows):
    nb = block_expert.shape[0]
    dst3 = rows_dst.reshape(nb + 1, 1, BM)
    last = nb - 1
    grid_spec = pltpu.PrefetchScalarGridSpec(
        num_scalar_prefetch=3,
        grid=(nb + 1,),
        in_specs=[
            pl.BlockSpec((1, 1, BM), lambda i, *_: (i, 0, 0), memory_space=pltpu.SMEM),
            pl.BlockSpec((BM * ROW_SUB, LANES), lambda i, *_: (jnp.minimum(i, last), 0)),
            pl.BlockSpec(memory_space=pl.ANY),
            pl.BlockSpec(memory_space=pl.ANY),
            pl.BlockSpec(memory_space=pl.ANY),
        ],
        out_specs=pl.BlockSpec(memory_space=pl.ANY),
        scratch_shapes=[
            pltpu.VMEM((D_MODEL, D_EXPERT), F32),
            pltpu.VMEM((D_MODEL, D_EXPERT), F32),
            pltpu.VMEM((D_EXPERT, D_MODEL), F32),
            pltpu.VMEM((D_MODEL, D_EXPERT), BF16),
            pltpu.VMEM((D_MODEL, D_EXPERT), BF16),
            pltpu.VMEM((D_EXPERT, D_MODEL), BF16),
            pltpu.VMEM((BM * ROW_SUB, LANES), U32),
            pltpu.VMEM((BM * ROW_SUB, LANES), U32),
            pltpu.SemaphoreType.DMA((3,)),
            pltpu.SemaphoreType.DMA((2,)),
        ],
    )
    return pl.pallas_call(
        _expert_kernel,
        grid_spec=grid_spec,
        out_shape=jax.ShapeDtypeStruct((n_out_rows * ROW_SUB, LANES), U32),
        compiler_params=_cparams(("arbitrary",)),
        name="experts",
    )(block_expert, next_expert, n_used, dst3, xs, wg, wu, wd)


def _combine_kernel(x2_ref, gate_ref, gfin_ref, y0_ref, y1_ref, o_ref):
    gate = gate_ref[...]
    half = D_MODEL // 2
    y_hi = x2_ref[:, :half]
    y_lo = x2_ref[:, half:]
    for k, y_ref in enumerate((y0_ref, y1_ref)):
        pairs = [_unpack_pair(w) for w in _load_row_tiles(y_ref)]
        hi = jnp.concatenate([p[0] for p in pairs], axis=1)
        lo = jnp.concatenate([p[1] for p in pairs], axis=1)
        gk = gate[:, k:k + 1]
        y_hi = y_hi + gk * hi
        y_lo = y_lo + gk * lo
    ms = (jnp.sum(y_hi * y_hi, axis=-1, keepdims=True)
          + jnp.sum(y_lo * y_lo, axis=-1, keepdims=True)) * (1.0 / D_MODEL)
    r = lax.rsqrt(ms + EPS)
    o_ref[:, :half] = y_hi * r * gfin_ref[:, :half]
    o_ref[:, half:] = y_lo * r * gfin_ref[:, half:]


def _combine(x2, gate, gfin, out2):
    t, d = x2.shape
    tm = TM_ROW
    nb = t // tm
    row = lambda i: (i, 0)
    return pl.pallas_call(
        _combine_kernel,
        grid=(nb,),
        in_specs=[
            pl.BlockSpec((tm, d), row),
            pl.BlockSpec((tm, LANES), row),
            _const_spec((1, d)),
            pl.BlockSpec((tm * ROW_SUB, LANES), row),
            pl.BlockSpec((tm * ROW_SUB, LANES), lambda i: (nb + i, 0)),
        ],
        out_specs=pl.BlockSpec((tm, d), row),
        out_shape=jax.ShapeDtypeStruct((t, d), F32),
        compiler_params=_cparams(("arbitrary",)),
        name="combine",
    )(x2, gate, gfin, out2, out2)


def _rot_half_cols(w):
    half = QK_ROPE // 2
    return jnp.concatenate([-w[..., half:], w[..., :half]], axis=-1)


def kernel(x, mem, g_norm_mix, w_in, g_v, b_v, w_spatial, b_spatial, g_q_lora, w_uq, g_kv_lora, w_ukv, g_out_gmlp, g_out_mla, w_out, g_norm_xattn, g_norm_mem, w_mq, w_mk, w_mv, w_mo, g_norm_moe, w_router_group, b_router_group, w_router_expert, b_router_expert, w_exp_gate, w_exp_up, w_exp_down, g_final):
    batch, seq, d = x.shape
    t = batch * seq
    x2d = x.reshape(t, d)
    r2 = lambda a: a.reshape(1, -1)

    w_kr = w_in[:, C_KR:C_KR + QK_ROPE]
    win_ext = jnp.concatenate([w_in, _rot_half_cols(w_kr)], axis=1).astype(BF16)
    wq3 = w_uq.reshape(Q_LORA, MLA_HEADS, QK_NOPE + QK_ROPE)
    wq_rope = wq3[..., QK_NOPE:]
    wuq_ext = jnp.concatenate([wq3, _rot_half_cols(wq_rope)], axis=-1)
    wuq_ext = wuq_ext.reshape(Q_LORA, MLA_HEADS * QK_PAD).astype(BF16)
    wukv = w_ukv.astype(BF16)
    causal = jnp.tril(jnp.ones((CHUNK, CHUNK), dtype=bool))
    ws = jnp.where(causal[None], w_spatial, 0.0).astype(BF16)
    bsp = jnp.repeat(b_spatial.T, LANES, axis=1)
    wr = jnp.concatenate(
        [w_router_group, w_router_expert,
         jnp.zeros((d, LANES - N_GROUPS - N_EXPERTS), F32)], axis=1).astype(BF16)
    br = jnp.concatenate(
        [b_router_group, b_router_expert, jnp.zeros((LANES - N_GROUPS - N_EXPERTS,), F32)]).reshape(1, LANES)

    pos = jnp.arange(seq, dtype=F32)
    inv_freq = ROPE_THETA ** (-jnp.arange(0, QK_ROPE, 2, dtype=F32) / QK_ROPE)
    ang = pos[:, None] * inv_freq[None, :]
    cos, sin = jnp.cos(ang), jnp.sin(ang)
    cs = jnp.concatenate([cos, cos, sin, sin], axis=1)

    an, q, k, v = _front(x2d, r2(g_norm_mix), win_ext, r2(g_v), r2(b_v), ws, bsp, r2(g_out_gmlp),
                         r2(g_q_lora), wuq_ext, r2(g_kv_lora), wukv, cs, seq)
    m = _mla_attn(q, k, v, batch, seq)
    kmem, vmem = _mem_kv(mem, r2(g_norm_mem), w_mk.astype(BF16), w_mv.astype(BF16))
    x2, hp, idx, gate = _mid(an, m, x2d, w_out.astype(BF16), r2(g_out_mla), r2(g_norm_xattn),
                             w_mq.astype(BF16), kmem, vmem, w_mo.astype(BF16), r2(g_norm_moe),
                             wr, br, seq)

    n_assign = t * TOP_K
    p_rows = n_assign + N_EXPERTS * BM
    flat_e = idx[:, :TOP_K].reshape(n_assign)
    onehot = (flat_e[:, None] == jnp.arange(N_EXPERTS, dtype=jnp.int32)[None, :]).astype(jnp.int32)
    csum = jnp.cumsum(onehot, axis=0)
    rank = jnp.sum(onehot * csum, axis=1) - 1
    counts = csum[-1]
    padded = ((counts + BM - 1) // BM) * BM
    padded_ends = jnp.cumsum(padded)
    padded_starts = padded_ends - padded
    dest = (jnp.sum(onehot * padded_starts[None, :], axis=1) + rank).astype(jnp.int32)
    block_expert = jnp.clip(
        jnp.searchsorted(padded_ends, jnp.arange(p_rows // BM, dtype=jnp.int32) * BM, side='right'),
        0, N_EXPERTS - 1).astype(jnp.int32)

    nb = p_rows // BM
    n_used = (padded_ends[-1:] // BM).astype(jnp.int32)
    nxt_blk = padded_ends[block_expert] // BM
    next_expert = jnp.where(nxt_blk < n_used[0],
                            block_expert[jnp.minimum(nxt_blk, nb - 1)], -1).astype(jnp.int32)
    rows_a = jnp.full((p_rows,), -1, jnp.int32).at[dest].set(jnp.arange(n_assign, dtype=jnp.int32))
    prow = jnp.arange(p_rows, dtype=jnp.int32)
    spare = n_assign + (prow % BM)
    rows_dst = jnp.where(rows_a >= 0, (rows_a & 1) * t + (rows_a >> 1), spare)
    rows_dst = jnp.concatenate([spare[:BM], rows_dst])

    xs = _dispatch(dest * ROW_SUB, hp, jnp.zeros((p_rows * ROW_SUB, LANES), U32))
    out2 = _experts(block_expert, next_expert, n_used, rows_dst * ROW_SUB, xs,
                    w_exp_gate, w_exp_up, w_exp_down, n_assign + BM)
    out = _combine(x2, gate, r2(g_final), out2)
    return out.reshape(batch, seq, d)
```

```python
import math

import jax
import jax.numpy as jnp
import numpy as np
from jax import lax
from jax.experimental import pallas as pl
from jax.experimental.pallas import tpu as pltpu

D_MODEL = 2048
CHUNK = 128
GMLP_HEADS = 8
GMLP_WIDTH = 1024
MLA_HEADS = 8
Q_LORA = 512
KV_LORA = 256
QK_NOPE = 128
QK_ROPE = 64
V_DIM = 128
MLA_WIDTH = MLA_HEADS * V_DIM
ROPE_THETA = 10000.0
MEM_HEADS = 4
MEM_HEAD_DIM = 128
MEM_WIDTH = MEM_HEADS * MEM_HEAD_DIM
N_GROUPS = 8
EXPERTS_PER_GROUP = 8
N_EXPERTS = 64
TOP_K = 2
D_EXPERT = 512
EPS = 1e-6
LN_EPS = 1e-5

LANES = 128
QK_PAD = 256
VMEM_LIMIT = 56 * 1024 * 1024

C_UV = 0
C_Q = 2 * GMLP_WIDTH
C_KV = C_Q + Q_LORA
C_KR = C_KV + KV_LORA
SMEM_1D_TILE = 1024

TM_FRONT = 512
TM_MID = 256
TQ = 256
BM = 256
TM_ROW = 256
TM_DISPATCH = 512

F32 = jnp.float32
BF16 = jnp.bfloat16
U32 = jnp.uint32
NEG = float(np.finfo(np.float32).min)


def _cparams(sem):
    return pltpu.CompilerParams(dimension_semantics=sem, vmem_limit_bytes=VMEM_LIMIT)


def _const_spec(shape):
    n = len(shape)
    return pl.BlockSpec(shape, lambda *_: (0,) * n, pipeline_mode=pl.Buffered(1))


def _rms(x, g):
    ms = jnp.mean(x * x, axis=-1, keepdims=True)
    return x * lax.rsqrt(ms + EPS) * g


def _gelu_tanh(x):
    c = math.sqrt(2.0 / math.pi)
    return 0.5 * x * (1.0 + jnp.tanh(c * (x + 0.044715 * (x * x * x))))


def _dot(a, b):
    return jnp.dot(a, b, preferred_element_type=F32)


def _dot_nt(a, b):
    return lax.dot_general(a, b, (((1,), (1,)), ((), ())), preferred_element_type=F32)


def _pack_pair(hi, lo):
    hb = pltpu.bitcast(hi.astype(BF16).astype(F32), U32)
    lb = pltpu.bitcast(lo.astype(BF16).astype(F32), U32)
    return hb | (lb >> 16)


def _unpack_pair(w):
    hi = pltpu.bitcast(w & jnp.uint32(0xFFFF0000), F32)
    lo = pltpu.bitcast(w << 16, F32)
    return hi, lo


ROW_SUB = 8
ROW_WORDS = ROW_SUB * LANES


def _store_row_tiles(ref, packed):
    m = packed.shape[0]
    for j in range(ROW_SUB):
        ref[pl.ds(j, m, stride=ROW_SUB), :] = packed[:, j * LANES:(j + 1) * LANES]


def _load_row_tiles(ref):
    m = ref.shape[0] // ROW_SUB
    return [ref[pl.ds(j, m, stride=ROW_SUB), :] for j in range(ROW_SUB)]


def _front_kernel(x_ref, gmix_ref, win_ref, wkr_ref, gv_ref, bv_ref, ws_ref, bsp_ref, goutg_ref,
                  gq_ref, wuq_ref, gkv_ref, wukv_ref, cs_ref,
                  an_ref, q_ref, k_ref, v_ref):
    tm = x_ref.shape[0]
    xn = _rms(x_ref[...], gmix_ref[...]).astype(BF16)
    z = _dot(xn, win_ref[...])
    zk = _dot(xn, wkr_ref[...])

    u = _gelu_tanh(z[:, :GMLP_WIDTH])
    v = _gelu_tanh(z[:, GMLP_WIDTH:2 * GMLP_WIDTH])
    mu = jnp.mean(v, axis=-1, keepdims=True)
    vc = v - mu
    var = jnp.mean(vc * vc, axis=-1, keepdims=True)
    vn = (vc * lax.rsqrt(var + LN_EPS) * gv_ref[...] + bv_ref[...]).astype(BF16)
    a_chunks = []
    for c in range(tm // CHUNK):
        r0 = c * CHUNK
        cols = []
        for g in range(GMLP_HEADS):
            c0 = g * LANES
            sv = _dot(ws_ref[g], vn[r0:r0 + CHUNK, c0:c0 + LANES])
            cols.append(sv)
        sv_all = jnp.concatenate(cols, axis=1) + bsp_ref[...]
        a_chunks.append(u[r0:r0 + CHUNK, :] * sv_all)
    a = jnp.concatenate(a_chunks, axis=0)
    an_ref[...] = _rms(a, goutg_ref[...]).astype(BF16)

    cs = cs_ref[...]
    scale = 1.0 / math.sqrt(QK_NOPE + QK_ROPE)
    cqn = _rms(z[:, C_Q:C_KV], gq_ref[...]).astype(BF16)
    q = _dot(cqn, wuq_ref[...]) * scale
    ckvn = _rms(z[:, C_KV:C_KR], gkv_ref[...]).astype(BF16)
    kv = _dot(ckvn, wukv_ref[...])
    lane = lax.broadcasted_iota(jnp.int32, (tm, LANES), 1)
    kr = zk * cs
    kr = jnp.where(lane < QK_ROPE, kr + pltpu.roll(kr, QK_ROPE, 1), 0.0).astype(BF16)
    for h in range(MLA_HEADS):
        b0 = h * QK_PAD
        qr = q[:, b0 + LANES:b0 + QK_PAD] * cs
        qr = qr + pltpu.roll(qr, QK_ROPE, 1)
        q_ref[:, b0:b0 + LANES] = q[:, b0:b0 + LANES].astype(BF16)
        q_ref[:, b0 + LANES:b0 + QK_PAD] = qr.astype(BF16)
        k_ref[:, b0:b0 + LANES] = kv[:, b0:b0 + LANES].astype(BF16)
        k_ref[:, b0 + LANES:b0 + QK_PAD] = kr
        v_ref[:, h * V_DIM:(h + 1) * V_DIM] = kv[:, b0 + LANES:b0 + QK_PAD].astype(BF16)


def _front(x2d, gmix, win, wkr, gv, bv, ws, bsp, goutg, gq, wuq_ext, gkv, wukv, cs, seq):
    t = x2d.shape[0]
    tm = TM_FRONT
    nseq = seq // tm
    row = lambda i: (i, 0)
    return pl.pallas_call(
        _front_kernel,
        grid=(t // tm,),
        in_specs=[
            pl.BlockSpec((tm, D_MODEL), row),
            _const_spec((1, D_MODEL)),
            _const_spec((D_MODEL, C_KR)),
            _const_spec((D_MODEL, LANES)),
            _const_spec((1, GMLP_WIDTH)),
            _const_spec((1, GMLP_WIDTH)),
            _const_spec((GMLP_HEADS, CHUNK, CHUNK)),
            _const_spec((CHUNK, GMLP_WIDTH)),
            _const_spec((1, GMLP_WIDTH)),
            _const_spec((1, Q_LORA)),
            _const_spec((Q_LORA, MLA_HEADS * QK_PAD)),
            _const_spec((1, KV_LORA)),
            _const_spec((KV_LORA, MLA_HEADS * QK_PAD)),
            pl.BlockSpec((tm, LANES), lambda i: (i % nseq, 0)),
        ],
        out_specs=[
            pl.BlockSpec((tm, GMLP_WIDTH), row),
            pl.BlockSpec((tm, MLA_HEADS * QK_PAD), row),
            pl.BlockSpec((tm, MLA_HEADS * QK_PAD), row),
            pl.BlockSpec((tm, MLA_WIDTH), row),
        ],
        out_shape=[
            jax.ShapeDtypeStruct((t, GMLP_WIDTH), BF16),
            jax.ShapeDtypeStruct((t, MLA_HEADS * QK_PAD), BF16),
            jax.ShapeDtypeStruct((t, MLA_HEADS * QK_PAD), BF16),
            jax.ShapeDtypeStruct((t, MLA_WIDTH), BF16),
        ],
        compiler_params=_cparams(("arbitrary",)),
        name="front",
    )(x2d, gmix, win, wkr, gv, bv, ws, bsp, goutg, gq, wuq_ext, gkv, wukv, cs)


def _attn_kernel(q_ref, k_ref, v_ref, o_ref):
    s_len = q_ref.shape[0]
    row = lax.broadcasted_iota(jnp.int32, (TQ, TQ), 0)
    col = lax.broadcasted_iota(jnp.int32, (TQ, TQ), 1)
    causal = col <= row
    for qi in range(s_len // TQ):
        r0 = qi * TQ
        qb = q_ref[r0:r0 + TQ, :]
        sd = jnp.where(causal, _dot_nt(qb, k_ref[r0:r0 + TQ, :]), NEG)
        m = jnp.max(sd, axis=-1, keepdims=True)
        if qi > 0:
            so = _dot_nt(qb, k_ref[0:r0, :])
            m = jnp.maximum(m, jnp.max(so, axis=-1, keepdims=True))
        pd = jnp.exp(sd - m)
        l = jnp.sum(pd, axis=-1, keepdims=True)
        acc = _dot(pd.astype(BF16), v_ref[r0:r0 + TQ, :])
        if qi > 0:
            po = jnp.exp(so - m)
            l = l + jnp.sum(po, axis=-1, keepdims=True)
            acc = acc + _dot(po.astype(BF16), v_ref[0:r0, :])
        o_ref[r0:r0 + TQ, :] = (acc / l).astype(o_ref.dtype)


def _mla_attn(q, k, v, batch, seq):
    q3 = q.reshape(batch, seq, MLA_HEADS * QK_PAD)
    k3 = k.reshape(batch, seq, MLA_HEADS * QK_PAD)
    v3 = v.reshape(batch, seq, MLA_WIDTH)
    hmap = lambda b, h: (b, 0, h)
    out = pl.pallas_call(
        _attn_kernel,
        grid=(batch, MLA_HEADS),
        in_specs=[
            pl.BlockSpec((None, seq, QK_PAD), hmap),
            pl.BlockSpec((None, seq, QK_PAD), hmap),
            pl.BlockSpec((None, seq, V_DIM), hmap),
        ],
        out_specs=pl.BlockSpec((None, seq, V_DIM), hmap),
        out_shape=jax.ShapeDtypeStruct((batch, seq, MLA_WIDTH), BF16),
        compiler_params=_cparams(("arbitrary", "arbitrary")),
        name="mla_attn",
    )(q3, k3, v3)
    return out.reshape(batch * seq, MLA_WIDTH)


def _memkv_kernel(mem_ref, g_ref, wk_ref, wv_ref, k_ref, v_ref):
    mn = _rms(mem_ref[...], g_ref[...]).astype(BF16)
    k_ref[...] = _dot(mn, wk_ref[...]).astype(BF16)
    v_ref[...] = _dot(mn, wv_ref[...]).astype(BF16)


def _mem_kv(mem, g, wk, wv):
    b, m, d = mem.shape
    bmap = lambda i: (i, 0, 0)
    return pl.pallas_call(
        _memkv_kernel,
        grid=(b,),
        in_specs=[
            pl.BlockSpec((None, m, d), bmap),
            _const_spec((1, d)),
            _const_spec((d, MEM_WIDTH)),
            _const_spec((d, MEM_WIDTH)),
        ],
        out_specs=[pl.BlockSpec((None, m, MEM_WIDTH), bmap)] * 2,
        out_shape=[jax.ShapeDtypeStruct((b, m, MEM_WIDTH), BF16)] * 2,
        compiler_params=_cparams(("arbitrary",)),
        name="mem_kv",
    )(mem, g, wk, wv)


def _route(lg, carry):
    tm = lg.shape[0]
    lane_i = lax.broadcasted_iota(jnp.int32, (tm, LANES), 1)
    lane = lane_i.astype(F32)
    big = float(LANES)
    gmask = lane_i < N_GROUPS
    gl = jnp.where(gmask, lg, NEG)
    gmax = jnp.max(gl, axis=-1, keepdims=True)
    grp = jnp.min(jnp.where(gl == gmax, lane, big), axis=-1, keepdims=True)
    gsum = jnp.sum(jnp.where(gmask, jnp.exp(gl - gmax), 0.0), axis=-1, keepdims=True)
    p_grp = 1.0 / gsum

    lo = (grp + 1.0) * EXPERTS_PER_GROUP
    emask = jnp.logical_and(lane >= lo, lane < lo + EXPERTS_PER_GROUP)
    el = jnp.where(emask, lg, NEG)
    emax = jnp.max(el, axis=-1, keepdims=True)
    ee = jnp.where(emask, jnp.exp(el - emax), 0.0)
    ep = ee / jnp.sum(ee, axis=-1, keepdims=True)
    ep = jnp.where(emask, ep, -1.0)
    p1 = jnp.max(ep, axis=-1, keepdims=True)
    i1 = jnp.min(jnp.where(ep == p1, lane, big), axis=-1, keepdims=True)
    ep2 = jnp.where(lane == i1, -1.0, ep)
    p2 = jnp.max(ep2, axis=-1, keepdims=True)
    i2 = jnp.min(jnp.where(ep2 == p2, lane, big), axis=-1, keepdims=True)
    den = p1 + p2
    g1 = p_grp * (p1 / den)
    g2 = p_grp * (p2 / den)
    e1 = i1 - N_GROUPS
    e2 = i2 - N_GROUPS
    gate = jnp.where(lane_i == 0, g1, jnp.where(lane_i == 1, g2, 0.0))

    oh1 = (lane == e1).astype(F32)
    oh2 = (lane == e2).astype(F32)
    both = oh1 + oh2
    r_i = lax.broadcasted_iota(jnp.int32, (tm, tm), 0)
    c_i = lax.broadcasted_iota(jnp.int32, (tm, tm), 1)
    before = (c_i < r_i).astype(BF16)
    base = carry + _dot(before, both.astype(BF16))
    r1 = jnp.sum(oh1 * base, axis=-1, keepdims=True)
    r2 = jnp.sum(oh2 * base, axis=-1, keepdims=True)
    new_carry = carry + jnp.sum(both, axis=0, keepdims=True)
    idx = jnp.where(lane_i == 0, e1, jnp.where(lane_i == 1, e2,
                    jnp.where(lane_i == 2, r1, jnp.where(lane_i == 3, r2, 0.0)))).astype(jnp.int32)
    return idx, gate, new_carry


def _mid_kernel(an_ref, m_ref, x_ref, wout_ref, goutm_ref, gxa_ref, wmq_ref, kmem_ref, vmem_ref,
                wmo_ref, gmoe_ref, wr_ref, br_ref,
                x2_ref, hp_ref, idx_ref, gate_ref, cnt_ref, carry_ref):
    @pl.when(pl.program_id(0) == 0)
    def _():
        carry_ref[...] = jnp.zeros(carry_ref.shape, carry_ref.dtype)

    mn = _rms(m_ref[...].astype(F32), goutm_ref[...]).astype(BF16)
    x1 = (x_ref[...] + _dot(an_ref[...], wout_ref[:GMLP_WIDTH, :])
          + _dot(mn, wout_ref[GMLP_WIDTH:, :]))

    h2 = _rms(x1, gxa_ref[...]).astype(BF16)
    q2 = (_dot(h2, wmq_ref[...]) * (1.0 / math.sqrt(MEM_HEAD_DIM))).astype(BF16)
    outs = []
    for h in range(MEM_HEADS):
        c0 = h * MEM_HEAD_DIM
        s = _dot_nt(q2[:, c0:c0 + MEM_HEAD_DIM], kmem_ref[:, c0:c0 + MEM_HEAD_DIM])
        m = jnp.max(s, axis=-1, keepdims=True)
        p = jnp.exp(s - m)
        l = jnp.sum(p, axis=-1, keepdims=True)
        outs.append(_dot(p.astype(BF16), vmem_ref[:, c0:c0 + MEM_HEAD_DIM]) / l)
    o = jnp.concatenate(outs, axis=1).astype(BF16)
    x2 = x1 + _dot(o, wmo_ref[...])
    x2_ref[...] = x2

    h3 = _rms(x2, gmoe_ref[...])
    half = D_MODEL // 2
    _store_row_tiles(hp_ref, _pack_pair(h3[:, :half], h3[:, half:]))
    lg = _dot(h3.astype(BF16), wr_ref[...]) + br_ref[...]
    idx, gate, carry = _route(lg, carry_ref[...])
    idx_ref[...] = idx
    gate_ref[...] = gate
    carry_ref[...] = carry
    cnt_ref[...] = jnp.broadcast_to(carry, cnt_ref.shape).astype(jnp.int32)


def _mid(an, m, x2d, wout, goutm, gxa, wmq, kmem, vmem, wmo, gmoe, wr, br, seq):
    t = x2d.shape[0]
    tm = TM_MID
    nseq = seq // tm
    mem_len = kmem.shape[1]
    row = lambda i: (i, 0)
    bmap = lambda i: (i // nseq, 0, 0)
    return pl.pallas_call(
        _mid_kernel,
        grid=(t // tm,),
        in_specs=[
            pl.BlockSpec((tm, GMLP_WIDTH), row),
            pl.BlockSpec((tm, MLA_WIDTH), row),
            pl.BlockSpec((tm, D_MODEL), row),
            _const_spec((D_MODEL, D_MODEL)),
            _const_spec((1, MLA_WIDTH)),
            _const_spec((1, D_MODEL)),
            _const_spec((D_MODEL, MEM_WIDTH)),
            pl.BlockSpec((None, mem_len, MEM_WIDTH), bmap),
            pl.BlockSpec((None, mem_len, MEM_WIDTH), bmap),
            _const_spec((MEM_WIDTH, D_MODEL)),
            _const_spec((1, D_MODEL)),
            _const_spec((D_MODEL, LANES)),
            _const_spec((1, LANES)),
        ],
        out_specs=[
            pl.BlockSpec((tm, D_MODEL), row),
            pl.BlockSpec((tm * ROW_SUB, LANES), row),
            pl.BlockSpec((tm, LANES), row),
            pl.BlockSpec((tm, LANES), row),
            pl.BlockSpec((ROW_SUB, LANES), lambda i: (0, 0)),
        ],
        out_shape=[
            jax.ShapeDtypeStruct((t, D_MODEL), F32),
            jax.ShapeDtypeStruct((t * ROW_SUB, LANES), U32),
            jax.ShapeDtypeStruct((t, LANES), jnp.int32),
            jax.ShapeDtypeStruct((t, LANES), F32),
            jax.ShapeDtypeStruct((ROW_SUB, LANES), jnp.int32),
        ],
        scratch_shapes=[pltpu.VMEM((1, LANES), F32)],
        compiler_params=_cparams(("arbitrary",)),
        name="mid",
    )(an, m, x2d, wout, goutm, gxa, wmq, kmem, vmem, wmo, gmoe, wr, br)


PAD_BITS = tuple(1 << b for b in reversed(range(BM.bit_length() - 1)))


def _dispatch_kernel(starts_ref, padfrom_ref, padn_ref, nused_ref,
                     er_ref, hp_ref, dflt_ref,
                     xs_ref, rdst_ref,
                     zero_buf, sem, zsem, isem):
    i = pl.program_id(0)
    tm = hp_ref.shape[0] // ROW_SUB
    n_tok = pl.num_programs(0) * tm

    @pl.when(i == 0)
    def _():
        preset = pltpu.make_async_copy(dflt_ref, rdst_ref, isem)
        preset.start()
        zero_buf[...] = jnp.zeros(zero_buf.shape, zero_buf.dtype)

        def pad_copy(e, bit):
            n = padn_ref[e]
            off = padfrom_ref[e] + (n & ~(2 * bit - 1))
            return n & bit, pltpu.make_async_copy(
                zero_buf.at[pl.ds(0, bit * ROW_SUB), :],
                xs_ref.at[pl.ds(pl.multiple_of(off * ROW_SUB, ROW_SUB), bit * ROW_SUB), :], zsem)

        def fill(e, c):
            for bit in PAD_BITS:
                on, cp = pad_copy(e, bit)

                @pl.when(on != 0)
                def _():
                    cp.start()
            return c

        def fill_wait(e, c):
            for bit in PAD_BITS:
                on, cp = pad_copy(e, bit)

                @pl.when(on != 0)
                def _():
                    cp.wait()
            return c

        def tail_copy(b):
            return pltpu.make_async_copy(
                zero_buf, xs_ref.at[pl.ds(pl.multiple_of(b * (BM * ROW_SUB), BM * ROW_SUB), BM * ROW_SUB), :],
                zsem)

        def tail(b, c):
            tail_copy(b).start()
            return c

        def tail_wait(b, c):
            tail_copy(b).wait()
            return c

        n_blocks = xs_ref.shape[0] // (BM * ROW_SUB)
        lax.fori_loop(0, N_EXPERTS, fill, 0)
        lax.fori_loop(nused_ref[0], n_blocks, tail, 0)
        lax.fori_loop(0, N_EXPERTS, fill_wait, 0)
        lax.fori_loop(nused_ref[0], n_blocks, tail_wait, 0)
        preset.wait()

    for t in range(tm):
        for k in range(TOP_K):
            e = er_ref[0, 0, 4 * t + k]
            d = starts_ref[e] + er_ref[0, 0, 4 * t + TOP_K + k]
            pltpu.make_async_copy(hp_ref.at[pl.ds(t * ROW_SUB, ROW_SUB), :],
                                  xs_ref.at[pl.ds(pl.multiple_of(d * ROW_SUB, ROW_SUB), ROW_SUB), :],
                                  sem).start()
            rdst_ref[BM + d] = (k * n_tok + i * tm + t) * ROW_SUB
    for k in range(TOP_K):
        pltpu.make_async_copy(hp_ref, xs_ref.at[pl.ds(0, tm * ROW_SUB), :], sem).wait()


def _dispatch(starts, pad_from, pad_n, n_used, er, hp, dflt, p_rows):
    tm = TM_DISPATCH
    nb = hp.shape[0] // (tm * ROW_SUB)
    er3 = er.reshape(nb, 1, tm * 2 * TOP_K)
    grid_spec = pltpu.PrefetchScalarGridSpec(
        num_scalar_prefetch=4,
        grid=(nb,),
        in_specs=[
            pl.BlockSpec((1, 1, tm * 2 * TOP_K), lambda i, *_: (i, 0, 0), memory_space=pltpu.SMEM),
            pl.BlockSpec((tm * ROW_SUB, LANES), lambda i, *_: (i, 0)),
            pl.BlockSpec(memory_space=pl.ANY),
        ],
        out_specs=[
            pl.BlockSpec(memory_space=pl.ANY),
            pl.BlockSpec(memory_space=pltpu.SMEM),
        ],
        scratch_shapes=[
            pltpu.VMEM((BM * ROW_SUB, LANES), U32),
            pltpu.SemaphoreType.DMA(()),
            pltpu.SemaphoreType.DMA(()),
            pltpu.SemaphoreType.DMA(()),
        ],
    )
    return pl.pallas_call(
        _dispatch_kernel,
        grid_spec=grid_spec,
        out_shape=[
            jax.ShapeDtypeStruct((p_rows * ROW_SUB, LANES), U32),
            jax.ShapeDtypeStruct(dflt.shape, jnp.int32),
        ],
        compiler_params=_cparams(("arbitrary",)),
        name="dispatch",
    )(starts, pad_from, pad_n, n_used, er3, hp, dflt)


def _row_scatter_start(src_vmem, idx_ref, dst_hbm, sem):
    for r in range(BM):
        d0 = pl.multiple_of(idx_ref[0, 0, r], ROW_SUB)
        pltpu.make_async_copy(src_vmem.at[pl.ds(r * ROW_SUB, ROW_SUB), :],
                              dst_hbm.at[pl.ds(d0, ROW_SUB), :], sem).start()


def _scatter_wait(src_vmem, dst_hbm, sem):
    pltpu.make_async_copy(src_vmem, dst_hbm.at[pl.ds(0, BM * ROW_SUB), :], sem).wait()


def _cast_pair_rows(dst_s, src_st):
    half = D_MODEL // 2
    for j in range(ROW_SUB):
        a = j * LANES
        dst_s[2 * a:2 * a + LANES, :] = src_st[a:a + LANES, :].astype(BF16)
        dst_s[2 * a + LANES:2 * a + 2 * LANES, :] = src_st[half + a:half + a + LANES, :].astype(BF16)


def _expert_kernel(be_ref, nxt_ref, nused_ref, segpar_ref,
                   dstp_ref, xs_ref, wg_ref, wu_ref, wd_ref,
                   out_ref,
                   wg_st, wu_st, wd_st, wg_s, wu_s, wd_s, ybuf0, ybuf1,
                   wsem, ssem):
    i = pl.program_id(0)
    nused = nused_ref[0]
    used = i < nused
    par = i % 2
    ic = jnp.minimum(i, be_ref.shape[0] - 1)
    e = be_ref[ic]
    first = jnp.logical_or(i == 0, e != be_ref[jnp.maximum(i - 1, 0)])

    def weight_copies(ex, slot):
        return (pltpu.make_async_copy(wg_ref.at[ex], wg_st.at[slot], wsem.at[slot, 0]),
                pltpu.make_async_copy(wu_ref.at[ex], wu_st.at[slot], wsem.at[slot, 1]),
                pltpu.make_async_copy(wd_ref.at[ex], wd_st.at[slot], wsem.at[slot, 2]))

    @pl.when(i == 0)
    def _():
        for cp in weight_copies(e, 0):
            cp.start()
        ybuf1[...] = jnp.zeros(ybuf1.shape, ybuf1.dtype)

    def load_weights(slot):
        nxt = nxt_ref[ic]

        @pl.when(nxt >= 0)
        def _():
            for cp in weight_copies(nxt, 1 - slot):
                cp.start()

        for cp in weight_copies(0, slot):
            cp.wait()
        _cast_pair_rows(wg_s, wg_st.at[slot])
        _cast_pair_rows(wu_s, wu_st.at[slot])
        wd_s[...] = wd_st[slot].astype(BF16)

    for slot in range(2):
        @pl.when(jnp.logical_and(jnp.logical_and(first, used), segpar_ref[ic] == slot))
        def _():
            load_weights(slot)

    def step(yb, yo, s):
        o = 1 - s

        @pl.when(i >= 1)
        def _():
            _scatter_wait(yb, out_ref, ssem.at[s])

        _row_scatter_start(yo, dstp_ref, out_ref, ssem.at[o])
        half = D_MODEL // 2
        cols = []
        for w in _load_row_tiles(xs_ref):
            hi, lo = _unpack_pair(w)
            cols += [hi.astype(BF16), lo.astype(BF16)]
        xrow = jnp.concatenate(cols, axis=1)
        g = _dot(xrow, wg_s[...])
        u = _dot(xrow, wu_s[...])
        hm = (g * jax.nn.sigmoid(g) * u).astype(BF16)
        y = _dot(hm, wd_s[...])
        _store_row_tiles(yb, _pack_pair(y[:, :half], y[:, half:]))

    @pl.when(jnp.logical_and(used, par == 0))
    def _():
        step(ybuf0, ybuf1, 0)

    @pl.when(jnp.logical_and(used, par == 1))
    def _():
        step(ybuf1, ybuf0, 1)

    def drain(yb, yo, s):
        o = 1 - s
        _scatter_wait(yb, out_ref, ssem.at[s])
        _row_scatter_start(yo, dstp_ref, out_ref, ssem.at[o])
        _scatter_wait(yo, out_ref, ssem.at[o])

    @pl.when(jnp.logical_and(i == nused, par == 0))
    def _():
        drain(ybuf0, ybuf1, 0)

    @pl.when(jnp.logical_and(i == nused, par == 1))
    def _():
        drain(ybuf1, ybuf0, 1)


def _experts(block_expert, next_expert, n_used, seg_par, rows_dst, xs, wg, wu, wd, n_out_rows):
    nb = block_expert.shape[0]
    dst3 = rows_dst.reshape(nb + 1, 1, BM)
    grid_spec = pltpu.PrefetchScalarGridSpec(
        num_scalar_prefetch=4,
        grid=(nb + 1,),
        in_specs=[
            pl.BlockSpec((1, 1, BM), lambda i, *_: (i, 0, 0), memory_space=pltpu.SMEM),
            pl.BlockSpec((BM * ROW_SUB, LANES), lambda i, be, nx, nu, sp: (jnp.minimum(i, nu[0] - 1), 0)),
            pl.BlockSpec(memory_space=pl.ANY),
            pl.BlockSpec(memory_space=pl.ANY),
            pl.BlockSpec(memory_space=pl.ANY),
        ],
        out_specs=pl.BlockSpec(memory_space=pl.ANY),
        scratch_shapes=[
            pltpu.VMEM((2, D_MODEL, D_EXPERT), F32),
            pltpu.VMEM((2, D_MODEL, D_EXPERT), F32),
            pltpu.VMEM((2, D_EXPERT, D_MODEL), F32),
            pltpu.VMEM((D_MODEL, D_EXPERT), BF16),
            pltpu.VMEM((D_MODEL, D_EXPERT), BF16),
            pltpu.VMEM((D_EXPERT, D_MODEL), BF16),
            pltpu.VMEM((BM * ROW_SUB, LANES), U32),
            pltpu.VMEM((BM * ROW_SUB, LANES), U32),
            pltpu.SemaphoreType.DMA((2, 3)),
            pltpu.SemaphoreType.DMA((2,)),
        ],
    )
    return pl.pallas_call(
        _expert_kernel,
        grid_spec=grid_spec,
        out_shape=jax.ShapeDtypeStruct((n_out_rows * ROW_SUB, LANES), U32),
        compiler_params=_cparams(("arbitrary",)),
        name="experts",
    )(block_expert, next_expert, n_used, seg_par, dst3, xs, wg, wu, wd)


def _combine_kernel(x2_ref, gate_ref, gfin_ref, y0_ref, y1_ref, o_ref):
    gate = gate_ref[...]
    half = D_MODEL // 2
    y_hi = x2_ref[:, :half]
    y_lo = x2_ref[:, half:]
    for k, y_ref in enumerate((y0_ref, y1_ref)):
        pairs = [_unpack_pair(w) for w in _load_row_tiles(y_ref)]
        hi = jnp.concatenate([p[0] for p in pairs], axis=1)
        lo = jnp.concatenate([p[1] for p in pairs], axis=1)
        gk = gate[:, k:k + 1]
        y_hi = y_hi + gk * hi
        y_lo = y_lo + gk * lo
    ms = (jnp.sum(y_hi * y_hi, axis=-1, keepdims=True)
          + jnp.sum(y_lo * y_lo, axis=-1, keepdims=True)) * (1.0 / D_MODEL)
    r = lax.rsqrt(ms + EPS)
    o_ref[:, :half] = y_hi * r * gfin_ref[:, :half]
    o_ref[:, half:] = y_lo * r * gfin_ref[:, half:]


def _combine(x2, gate, gfin, out2):
    t, d = x2.shape
    tm = TM_ROW
    nb = t // tm
    row = lambda i: (i, 0)
    return pl.pallas_call(
        _combine_kernel,
        grid=(nb,),
        in_specs=[
            pl.BlockSpec((tm, d), row),
            pl.BlockSpec((tm, LANES), row),
            _const_spec((1, d)),
            pl.BlockSpec((tm * ROW_SUB, LANES), row),
            pl.BlockSpec((tm * ROW_SUB, LANES), lambda i: (nb + i, 0)),
        ],
        out_specs=pl.BlockSpec((tm, d), row),
        out_shape=jax.ShapeDtypeStruct((t, d), F32),
        compiler_params=_cparams(("arbitrary",)),
        name="combine",
    )(x2, gate, gfin, out2, out2)


def _rot_half_cols(w):
    half = QK_ROPE // 2
    return jnp.concatenate([-w[..., half:], w[..., :half]], axis=-1)


def kernel(x, mem, g_norm_mix, w_in, g_v, b_v, w_spatial, b_spatial, g_q_lora, w_uq, g_kv_lora, w_ukv, g_out_gmlp, g_out_mla, w_out, g_norm_xattn, g_norm_mem, w_mq, w_mk, w_mv, w_mo, g_norm_moe, w_router_group, b_router_group, w_router_expert, b_router_expert, w_exp_gate, w_exp_up, w_exp_down, g_final):
    batch, seq, d = x.shape
    t = batch * seq
    x2d = x.reshape(t, d)
    r2 = lambda a: a.reshape(1, -1)

    w_kr = w_in[:, C_KR:C_KR + QK_ROPE]
    win = w_in[:, :C_KR].astype(BF16)
    wkr = jnp.concatenate([w_kr, _rot_half_cols(w_kr)], axis=1).astype(BF16)
    wq3 = w_uq.reshape(Q_LORA, MLA_HEADS, QK_NOPE + QK_ROPE)
    wq_rope = wq3[..., QK_NOPE:]
    wuq_ext = jnp.concatenate([wq3, _rot_half_cols(wq_rope)], axis=-1)
    wuq_ext = wuq_ext.reshape(Q_LORA, MLA_HEADS * QK_PAD).astype(BF16)
    wukv = w_ukv.astype(BF16)
    causal = jnp.tril(jnp.ones((CHUNK, CHUNK), dtype=bool))
    ws = jnp.where(causal[None], w_spatial, 0.0).astype(BF16)
    bsp = jnp.repeat(b_spatial.T, LANES, axis=1)
    wr = jnp.concatenate(
        [w_router_group, w_router_expert,
         jnp.zeros((d, LANES - N_GROUPS - N_EXPERTS), F32)], axis=1).astype(BF16)
    br = jnp.concatenate(
        [b_router_group, b_router_expert, jnp.zeros((LANES - N_GROUPS - N_EXPERTS,), F32)]).reshape(1, LANES)

    pos = jnp.arange(seq, dtype=F32)
    inv_freq = ROPE_THETA ** (-jnp.arange(0, QK_ROPE, 2, dtype=F32) / QK_ROPE)
    ang = pos[:, None] * inv_freq[None, :]
    cos, sin = jnp.cos(ang), jnp.sin(ang)
    cs = jnp.concatenate([cos, cos, sin, sin], axis=1)

    an, q, k, v = _front(x2d, r2(g_norm_mix), win, wkr, r2(g_v), r2(b_v), ws, bsp, r2(g_out_gmlp),
                         r2(g_q_lora), wuq_ext, r2(g_kv_lora), wukv, cs, seq)
    m = _mla_attn(q, k, v, batch, seq)
    kmem, vmem = _mem_kv(mem, r2(g_norm_mem), w_mk.astype(BF16), w_mv.astype(BF16))
    x2, hp, idx, gate, cnt = _mid(an, m, x2d, w_out.astype(BF16), r2(g_out_mla), r2(g_norm_xattn),
                             w_mq.astype(BF16), kmem, vmem, w_mo.astype(BF16), r2(g_norm_moe),
                             wr, br, seq)

    n_assign = t * TOP_K
    p_rows = n_assign + N_EXPERTS * BM
    nb = p_rows // BM
    counts = cnt[0, :N_EXPERTS]
    padded = ((counts + BM - 1) // BM) * BM
    padded_ends = jnp.cumsum(padded)
    padded_starts = padded_ends - padded
    n_used = (padded_ends[-1:] // BM).astype(jnp.int32)
    blk_row = jnp.arange(nb, dtype=jnp.int32) * BM
    block_expert = jnp.minimum(jnp.sum(padded_ends[None, :] <= blk_row[:, None], axis=1),
                               N_EXPERTS - 1).astype(jnp.int32)
    be_onehot = block_expert[:, None] == jnp.arange(N_EXPERTS, dtype=jnp.int32)[None, :]
    nxt_blk = jnp.sum(jnp.where(be_onehot, padded_ends[None, :], 0), axis=1) // BM
    nxt_onehot = nxt_blk[:, None] == jnp.arange(nb, dtype=jnp.int32)[None, :]
    next_expert = jnp.where(nxt_blk < n_used[0],
                            jnp.sum(jnp.where(nxt_onehot, block_expert[None, :], 0), axis=1),
                            -1).astype(jnp.int32)
    seg_id = jnp.cumsum((counts > 0).astype(jnp.int32)) - 1
    seg_par = (jnp.sum(jnp.where(be_onehot, seg_id[None, :], 0), axis=1) & 1).astype(jnp.int32)
    n_dst = -(-(p_rows + BM) // SMEM_1D_TILE) * SMEM_1D_TILE
    dflt = (n_assign + jnp.arange(n_dst, dtype=jnp.int32) % BM) * ROW_SUB

    xs, rows_dst = _dispatch(padded_starts.astype(jnp.int32), (padded_starts + counts).astype(jnp.int32),
                             (padded - counts).astype(jnp.int32), n_used, idx[:, :2 * TOP_K], hp, dflt,
                             p_rows)
    out2 = _experts(block_expert, next_expert, n_used, seg_par, rows_dst[:p_rows + BM], xs,
                    w_exp_gate, w_exp_up, w_exp_down, n_assign + BM)
    out = _combine(x2, gate, r2(g_final), out2)
    return out.reshape(batch, seq, d)
```

```python
import math

import jax
import jax.numpy as jnp
import numpy as np
from jax import lax
from jax.experimental import pallas as pl
from jax.experimental.pallas import tpu as pltpu

D_MODEL = 2048
CHUNK = 128
GMLP_HEADS = 8
GMLP_WIDTH = 1024
MLA_HEADS = 8
Q_LORA = 512
KV_LORA = 256
QK_NOPE = 128
QK_ROPE = 64
V_DIM = 128
MLA_WIDTH = MLA_HEADS * V_DIM
ROPE_THETA = 10000.0
MEM_HEADS = 4
MEM_HEAD_DIM = 128
MEM_WIDTH = MEM_HEADS * MEM_HEAD_DIM
N_GROUPS = 8
EXPERTS_PER_GROUP = 8
N_EXPERTS = 64
TOP_K = 2
D_EXPERT = 512
EPS = 1e-6
LN_EPS = 1e-5

LANES = 128
QK_PAD = 256
VMEM_LIMIT = 56 * 1024 * 1024

C_UV = 0
C_Q = 2 * GMLP_WIDTH
C_KV = C_Q + Q_LORA
C_KR = C_KV + KV_LORA
SMEM_1D_TILE = 1024

TM_FRONT = 512
TM_MID = 512
SUB_MID = 256
TQ = 256
ATTN_GROUP = 2
BM = 256
TM_ROW = 256
TM_DISPATCH = 512

F32 = jnp.float32
BF16 = jnp.bfloat16
U32 = jnp.uint32
NEG = float(np.finfo(np.float32).min)


def _cparams(sem):
    return pltpu.CompilerParams(dimension_semantics=sem, vmem_limit_bytes=VMEM_LIMIT)


def _const_spec(shape):
    n = len(shape)
    return pl.BlockSpec(shape, lambda *_: (0,) * n, pipeline_mode=pl.Buffered(1))


def _rms(x, g):
    ms = jnp.mean(x * x, axis=-1, keepdims=True)
    return x * lax.rsqrt(ms + EPS) * g


def _gelu_tanh(x):
    c = math.sqrt(2.0 / math.pi)
    return 0.5 * x * (1.0 + jnp.tanh(c * (x + 0.044715 * (x * x * x))))


def _dot(a, b):
    return jnp.dot(a, b, preferred_element_type=F32)


def _dot_nt(a, b):
    return lax.dot_general(a, b, (((1,), (1,)), ((), ())), preferred_element_type=F32)


def _pack_pair(hi, lo):
    hb = pltpu.bitcast(hi.astype(BF16).astype(F32), U32)
    lb = pltpu.bitcast(lo.astype(BF16).astype(F32), U32)
    return hb | (lb >> 16)


def _unpack_pair(w):
    hi = pltpu.bitcast(w & jnp.uint32(0xFFFF0000), F32)
    lo = pltpu.bitcast(w << 16, F32)
    return hi, lo


ROW_SUB = 8
ROW_WORDS = ROW_SUB * LANES


def _store_row_tiles(ref, packed):
    m = packed.shape[0]
    for j in range(ROW_SUB):
        ref[pl.ds(j, m, stride=ROW_SUB), :] = packed[:, j * LANES:(j + 1) * LANES]


def _load_row_tiles(ref):
    m = ref.shape[0] // ROW_SUB
    return [ref[pl.ds(j, m, stride=ROW_SUB), :] for j in range(ROW_SUB)]


def _front_kernel(x_ref, gmix_ref, win_ref, wkr_ref, gv_ref, bv_ref, ws_ref, bsp_ref, goutg_ref,
                  gq_ref, wuq_ref, gkv_ref, wukv_ref, cs_ref,
                  an_ref, q_ref, k_ref, v_ref):
    tm = x_ref.shape[0]
    xn = _rms(x_ref[...], gmix_ref[...]).astype(BF16)
    z = _dot(xn, win_ref[...])
    zk = _dot(xn, wkr_ref[...])

    u = _gelu_tanh(z[:, :GMLP_WIDTH])
    v = _gelu_tanh(z[:, GMLP_WIDTH:2 * GMLP_WIDTH])
    mu = jnp.mean(v, axis=-1, keepdims=True)
    vc = v - mu
    var = jnp.mean(vc * vc, axis=-1, keepdims=True)
    vn = (vc * lax.rsqrt(var + LN_EPS) * gv_ref[...] + bv_ref[...]).astype(BF16)
    a_chunks = []
    for c in range(tm // CHUNK):
        r0 = c * CHUNK
        cols = []
        for g in range(GMLP_HEADS):
            c0 = g * LANES
            sv = _dot(ws_ref[g], vn[r0:r0 + CHUNK, c0:c0 + LANES])
            cols.append(sv)
        sv_all = jnp.concatenate(cols, axis=1) + bsp_ref[...]
        a_chunks.append(u[r0:r0 + CHUNK, :] * sv_all)
    a = jnp.concatenate(a_chunks, axis=0)
    an_ref[...] = _rms(a, goutg_ref[...]).astype(BF16)

    cs = cs_ref[...]
    scale = 1.0 / math.sqrt(QK_NOPE + QK_ROPE)
    cqn = _rms(z[:, C_Q:C_KV], gq_ref[...]).astype(BF16)
    q = _dot(cqn, wuq_ref[...]) * scale
    ckvn = _rms(z[:, C_KV:C_KR], gkv_ref[...]).astype(BF16)
    kv = _dot(ckvn, wukv_ref[...])
    lane = lax.broadcasted_iota(jnp.int32, (tm, LANES), 1)
    kr = zk * cs
    kr = jnp.where(lane < QK_ROPE, kr + pltpu.roll(kr, QK_ROPE, 1), 0.0).astype(BF16)
    for h in range(MLA_HEADS):
        b0 = h * QK_PAD
        qr = q[:, b0 + LANES:b0 + QK_PAD] * cs
        qr = qr + pltpu.roll(qr, QK_ROPE, 1)
        q_ref[:, b0:b0 + LANES] = q[:, b0:b0 + LANES].astype(BF16)
        q_ref[:, b0 + LANES:b0 + QK_PAD] = qr.astype(BF16)
        k_ref[:, b0:b0 + LANES] = kv[:, b0:b0 + LANES].astype(BF16)
        k_ref[:, b0 + LANES:b0 + QK_PAD] = kr
        v_ref[:, h * V_DIM:(h + 1) * V_DIM] = kv[:, b0 + LANES:b0 + QK_PAD].astype(BF16)


def _front(x2d, gmix, win, wkr, gv, bv, ws, bsp, goutg, gq, wuq_ext, gkv, wukv, cs, seq):
    t = x2d.shape[0]
    tm = TM_FRONT
    nseq = seq // tm
    row = lambda i: (i, 0)
    return pl.pallas_call(
        _front_kernel,
        grid=(t // tm,),
        in_specs=[
            pl.BlockSpec((tm, D_MODEL), row),
            _const_spec((1, D_MODEL)),
            _const_spec((D_MODEL, C_KR)),
            _const_spec((D_MODEL, LANES)),
            _const_spec((1, GMLP_WIDTH)),
            _const_spec((1, GMLP_WIDTH)),
            _const_spec((GMLP_HEADS, CHUNK, CHUNK)),
            _const_spec((CHUNK, GMLP_WIDTH)),
            _const_spec((1, GMLP_WIDTH)),
            _const_spec((1, Q_LORA)),
            _const_spec((Q_LORA, MLA_HEADS * QK_PAD)),
            _const_spec((1, KV_LORA)),
            _const_spec((KV_LORA, MLA_HEADS * QK_PAD)),
            pl.BlockSpec((tm, LANES), lambda i: (i % nseq, 0)),
        ],
        out_specs=[
            pl.BlockSpec((tm, GMLP_WIDTH), row),
            pl.BlockSpec((tm, MLA_HEADS * QK_PAD), row),
            pl.BlockSpec((tm, MLA_HEADS * QK_PAD), row),
            pl.BlockSpec((tm, MLA_WIDTH), row),
        ],
        out_shape=[
            jax.ShapeDtypeStruct((t, GMLP_WIDTH), BF16),
            jax.ShapeDtypeStruct((t, MLA_HEADS * QK_PAD), BF16),
            jax.ShapeDtypeStruct((t, MLA_HEADS * QK_PAD), BF16),
            jax.ShapeDtypeStruct((t, MLA_WIDTH), BF16),
        ],
        compiler_params=_cparams(("arbitrary",)),
        name="front",
    )(x2d, gmix, win, wkr, gv, bv, ws, bsp, goutg, gq, wuq_ext, gkv, wukv, cs)


def _attn_kernel(q_ref, k_ref, v_ref, o_ref):
    s_len = q_ref.shape[0]
    row = lax.broadcasted_iota(jnp.int32, (TQ, TQ), 0)
    col = lax.broadcasted_iota(jnp.int32, (TQ, TQ), 1)
    causal = col <= row
    n_blk = s_len // TQ
    for g0 in range(0, n_blk, ATTN_GROUP):
        blocks = list(range(g0, min(g0 + ATTN_GROUP, n_blk)))
        qb = {qi: q_ref[qi * TQ:(qi + 1) * TQ, :] for qi in blocks}
        sd = {qi: jnp.where(causal, _dot_nt(qb[qi], k_ref[qi * TQ:(qi + 1) * TQ, :]), NEG)
              for qi in blocks}
        so = {qi: _dot_nt(qb[qi], k_ref[0:qi * TQ, :]) for qi in blocks if qi > 0}
        m = {qi: jnp.max(sd[qi], axis=-1, keepdims=True) for qi in blocks}
        for qi in so:
            m[qi] = jnp.maximum(m[qi], jnp.max(so[qi], axis=-1, keepdims=True))
        pd = {qi: jnp.exp(sd[qi] - m[qi]) for qi in blocks}
        po = {qi: jnp.exp(so[qi] - m[qi]) for qi in so}
        l = {qi: jnp.sum(pd[qi], axis=-1, keepdims=True) for qi in blocks}
        acc = {qi: _dot(pd[qi].astype(BF16), v_ref[qi * TQ:(qi + 1) * TQ, :]) for qi in blocks}
        for qi in so:
            l[qi] = l[qi] + jnp.sum(po[qi], axis=-1, keepdims=True)
            acc[qi] = acc[qi] + _dot(po[qi].astype(BF16), v_ref[0:qi * TQ, :])
        for qi in blocks:
            o_ref[qi * TQ:(qi + 1) * TQ, :] = (acc[qi] / l[qi]).astype(o_ref.dtype)


def _mla_attn(q, k, v, batch, seq):
    q3 = q.reshape(batch, seq, MLA_HEADS * QK_PAD)
    k3 = k.reshape(batch, seq, MLA_HEADS * QK_PAD)
    v3 = v.reshape(batch, seq, MLA_WIDTH)
    hmap = lambda b, h: (b, 0, h)
    out = pl.pallas_call(
        _attn_kernel,
        grid=(batch, MLA_HEADS),
        in_specs=[
            pl.BlockSpec((None, seq, QK_PAD), hmap),
            pl.BlockSpec((None, seq, QK_PAD), hmap),
            pl.BlockSpec((None, seq, V_DIM), hmap),
        ],
        out_specs=pl.BlockSpec((None, seq, V_DIM), hmap),
        out_shape=jax.ShapeDtypeStruct((batch, seq, MLA_WIDTH), BF16),
        compiler_params=_cparams(("arbitrary", "arbitrary")),
        name="mla_attn",
    )(q3, k3, v3)
    return out.reshape(batch * seq, MLA_WIDTH)


def _memkv_kernel(mem_ref, g_ref, wk_ref, wv_ref, k_ref, v_ref):
    mn = _rms(mem_ref[...], g_ref[...]).astype(BF16)
    k_ref[...] = _dot(mn, wk_ref[...]).astype(BF16)
    v_ref[...] = _dot(mn, wv_ref[...]).astype(BF16)


def _mem_kv(mem, g, wk, wv):
    b, m, d = mem.shape
    bmap = lambda i: (i, 0, 0)
    return pl.pallas_call(
        _memkv_kernel,
        grid=(b,),
        in_specs=[
            pl.BlockSpec((None, m, d), bmap),
            _const_spec((1, d)),
            _const_spec((d, MEM_WIDTH)),
            _const_spec((d, MEM_WIDTH)),
        ],
        out_specs=[pl.BlockSpec((None, m, MEM_WIDTH), bmap)] * 2,
        out_shape=[jax.ShapeDtypeStruct((b, m, MEM_WIDTH), BF16)] * 2,
        compiler_params=_cparams(("arbitrary",)),
        name="mem_kv",
    )(mem, g, wk, wv)


def _route(lg, carry):
    tm = lg.shape[0]
    lane_i = lax.broadcasted_iota(jnp.int32, (tm, LANES), 1)
    lane = lane_i.astype(F32)
    big = float(LANES)
    gmask = lane_i < N_GROUPS
    gl = jnp.where(gmask, lg, NEG)
    gmax = jnp.max(gl, axis=-1, keepdims=True)
    grp = jnp.min(jnp.where(gl == gmax, lane, big), axis=-1, keepdims=True)
    gsum = jnp.sum(jnp.where(gmask, jnp.exp(gl - gmax), 0.0), axis=-1, keepdims=True)
    p_grp = 1.0 / gsum

    lo = (grp + 1.0) * EXPERTS_PER_GROUP
    emask = jnp.logical_and(lane >= lo, lane < lo + EXPERTS_PER_GROUP)
    el = jnp.where(emask, lg, NEG)
    emax = jnp.max(el, axis=-1, keepdims=True)
    ee = jnp.where(emask, jnp.exp(el - emax), 0.0)
    ep = ee / jnp.sum(ee, axis=-1, keepdims=True)
    ep = jnp.where(emask, ep, -1.0)
    p1 = jnp.max(ep, axis=-1, keepdims=True)
    i1 = jnp.min(jnp.where(ep == p1, lane, big), axis=-1, keepdims=True)
    ep2 = jnp.where(lane == i1, -1.0, ep)
    p2 = jnp.max(ep2, axis=-1, keepdims=True)
    i2 = jnp.min(jnp.where(ep2 == p2, lane, big), axis=-1, keepdims=True)
    den = p1 + p2
    g1 = p_grp * (p1 / den)
    g2 = p_grp * (p2 / den)
    e1 = i1 - N_GROUPS
    e2 = i2 - N_GROUPS
    gate = jnp.where(lane_i == 0, g1, jnp.where(lane_i == 1, g2, 0.0))

    oh1 = (lane == e1).astype(F32)
    oh2 = (lane == e2).astype(F32)
    both = oh1 + oh2
    r_i = lax.broadcasted_iota(jnp.int32, (tm, tm), 0)
    c_i = lax.broadcasted_iota(jnp.int32, (tm, tm), 1)
    before = (c_i < r_i).astype(BF16)
    base = carry + _dot(before, both.astype(BF16))
    r1 = jnp.sum(oh1 * base, axis=-1, keepdims=True)
    r2 = jnp.sum(oh2 * base, axis=-1, keepdims=True)
    new_carry = carry + jnp.sum(both, axis=0, keepdims=True)
    idx = jnp.where(lane_i == 0, e1, jnp.where(lane_i == 1, e2,
                    jnp.where(lane_i == 2, r1, jnp.where(lane_i == 3, r2, 0.0)))).astype(jnp.int32)
    return idx, gate, new_carry


def _mid_kernel(an_ref, m_ref, x_ref, wout_ref, goutm_ref, gxa_ref, wmq_ref, kmem_ref, vmem_ref,
                wmo_ref, gmoe_ref, wr_ref, br_ref,
                x2_ref, hp_ref, idx_ref, gate_ref, cnt_ref, carry_ref):
    @pl.when(pl.program_id(0) == 0)
    def _():
        carry_ref[...] = jnp.zeros(carry_ref.shape, carry_ref.dtype)

    subs = [pl.ds(r0, SUB_MID) for r0 in range(0, x_ref.shape[0], SUB_MID)]
    mn = [_rms(m_ref[r, :].astype(F32), goutm_ref[...]).astype(BF16) for r in subs]
    x1 = [x_ref[r, :] + _dot(an_ref[r, :], wout_ref[:GMLP_WIDTH, :]) + _dot(a, wout_ref[GMLP_WIDTH:, :])
          for r, a in zip(subs, mn)]
    h2 = [_rms(a, gxa_ref[...]).astype(BF16) for a in x1]
    q2 = [(_dot(a, wmq_ref[...]) * (1.0 / math.sqrt(MEM_HEAD_DIM))).astype(BF16) for a in h2]

    def mem_attention(q):
        outs = []
        for h in range(MEM_HEADS):
            c0 = h * MEM_HEAD_DIM
            s = _dot_nt(q[:, c0:c0 + MEM_HEAD_DIM], kmem_ref[:, c0:c0 + MEM_HEAD_DIM])
            m = jnp.max(s, axis=-1, keepdims=True)
            p = jnp.exp(s - m)
            l = jnp.sum(p, axis=-1, keepdims=True)
            outs.append(_dot(p.astype(BF16), vmem_ref[:, c0:c0 + MEM_HEAD_DIM]) / l)
        return jnp.concatenate(outs, axis=1).astype(BF16)

    o = [mem_attention(q) for q in q2]
    x2 = [a + _dot(b, wmo_ref[...]) for a, b in zip(x1, o)]
    for r, a in zip(subs, x2):
        x2_ref[r, :] = a
    h3 = [_rms(a, gmoe_ref[...]) for a in x2]
    half = D_MODEL // 2
    for n, a in enumerate(h3):
        _store_row_tiles(hp_ref.at[pl.ds(n * SUB_MID * ROW_SUB, SUB_MID * ROW_SUB), :],
                         _pack_pair(a[:, :half], a[:, half:]))
    lg = [_dot(a.astype(BF16), wr_ref[...]) + br_ref[...] for a in h3]
    carry = carry_ref[...]
    for r, a in zip(subs, lg):
        idx, gate, carry = _route(a, carry)
        idx_ref[r, :] = idx
        gate_ref[r, :] = gate
    carry_ref[...] = carry
    cnt_ref[...] = jnp.broadcast_to(carry, cnt_ref.shape).astype(jnp.int32)


def _mid(an, m, x2d, wout, goutm, gxa, wmq, kmem, vmem, wmo, gmoe, wr, br, seq):
    t = x2d.shape[0]
    tm = TM_MID
    nseq = seq // tm
    mem_len = kmem.shape[1]
    row = lambda i: (i, 0)
    bmap = lambda i: (i // nseq, 0, 0)
    return pl.pallas_call(
        _mid_kernel,
        grid=(t // tm,),
        in_specs=[
            pl.BlockSpec((tm, GMLP_WIDTH), row),
            pl.BlockSpec((tm, MLA_WIDTH), row),
            pl.BlockSpec((tm, D_MODEL), row),
            _const_spec((D_MODEL, D_MODEL)),
            _const_spec((1, MLA_WIDTH)),
            _const_spec((1, D_MODEL)),
            _const_spec((D_MODEL, MEM_WIDTH)),
            pl.BlockSpec((None, mem_len, MEM_WIDTH), bmap),
            pl.BlockSpec((None, mem_len, MEM_WIDTH), bmap),
            _const_spec((MEM_WIDTH, D_MODEL)),
            _const_spec((1, D_MODEL)),
            _const_spec((D_MODEL, LANES)),
            _const_spec((1, LANES)),
        ],
        out_specs=[
            pl.BlockSpec((tm, D_MODEL), row),
            pl.BlockSpec((tm * ROW_SUB, LANES), row),
            pl.BlockSpec((tm, LANES), row),
            pl.BlockSpec((tm, LANES), row),
            pl.BlockSpec((ROW_SUB, LANES), lambda i: (0, 0)),
        ],
        out_shape=[
            jax.ShapeDtypeStruct((t, D_MODEL), F32),
            jax.ShapeDtypeStruct((t * ROW_SUB, LANES), U32),
            jax.ShapeDtypeStruct((t, LANES), jnp.int32),
            jax.ShapeDtypeStruct((t, LANES), F32),
            jax.ShapeDtypeStruct((ROW_SUB, LANES), jnp.int32),
        ],
        scratch_shapes=[pltpu.VMEM((1, LANES), F32)],
        compiler_params=_cparams(("arbitrary",)),
        name="mid",
    )(an, m, x2d, wout, goutm, gxa, wmq, kmem, vmem, wmo, gmoe, wr, br)


PAD_BITS = tuple(1 << b for b in reversed(range(BM.bit_length() - 1)))


def _dispatch_kernel(starts_ref, padfrom_ref, padn_ref, nused_ref,
                     er_ref, hp_ref, dflt_ref,
                     xs_ref, rdst_ref,
                     zero_buf, sem, zsem, isem):
    i = pl.program_id(0)
    tm = hp_ref.shape[0] // ROW_SUB
    n_tok = pl.num_programs(0) * tm

    @pl.when(i == 0)
    def _():
        preset = pltpu.make_async_copy(dflt_ref, rdst_ref, isem)
        preset.start()
        zero_buf[...] = jnp.zeros(zero_buf.shape, zero_buf.dtype)

        def pad_copy(e, bit):
            n = padn_ref[e]
            off = padfrom_ref[e] + (n & ~(2 * bit - 1))
            return n & bit, pltpu.make_async_copy(
                zero_buf.at[pl.ds(0, bit * ROW_SUB), :],
                xs_ref.at[pl.ds(pl.multiple_of(off * ROW_SUB, ROW_SUB), bit * ROW_SUB), :], zsem)

        def fill(e, c):
            for bit in PAD_BITS:
                on, cp = pad_copy(e, bit)

                @pl.when(on != 0)
                def _():
                    cp.start()
            return c

        def fill_wait(e, c):
            for bit in PAD_BITS:
                on, cp = pad_copy(e, bit)

                @pl.when(on != 0)
                def _():
                    cp.wait()
            return c

        def tail_copy(b):
            return pltpu.make_async_copy(
                zero_buf, xs_ref.at[pl.ds(pl.multiple_of(b * (BM * ROW_SUB), BM * ROW_SUB), BM * ROW_SUB), :],
                zsem)

        def tail(b, c):
            tail_copy(b).start()
            return c

        def tail_wait(b, c):
            tail_copy(b).wait()
            return c

        n_blocks = xs_ref.shape[0] // (BM * ROW_SUB)
        lax.fori_loop(0, N_EXPERTS, fill, 0)
        lax.fori_loop(nused_ref[0], n_blocks, tail, 0)
        lax.fori_loop(0, N_EXPERTS, fill_wait, 0)
        lax.fori_loop(nused_ref[0], n_blocks, tail_wait, 0)
        preset.wait()

    for t in range(tm):
        for k in range(TOP_K):
            e = er_ref[0, 0, 4 * t + k]
            d = starts_ref[e] + er_ref[0, 0, 4 * t + TOP_K + k]
            pltpu.make_async_copy(hp_ref.at[pl.ds(t * ROW_SUB, ROW_SUB), :],
                                  xs_ref.at[pl.ds(pl.multiple_of(d * ROW_SUB, ROW_SUB), ROW_SUB), :],
                                  sem).start()
            rdst_ref[BM + d] = (k * n_tok + i * tm + t) * ROW_SUB
    for k in range(TOP_K):
        pltpu.make_async_copy(hp_ref, xs_ref.at[pl.ds(0, tm * ROW_SUB), :], sem).wait()


def _dispatch(starts, pad_from, pad_n, n_used, er, hp, dflt, p_rows):
    tm = TM_DISPATCH
    nb = hp.shape[0] // (tm * ROW_SUB)
    er3 = er.reshape(nb, 1, tm * 2 * TOP_K)
    grid_spec = pltpu.PrefetchScalarGridSpec(
        num_scalar_prefetch=4,
        grid=(nb,),
        in_specs=[
            pl.BlockSpec((1, 1, tm * 2 * TOP_K), lambda i, *_: (i, 0, 0), memory_space=pltpu.SMEM),
            pl.BlockSpec((tm * ROW_SUB, LANES), lambda i, *_: (i, 0)),
            pl.BlockSpec(memory_space=pl.ANY),
        ],
        out_specs=[
            pl.BlockSpec(memory_space=pl.ANY),
            pl.BlockSpec(memory_space=pltpu.SMEM),
        ],
        scratch_shapes=[
            pltpu.VMEM((BM * ROW_SUB, LANES), U32),
            pltpu.SemaphoreType.DMA(()),
            pltpu.SemaphoreType.DMA(()),
            pltpu.SemaphoreType.DMA(()),
        ],
    )
    return pl.pallas_call(
        _dispatch_kernel,
        grid_spec=grid_spec,
        out_shape=[
            jax.ShapeDtypeStruct((p_rows * ROW_SUB, LANES), U32),
            jax.ShapeDtypeStruct(dflt.shape, jnp.int32),
        ],
        compiler_params=_cparams(("arbitrary",)),
        name="dispatch",
    )(starts, pad_from, pad_n, n_used, er3, hp, dflt)


def _row_scatter_start(src_vmem, idx_ref, dst_hbm, sem):
    for r in range(BM):
        d0 = pl.multiple_of(idx_ref[0, 0, r], ROW_SUB)
        pltpu.make_async_copy(src_vmem.at[pl.ds(r * ROW_SUB, ROW_SUB), :],
                              dst_hbm.at[pl.ds(d0, ROW_SUB), :], sem).start()


def _scatter_wait(src_vmem, dst_hbm, sem):
    pltpu.make_async_copy(src_vmem, dst_hbm.at[pl.ds(0, BM * ROW_SUB), :], sem).wait()


def _cast_pair_rows(dst_s, src_st):
    half = D_MODEL // 2
    for j in range(ROW_SUB):
        a = j * LANES
        dst_s[2 * a:2 * a + LANES, :] = src_st[a:a + LANES, :].astype(BF16)
        dst_s[2 * a + LANES:2 * a + 2 * LANES, :] = src_st[half + a:half + a + LANES, :].astype(BF16)


def _expert_kernel(be_ref, nxt_ref, nused_ref, segpar_ref,
                   dstp_ref, xs_ref, wg_ref, wu_ref, wd_ref,
                   out_ref,
                   wg_st, wu_st, wd_st, wg_s, wu_s, wd_s, ybuf0, ybuf1,
                   wsem, ssem):
    i = pl.program_id(0)
    nused = nused_ref[0]
    used = i < nused
    par = i % 2
    ic = jnp.minimum(i, be_ref.shape[0] - 1)
    e = be_ref[ic]
    first = jnp.logical_or(i == 0, e != be_ref[jnp.maximum(i - 1, 0)])

    def weight_copies(ex, slot):
        return (pltpu.make_async_copy(wg_ref.at[ex], wg_st.at[slot], wsem.at[slot, 0]),
                pltpu.make_async_copy(wu_ref.at[ex], wu_st.at[slot], wsem.at[slot, 1]),
                pltpu.make_async_copy(wd_ref.at[ex], wd_st.at[slot], wsem.at[slot, 2]))

    @pl.when(i == 0)
    def _():
        for cp in weight_copies(e, 0):
            cp.start()
        ybuf1[...] = jnp.zeros(ybuf1.shape, ybuf1.dtype)

    def load_weights(slot):
        nxt = nxt_ref[ic]

        @pl.when(nxt >= 0)
        def _():
            for cp in weight_copies(nxt, 1 - slot):
                cp.start()

        for cp in weight_copies(0, slot):
            cp.wait()
        _cast_pair_rows(wg_s, wg_st.at[slot])
        _cast_pair_rows(wu_s, wu_st.at[slot])
        wd_s[...] = wd_st[slot].astype(BF16)

    for slot in range(2):
        @pl.when(jnp.logical_and(jnp.logical_and(first, used), segpar_ref[ic] == slot))
        def _():
            load_weights(slot)

    def step(yb, yo, s):
        o = 1 - s

        @pl.when(i >= 1)
        def _():
            _scatter_wait(yb, out_ref, ssem.at[s])

        _row_scatter_start(yo, dstp_ref, out_ref, ssem.at[o])
        half = D_MODEL // 2
        cols = []
        for w in _load_row_tiles(xs_ref):
            hi, lo = _unpack_pair(w)
            cols += [hi.astype(BF16), lo.astype(BF16)]
        xrow = jnp.concatenate(cols, axis=1)
        g = _dot(xrow, wg_s[...])
        u = _dot(xrow, wu_s[...])
        hm = (g * jax.nn.sigmoid(g) * u).astype(BF16)
        y = _dot(hm, wd_s[...])
        _store_row_tiles(yb, _pack_pair(y[:, :half], y[:, half:]))

    @pl.when(jnp.logical_and(used, par == 0))
    def _():
        step(ybuf0, ybuf1, 0)

    @pl.when(jnp.logical_and(used, par == 1))
    def _():
        step(ybuf1, ybuf0, 1)

    def drain(yb, yo, s):
        o = 1 - s
        _scatter_wait(yb, out_ref, ssem.at[s])
        _row_scatter_start(yo, dstp_ref, out_ref, ssem.at[o])
        _scatter_wait(yo, out_ref, ssem.at[o])

    @pl.when(jnp.logical_and(i == nused, par == 0))
    def _():
        drain(ybuf0, ybuf1, 0)

    @pl.when(jnp.logical_and(i == nused, par == 1))
    def _():
        drain(ybuf1, ybuf0, 1)


def _experts(block_expert, next_expert, n_used, seg_par, rows_dst, xs, wg, wu, wd, n_out_rows):
    nb = block_expert.shape[0]
    dst3 = rows_dst.reshape(nb + 1, 1, BM)
    grid_spec = pltpu.PrefetchScalarGridSpec(
        num_scalar_prefetch=4,
        grid=(nb + 1,),
        in_specs=[
            pl.BlockSpec((1, 1, BM), lambda i, *_: (i, 0, 0), memory_space=pltpu.SMEM),
            pl.BlockSpec((BM * ROW_SUB, LANES), lambda i, be, nx, nu, sp: (jnp.minimum(i, nu[0] - 1), 0)),
            pl.BlockSpec(memory_space=pl.ANY),
            pl.BlockSpec(memory_space=pl.ANY),
            pl.BlockSpec(memory_space=pl.ANY),
        ],
        out_specs=pl.BlockSpec(memory_space=pl.ANY),
        scratch_shapes=[
            pltpu.VMEM((2, D_MODEL, D_EXPERT), F32),
            pltpu.VMEM((2, D_MODEL, D_EXPERT), F32),
            pltpu.VMEM((2, D_EXPERT, D_MODEL), F32),
            pltpu.VMEM((D_MODEL, D_EXPERT), BF16),
            pltpu.VMEM((D_MODEL, D_EXPERT), BF16),
            pltpu.VMEM((D_EXPERT, D_MODEL), BF16),
            pltpu.VMEM((BM * ROW_SUB, LANES), U32),
            pltpu.VMEM((BM * ROW_SUB, LANES), U32),
            pltpu.SemaphoreType.DMA((2, 3)),
            pltpu.SemaphoreType.DMA((2,)),
        ],
    )
    return pl.pallas_call(
        _expert_kernel,
        grid_spec=grid_spec,
        out_shape=jax.ShapeDtypeStruct((n_out_rows * ROW_SUB, LANES), U32),
        compiler_params=_cparams(("arbitrary",)),
        name="experts",
    )(block_expert, next_expert, n_used, seg_par, dst3, xs, wg, wu, wd)


def _combine_kernel(x2_ref, gate_ref, gfin_ref, y0_ref, y1_ref, o_ref):
    gate = gate_ref[...]
    half = D_MODEL // 2
    y_hi = x2_ref[:, :half]
    y_lo = x2_ref[:, half:]
    for k, y_ref in enumerate((y0_ref, y1_ref)):
        pairs = [_unpack_pair(w) for w in _load_row_tiles(y_ref)]
        hi = jnp.concatenate([p[0] for p in pairs], axis=1)
        lo = jnp.concatenate([p[1] for p in pairs], axis=1)
        gk = gate[:, k:k + 1]
        y_hi = y_hi + gk * hi
        y_lo = y_lo + gk * lo
    ms = (jnp.sum(y_hi * y_hi, axis=-1, keepdims=True)
          + jnp.sum(y_lo * y_lo, axis=-1, keepdims=True)) * (1.0 / D_MODEL)
    r = lax.rsqrt(ms + EPS)
    o_ref[:, :half] = y_hi * r * gfin_ref[:, :half]
    o_ref[:, half:] = y_lo * r * gfin_ref[:, half:]


def _combine(x2, gate, gfin, out2):
    t, d = x2.shape
    tm = TM_ROW
    nb = t // tm
    row = lambda i: (i, 0)
    return pl.pallas_call(
        _combine_kernel,
        grid=(nb,),
        in_specs=[
            pl.BlockSpec((tm, d), row),
            pl.BlockSpec((tm, LANES), row),
            _const_spec((1, d)),
            pl.BlockSpec((tm * ROW_SUB, LANES), row),
            pl.BlockSpec((tm * ROW_SUB, LANES), lambda i: (nb + i, 0)),
        ],
        out_specs=pl.BlockSpec((tm, d), row),
        out_shape=jax.ShapeDtypeStruct((t, d), F32),
        compiler_params=_cparams(("arbitrary",)),
        name="combine",
    )(x2, gate, gfin, out2, out2)


def _rot_half_cols(w):
    half = QK_ROPE // 2
    return jnp.concatenate([-w[..., half:], w[..., :half]], axis=-1)


def kernel(x, mem, g_norm_mix, w_in, g_v, b_v, w_spatial, b_spatial, g_q_lora, w_uq, g_kv_lora, w_ukv, g_out_gmlp, g_out_mla, w_out, g_norm_xattn, g_norm_mem, w_mq, w_mk, w_mv, w_mo, g_norm_moe, w_router_group, b_router_group, w_router_expert, b_router_expert, w_exp_gate, w_exp_up, w_exp_down, g_final):
    batch, seq, d = x.shape
    t = batch * seq
    x2d = x.reshape(t, d)
    r2 = lambda a: a.reshape(1, -1)

    w_kr = w_in[:, C_KR:C_KR + QK_ROPE]
    win = w_in[:, :C_KR].astype(BF16)
    wkr = jnp.concatenate([w_kr, _rot_half_cols(w_kr)], axis=1).astype(BF16)
    wq3 = w_uq.reshape(Q_LORA, MLA_HEADS, QK_NOPE + QK_ROPE)
    wq_rope = wq3[..., QK_NOPE:]
    wuq_ext = jnp.concatenate([wq3, _rot_half_cols(wq_rope)], axis=-1)
    wuq_ext = wuq_ext.reshape(Q_LORA, MLA_HEADS * QK_PAD).astype(BF16)
    wukv = w_ukv.astype(BF16)
    causal = jnp.tril(jnp.ones((CHUNK, CHUNK), dtype=bool))
    ws = jnp.where(causal[None], w_spatial, 0.0).astype(BF16)
    bsp = jnp.repeat(b_spatial.T, LANES, axis=1)
    wr = jnp.concatenate(
        [w_router_group, w_router_expert,
         jnp.zeros((d, LANES - N_GROUPS - N_EXPERTS), F32)], axis=1).astype(BF16)
    br = jnp.concatenate(
        [b_router_group, b_router_expert, jnp.zeros((LANES - N_GROUPS - N_EXPERTS,), F32)]).reshape(1, LANES)

    pos = jnp.arange(seq, dtype=F32)
    inv_freq = ROPE_THETA ** (-jnp.arange(0, QK_ROPE, 2, dtype=F32) / QK_ROPE)
    ang = pos[:, None] * inv_freq[None, :]
    cos, sin = jnp.cos(ang), jnp.sin(ang)
    cs = jnp.concatenate([cos, cos, sin, sin], axis=1)

    an, q, k, v = _front(x2d, r2(g_norm_mix), win, wkr, r2(g_v), r2(b_v), ws, bsp, r2(g_out_gmlp),
                         r2(g_q_lora), wuq_ext, r2(g_kv_lora), wukv, cs, seq)
    m = _mla_attn(q, k, v, batch, seq)
    kmem, vmem = _mem_kv(mem, r2(g_norm_mem), w_mk.astype(BF16), w_mv.astype(BF16))
    x2, hp, idx, gate, cnt = _mid(an, m, x2d, w_out.astype(BF16), r2(g_out_mla), r2(g_norm_xattn),
                             w_mq.astype(BF16), kmem, vmem, w_mo.astype(BF16), r2(g_norm_moe),
                             wr, br, seq)

    n_assign = t * TOP_K
    p_rows = n_assign + N_EXPERTS * BM
    nb = p_rows // BM
    counts = cnt[0, :N_EXPERTS]
    padded = ((counts + BM - 1) // BM) * BM
    padded_ends = jnp.cumsum(padded)
    padded_starts = padded_ends - padded
    n_used = (padded_ends[-1:] // BM).astype(jnp.int32)
    blk_row = jnp.arange(nb, dtype=jnp.int32) * BM
    block_expert = jnp.minimum(jnp.sum(padded_ends[None, :] <= blk_row[:, None], axis=1),
                               N_EXPERTS - 1).astype(jnp.int32)
    be_onehot = block_expert[:, None] == jnp.arange(N_EXPERTS, dtype=jnp.int32)[None, :]
    nxt_blk = jnp.sum(jnp.where(be_onehot, padded_ends[None, :], 0), axis=1) // BM
    nxt_onehot = nxt_blk[:, None] == jnp.arange(nb, dtype=jnp.int32)[None, :]
    next_expert = jnp.where(nxt_blk < n_used[0],
                            jnp.sum(jnp.where(nxt_onehot, block_expert[None, :], 0), axis=1),
                            -1).astype(jnp.int32)
    seg_id = jnp.cumsum((counts > 0).astype(jnp.int32)) - 1
    seg_par = (jnp.sum(jnp.where(be_onehot, seg_id[None, :], 0), axis=1) & 1).astype(jnp.int32)
    n_dst = -(-(p_rows + BM) // SMEM_1D_TILE) * SMEM_1D_TILE
    dflt = (n_assign + jnp.arange(n_dst, dtype=jnp.int32) % BM) * ROW_SUB

    xs, rows_dst = _dispatch(padded_starts.astype(jnp.int32), (padded_starts + counts).astype(jnp.int32),
                             (padded - counts).astype(jnp.int32), n_used, idx[:, :2 * TOP_K], hp, dflt,
                             p_rows)
    out2 = _experts(block_expert, next_expert, n_used, seg_par, rows_dst[:p_rows + BM], xs,
                    w_exp_gate, w_exp_up, w_exp_down, n_assign + BM)
    out = _combine(x2, gate, r2(g_final), out2)
    return out.reshape(batch, seq, d)
```

```python
import math

import jax
import jax.numpy as jnp
import numpy as np
from jax import lax
from jax.experimental import pallas as pl
from jax.experimental.pallas import tpu as pltpu

D_MODEL = 2048
CHUNK = 128
GMLP_HEADS = 8
GMLP_WIDTH = 1024
MLA_HEADS = 8
Q_LORA = 512
KV_LORA = 256
QK_NOPE = 128
QK_ROPE = 64
V_DIM = 128
MLA_WIDTH = MLA_HEADS * V_DIM
ROPE_THETA = 10000.0
MEM_HEADS = 4
MEM_HEAD_DIM = 128
MEM_WIDTH = MEM_HEADS * MEM_HEAD_DIM
N_GROUPS = 8
EXPERTS_PER_GROUP = 8
N_EXPERTS = 64
TOP_K = 2
D_EXPERT = 512
EPS = 1e-6
LN_EPS = 1e-5

LANES = 128
QK_PAD = 256
VMEM_LIMIT = 56 * 1024 * 1024

C_UV = 0
C_Q = 2 * GMLP_WIDTH
C_KV = C_Q + Q_LORA
C_KR = C_KV + KV_LORA
SMEM_1D_TILE = 1024

TM_FRONT = 512
TM_MID = 512
SUB_MID = 256
TQ = 256
ATTN_GROUP = 2
BM = 256
TM_ROW = 256
TM_DISPATCH = 512

F32 = jnp.float32
BF16 = jnp.bfloat16
U32 = jnp.uint32
NEG = float(np.finfo(np.float32).min)


def _cparams(sem):
    return pltpu.CompilerParams(dimension_semantics=sem, vmem_limit_bytes=VMEM_LIMIT)


def _const_spec(shape):
    n = len(shape)
    return pl.BlockSpec(shape, lambda *_: (0,) * n, pipeline_mode=pl.Buffered(1))


def _rms(x, g):
    ms = jnp.mean(x * x, axis=-1, keepdims=True)
    return x * lax.rsqrt(ms + EPS) * g


def _gelu_tanh(x):
    c = math.sqrt(2.0 / math.pi)
    return 0.5 * x * (1.0 + jnp.tanh(c * (x + 0.044715 * (x * x * x))))


def _dot(a, b):
    return jnp.dot(a, b, preferred_element_type=F32)


def _dot_nt(a, b):
    return lax.dot_general(a, b, (((1,), (1,)), ((), ())), preferred_element_type=F32)


def _pack_pair(hi, lo):
    hb = pltpu.bitcast(hi.astype(BF16).astype(F32), U32)
    lb = pltpu.bitcast(lo.astype(BF16).astype(F32), U32)
    return hb | (lb >> 16)


def _unpack_pair(w):
    hi = pltpu.bitcast(w & jnp.uint32(0xFFFF0000), F32)
    lo = pltpu.bitcast(w << 16, F32)
    return hi, lo


ROW_SUB = 8
ROW_WORDS = ROW_SUB * LANES


def _store_row_tiles(ref, packed):
    m = packed.shape[0]
    for j in range(ROW_SUB):
        ref[pl.ds(j, m, stride=ROW_SUB), :] = packed[:, j * LANES:(j + 1) * LANES]


def _load_row_tiles(ref):
    m = ref.shape[0] // ROW_SUB
    return [ref[pl.ds(j, m, stride=ROW_SUB), :] for j in range(ROW_SUB)]


def _front_kernel(x_ref, gmix_ref, win_ref, wkr_ref, gv_ref, bv_ref, ws_ref, bsp_ref, goutg_ref,
                  gq_ref, wuq_ref, gkv_ref, wukv_ref, cs_ref,
                  an_ref, q_ref, k_ref, v_ref):
    tm = x_ref.shape[0]
    xn = _rms(x_ref[...], gmix_ref[...]).astype(BF16)
    z = _dot(xn, win_ref[...])
    zk = _dot(xn, wkr_ref[...])

    u = _gelu_tanh(z[:, :GMLP_WIDTH])
    v = _gelu_tanh(z[:, GMLP_WIDTH:2 * GMLP_WIDTH])
    mu = jnp.mean(v, axis=-1, keepdims=True)
    vc = v - mu
    var = jnp.mean(vc * vc, axis=-1, keepdims=True)
    vn = (vc * lax.rsqrt(var + LN_EPS) * gv_ref[...] + bv_ref[...]).astype(BF16)
    a_chunks = []
    for c in range(tm // CHUNK):
        r0 = c * CHUNK
        cols = []
        for g in range(GMLP_HEADS):
            c0 = g * LANES
            sv = _dot(ws_ref[g], vn[r0:r0 + CHUNK, c0:c0 + LANES])
            cols.append(sv)
        sv_all = jnp.concatenate(cols, axis=1) + bsp_ref[...]
        a_chunks.append(u[r0:r0 + CHUNK, :] * sv_all)
    a = jnp.concatenate(a_chunks, axis=0)
    an_ref[...] = _rms(a, goutg_ref[...]).astype(BF16)

    cs = cs_ref[...]
    scale = 1.0 / math.sqrt(QK_NOPE + QK_ROPE)
    cqn = _rms(z[:, C_Q:C_KV], gq_ref[...]).astype(BF16)
    q = _dot(cqn, wuq_ref[...]) * scale
    ckvn = _rms(z[:, C_KV:C_KR], gkv_ref[...]).astype(BF16)
    kv = _dot(ckvn, wukv_ref[...])
    lane = lax.broadcasted_iota(jnp.int32, (tm, LANES), 1)
    kr = zk * cs
    kr = jnp.where(lane < QK_ROPE, kr + pltpu.roll(kr, QK_ROPE, 1), 0.0).astype(BF16)
    for h in range(MLA_HEADS):
        b0 = h * QK_PAD
        qr = q[:, b0 + LANES:b0 + QK_PAD] * cs
        qr = qr + pltpu.roll(qr, QK_ROPE, 1)
        q_ref[:, b0:b0 + LANES] = q[:, b0:b0 + LANES].astype(BF16)
        q_ref[:, b0 + LANES:b0 + QK_PAD] = qr.astype(BF16)
        k_ref[:, b0:b0 + LANES] = kv[:, b0:b0 + LANES].astype(BF16)
        k_ref[:, b0 + LANES:b0 + QK_PAD] = kr
        v_ref[:, h * V_DIM:(h + 1) * V_DIM] = kv[:, b0 + LANES:b0 + QK_PAD].astype(BF16)


def _front(x2d, gmix, win, wkr, gv, bv, ws, bsp, goutg, gq, wuq_ext, gkv, wukv, cs, seq):
    t = x2d.shape[0]
    tm = TM_FRONT
    nseq = seq // tm
    row = lambda i: (i, 0)
    return pl.pallas_call(
        _front_kernel,
        grid=(t // tm,),
        in_specs=[
            pl.BlockSpec((tm, D_MODEL), row),
            _const_spec((1, D_MODEL)),
            _const_spec((D_MODEL, C_KR)),
            _const_spec((D_MODEL, LANES)),
            _const_spec((1, GMLP_WIDTH)),
            _const_spec((1, GMLP_WIDTH)),
            _const_spec((GMLP_HEADS, CHUNK, CHUNK)),
            _const_spec((CHUNK, GMLP_WIDTH)),
            _const_spec((1, GMLP_WIDTH)),
            _const_spec((1, Q_LORA)),
            _const_spec((Q_LORA, MLA_HEADS * QK_PAD)),
            _const_spec((1, KV_LORA)),
            _const_spec((KV_LORA, MLA_HEADS * QK_PAD)),
            pl.BlockSpec((tm, LANES), lambda i: (i % nseq, 0)),
        ],
        out_specs=[
            pl.BlockSpec((tm, GMLP_WIDTH), row),
            pl.BlockSpec((tm, MLA_HEADS * QK_PAD), row),
            pl.BlockSpec((tm, MLA_HEADS * QK_PAD), row),
            pl.BlockSpec((tm, MLA_WIDTH), row),
        ],
        out_shape=[
            jax.ShapeDtypeStruct((t, GMLP_WIDTH), BF16),
            jax.ShapeDtypeStruct((t, MLA_HEADS * QK_PAD), BF16),
            jax.ShapeDtypeStruct((t, MLA_HEADS * QK_PAD), BF16),
            jax.ShapeDtypeStruct((t, MLA_WIDTH), BF16),
        ],
        compiler_params=_cparams(("arbitrary",)),
        name="front",
    )(x2d, gmix, win, wkr, gv, bv, ws, bsp, goutg, gq, wuq_ext, gkv, wukv, cs)


def _attn_kernel(q_ref, k_ref, v_ref, o_ref):
    s_len = q_ref.shape[0]
    row = lax.broadcasted_iota(jnp.int32, (TQ, TQ), 0)
    col = lax.broadcasted_iota(jnp.int32, (TQ, TQ), 1)
    causal = col <= row
    n_blk = s_len // TQ
    for g0 in range(0, n_blk, ATTN_GROUP):
        blocks = list(range(g0, min(g0 + ATTN_GROUP, n_blk)))
        qb = {qi: q_ref[qi * TQ:(qi + 1) * TQ, :] for qi in blocks}
        sd = {qi: jnp.where(causal, _dot_nt(qb[qi], k_ref[qi * TQ:(qi + 1) * TQ, :]), NEG)
              for qi in blocks}
        so = {qi: _dot_nt(qb[qi], k_ref[0:qi * TQ, :]) for qi in blocks if qi > 0}
        m = {qi: jnp.max(sd[qi], axis=-1, keepdims=True) for qi in blocks}
        for qi in so:
            m[qi] = jnp.maximum(m[qi], jnp.max(so[qi], axis=-1, keepdims=True))
        pd = {qi: jnp.exp(sd[qi] - m[qi]) for qi in blocks}
        po = {qi: jnp.exp(so[qi] - m[qi]) for qi in so}
        l = {qi: jnp.sum(pd[qi], axis=-1, keepdims=True) for qi in blocks}
        acc = {qi: _dot(pd[qi].astype(BF16), v_ref[qi * TQ:(qi + 1) * TQ, :]) for qi in blocks}
        for qi in so:
            l[qi] = l[qi] + jnp.sum(po[qi], axis=-1, keepdims=True)
            acc[qi] = acc[qi] + _dot(po[qi].astype(BF16), v_ref[0:qi * TQ, :])
        for qi in blocks:
            o_ref[qi * TQ:(qi + 1) * TQ, :] = (acc[qi] / l[qi]).astype(o_ref.dtype)


def _mla_attn(q, k, v, batch, seq):
    q3 = q.reshape(batch, seq, MLA_HEADS * QK_PAD)
    k3 = k.reshape(batch, seq, MLA_HEADS * QK_PAD)
    v3 = v.reshape(batch, seq, MLA_WIDTH)
    hmap = lambda b, h: (b, 0, h)
    out = pl.pallas_call(
        _attn_kernel,
        grid=(batch, MLA_HEADS),
        in_specs=[
            pl.BlockSpec((None, seq, QK_PAD), hmap),
            pl.BlockSpec((None, seq, QK_PAD), hmap),
            pl.BlockSpec((None, seq, V_DIM), hmap),
        ],
        out_specs=pl.BlockSpec((None, seq, V_DIM), hmap),
        out_shape=jax.ShapeDtypeStruct((batch, seq, MLA_WIDTH), BF16),
        compiler_params=_cparams(("arbitrary", "arbitrary")),
        name="mla_attn",
    )(q3, k3, v3)
    return out.reshape(batch * seq, MLA_WIDTH)


def _memkv_kernel(mem_ref, g_ref, wk_ref, wv_ref, k_ref, v_ref):
    mn = _rms(mem_ref[...], g_ref[...]).astype(BF16)
    k_ref[...] = _dot(mn, wk_ref[...]).astype(BF16)
    v_ref[...] = _dot(mn, wv_ref[...]).astype(BF16)


def _mem_kv(mem, g, wk, wv):
    b, m, d = mem.shape
    bmap = lambda i: (i, 0, 0)
    return pl.pallas_call(
        _memkv_kernel,
        grid=(b,),
        in_specs=[
            pl.BlockSpec((None, m, d), bmap),
            _const_spec((1, d)),
            _const_spec((d, MEM_WIDTH)),
            _const_spec((d, MEM_WIDTH)),
        ],
        out_specs=[pl.BlockSpec((None, m, MEM_WIDTH), bmap)] * 2,
        out_shape=[jax.ShapeDtypeStruct((b, m, MEM_WIDTH), BF16)] * 2,
        compiler_params=_cparams(("arbitrary",)),
        name="mem_kv",
    )(mem, g, wk, wv)


def _row_to_col(x_row):
    r = lax.broadcasted_iota(jnp.int32, (LANES, LANES), 0)
    c = lax.broadcasted_iota(jnp.int32, (LANES, LANES), 1)
    return jnp.sum(jnp.where(r == c, jnp.broadcast_to(x_row, (LANES, LANES)), 0.0), axis=1, keepdims=True)


def _route(lg, state):
    tm = lg.shape[0]
    lane_i = lax.broadcasted_iota(jnp.int32, (tm, LANES), 1)
    lane = lane_i.astype(F32)
    big = float(LANES)
    gmask = lane_i < N_GROUPS
    gl = jnp.where(gmask, lg, NEG)
    gmax = jnp.max(gl, axis=-1, keepdims=True)
    grp = jnp.min(jnp.where(gl == gmax, lane, big), axis=-1, keepdims=True)
    gsum = jnp.sum(jnp.where(gmask, jnp.exp(gl - gmax), 0.0), axis=-1, keepdims=True)
    p_grp = 1.0 / gsum

    lo = (grp + 1.0) * EXPERTS_PER_GROUP
    emask = jnp.logical_and(lane >= lo, lane < lo + EXPERTS_PER_GROUP)
    el = jnp.where(emask, lg, NEG)
    emax = jnp.max(el, axis=-1, keepdims=True)
    ee = jnp.where(emask, jnp.exp(el - emax), 0.0)
    ep = ee / jnp.sum(ee, axis=-1, keepdims=True)
    ep = jnp.where(emask, ep, -1.0)
    p1 = jnp.max(ep, axis=-1, keepdims=True)
    i1 = jnp.min(jnp.where(ep == p1, lane, big), axis=-1, keepdims=True)
    ep2 = jnp.where(lane == i1, -1.0, ep)
    p2 = jnp.max(ep2, axis=-1, keepdims=True)
    i2 = jnp.min(jnp.where(ep2 == p2, lane, big), axis=-1, keepdims=True)
    den = p1 + p2
    g1 = p_grp * (p1 / den)
    g2 = p_grp * (p2 / den)
    e1 = i1 - N_GROUPS
    e2 = i2 - N_GROUPS
    gate = jnp.where(lane_i == 0, g1, jnp.where(lane_i == 1, g2, 0.0))

    cnt, page, npg, tbl = state
    oh1 = (lane == e1).astype(F32)
    oh2 = (lane == e2).astype(F32)
    both = oh1 + oh2
    r_i = lax.broadcasted_iota(jnp.int32, (tm, tm), 0)
    c_i = lax.broadcasted_iota(jnp.int32, (tm, tm), 1)
    before = (c_i < r_i).astype(BF16)
    prefix = _dot(before, both.astype(BF16))
    tc = jnp.sum(both, axis=0, keepdims=True)
    fill = cnt - BM * jnp.floor(cnt * (1.0 / BM))
    need = jnp.logical_and(tc > 0, jnp.logical_or(fill == 0, fill + tc > BM))
    need_f = need.astype(F32)
    e_r = lax.broadcasted_iota(jnp.int32, (LANES, LANES), 0)
    e_c = lax.broadcasted_iota(jnp.int32, (LANES, LANES), 1)
    lower_e = (e_r < e_c).astype(BF16)
    excl = _dot(jnp.broadcast_to(need_f, (ROW_SUB, LANES)).astype(BF16), lower_e)[0:1, :]
    newpage = npg + excl
    pos = fill + prefix
    in_cur = jnp.logical_and(fill > 0, pos < BM)
    slotval = (jnp.where(in_cur, page, newpage) * BM + jnp.where(pos >= BM, pos - BM, pos))
    s1 = jnp.sum(oh1 * slotval, axis=-1, keepdims=True)
    s2 = jnp.sum(oh2 * slotval, axis=-1, keepdims=True)
    slots = jnp.where(lane_i == 0, s1, jnp.where(lane_i == 1, s2, 0.0))

    ordinal = jnp.floor((cnt + (BM - 1)) * (1.0 / BM))
    hit = jnp.logical_and(_row_to_col(need_f) > 0, e_c.astype(F32) == _row_to_col(ordinal))
    tbl = jnp.where(hit, _row_to_col(newpage), tbl)
    new_state = (cnt + tc, jnp.where(need, newpage, page),
                 npg + jnp.sum(need_f, axis=-1, keepdims=True), tbl)
    return slots, gate, new_state


def _mid_kernel(an_ref, m_ref, x_ref, wout_ref, goutm_ref, gxa_ref, wmq_ref, kmem_ref, vmem_ref,
                wmo_ref, gmoe_ref, wr_ref, br_ref, dflt_ref,
                x2_ref, gate_ref, cnt_ref, tbl_ref, xs_ref, rdst_ref,
                state_ref, tbl_s, hp_s, slot_v, slot_s, rsem, ssem, isem):
    i = pl.program_id(0)
    n_steps = pl.num_programs(0)
    tm = x_ref.shape[0]
    n_tok = n_steps * tm
    par = i % 2

    def rows_wait(p):
        for k in range(TOP_K):
            pltpu.make_async_copy(hp_s.at[p], xs_ref.at[pl.ds(0, tm * ROW_SUB), :], rsem.at[p]).wait()

    def slot_copy(p):
        return pltpu.make_async_copy(slot_v, slot_s.at[p], ssem.at[p])

    def issue_row(p, step, t):
        for k in range(TOP_K):
            slot = slot_s[p, k, t]
            pltpu.make_async_copy(
                hp_s.at[p, pl.ds(t * ROW_SUB, ROW_SUB), :],
                xs_ref.at[pl.ds(pl.multiple_of(slot * ROW_SUB, ROW_SUB), ROW_SUB), :],
                rsem.at[p]).start()
            rdst_ref[BM + slot] = (k * n_tok + step * tm + t) * ROW_SUB

    @pl.when(i == 0)
    def _():
        state_ref[...] = jnp.zeros(state_ref.shape, state_ref.dtype)
        tbl_s[...] = jnp.zeros(tbl_s.shape, tbl_s.dtype)
        preset = pltpu.make_async_copy(dflt_ref, rdst_ref, isem)
        preset.start()
        hp_s[1] = jnp.zeros(hp_s.shape[1:], hp_s.dtype)
        spare0 = xs_ref.shape[0] // ROW_SUB - TOP_K * tm

        def prime(t, c):
            for k in range(TOP_K):
                slot_s[1, k, t] = spare0 + k * tm + t
            return c

        lax.fori_loop(0, tm, prime, 0)
        preset.wait()

    @pl.when(i >= 1)
    def _():
        rows_wait(par)
        slot_copy(1 - par).wait()

    for t in range(tm):
        issue_row(1 - par, i - 1, t)

    subs = [pl.ds(r0, SUB_MID) for r0 in range(0, x_ref.shape[0], SUB_MID)]
    mn = [_rms(m_ref[r, :].astype(F32), goutm_ref[...]).astype(BF16) for r in subs]
    x1 = [x_ref[r, :] + _dot(an_ref[r, :], wout_ref[:GMLP_WIDTH, :]) + _dot(a, wout_ref[GMLP_WIDTH:, :])
          for r, a in zip(subs, mn)]
    h2 = [_rms(a, gxa_ref[...]).astype(BF16) for a in x1]
    q2 = [(_dot(a, wmq_ref[...]) * (1.0 / math.sqrt(MEM_HEAD_DIM))).astype(BF16) for a in h2]

    def mem_attention(q):
        outs = []
        for h in range(MEM_HEADS):
            c0 = h * MEM_HEAD_DIM
            s = _dot_nt(q[:, c0:c0 + MEM_HEAD_DIM], kmem_ref[:, c0:c0 + MEM_HEAD_DIM])
            m = jnp.max(s, axis=-1, keepdims=True)
            p = jnp.exp(s - m)
            l = jnp.sum(p, axis=-1, keepdims=True)
            outs.append(_dot(p.astype(BF16), vmem_ref[:, c0:c0 + MEM_HEAD_DIM]) / l)
        return jnp.concatenate(outs, axis=1).astype(BF16)

    o = [mem_attention(q) for q in q2]
    x2 = [a + _dot(b, wmo_ref[...]) for a, b in zip(x1, o)]
    for r, a in zip(subs, x2):
        x2_ref[r, :] = a
    h3 = [_rms(a, gmoe_ref[...]) for a in x2]
    half = D_MODEL // 2
    for n, a in enumerate(h3):
        _store_row_tiles(hp_s.at[par, pl.ds(n * SUB_MID * ROW_SUB, SUB_MID * ROW_SUB), :],
                         _pack_pair(a[:, :half], a[:, half:]))
    lg = [_dot(a.astype(BF16), wr_ref[...]) + br_ref[...] for a in h3]
    state = (state_ref[0:1, :], state_ref[1:2, :], state_ref[2:3, :], tbl_s[...])
    for n, (r, a) in enumerate(zip(subs, lg)):
        slots, gate, state = _route(a, state)
        gate_ref[r, :] = gate
        slot_v[:, n * SUB_MID:(n + 1) * SUB_MID] = slots.T[0:ROW_SUB, :].astype(jnp.int32)
    slot_copy(par).start()
    state_ref[0:1, :] = state[0]
    state_ref[1:2, :] = state[1]
    state_ref[2:3, :] = state[2]
    tbl_s[...] = state[3]
    cnt_ref[...] = jnp.broadcast_to(state[0], cnt_ref.shape).astype(jnp.int32)
    tbl_ref[...] = state[3].astype(jnp.int32)

    @pl.when(i == n_steps - 1)
    def _():
        slot_copy(par).wait()

        def last_rows(t, c):
            issue_row(par, i, t)
            return c

        lax.fori_loop(0, tm, last_rows, 0)
        rows_wait(1 - par)
        rows_wait(par)


def _mid(an, m, x2d, wout, goutm, gxa, wmq, kmem, vmem, wmo, gmoe, wr, br, dflt, seq, p_rows):
    t = x2d.shape[0]
    tm = TM_MID
    xs_rows = p_rows + TOP_K * tm
    nseq = seq // tm
    mem_len = kmem.shape[1]
    row = lambda i: (i, 0)
    bmap = lambda i: (i // nseq, 0, 0)
    return pl.pallas_call(
        _mid_kernel,
        grid=(t // tm,),
        in_specs=[
            pl.BlockSpec((tm, GMLP_WIDTH), row),
            pl.BlockSpec((tm, MLA_WIDTH), row),
            pl.BlockSpec((tm, D_MODEL), row),
            _const_spec((D_MODEL, D_MODEL)),
            _const_spec((1, MLA_WIDTH)),
            _const_spec((1, D_MODEL)),
            _const_spec((D_MODEL, MEM_WIDTH)),
            pl.BlockSpec((None, mem_len, MEM_WIDTH), bmap),
            pl.BlockSpec((None, mem_len, MEM_WIDTH), bmap),
            _const_spec((MEM_WIDTH, D_MODEL)),
            _const_spec((1, D_MODEL)),
            _const_spec((D_MODEL, LANES)),
            _const_spec((1, LANES)),
            pl.BlockSpec(memory_space=pl.ANY),
        ],
        out_specs=[
            pl.BlockSpec((tm, D_MODEL), row),
            pl.BlockSpec((tm, LANES), row),
            pl.BlockSpec((ROW_SUB, LANES), lambda i: (0, 0)),
            pl.BlockSpec((LANES, LANES), lambda i: (0, 0)),
            pl.BlockSpec(memory_space=pl.ANY),
            pl.BlockSpec(memory_space=pltpu.SMEM),
        ],
        out_shape=[
            jax.ShapeDtypeStruct((t, D_MODEL), F32),
            jax.ShapeDtypeStruct((t, LANES), F32),
            jax.ShapeDtypeStruct((ROW_SUB, LANES), jnp.int32),
            jax.ShapeDtypeStruct((LANES, LANES), jnp.int32),
            jax.ShapeDtypeStruct((xs_rows * ROW_SUB, LANES), U32),
            jax.ShapeDtypeStruct(dflt.shape, jnp.int32),
        ],
        scratch_shapes=[
            pltpu.VMEM((ROW_SUB, LANES), F32),
            pltpu.VMEM((LANES, LANES), F32),
            pltpu.VMEM((2, tm * ROW_SUB, LANES), U32),
            pltpu.VMEM((ROW_SUB, tm), jnp.int32),
            pltpu.SMEM((2, ROW_SUB, tm), jnp.int32),
            pltpu.SemaphoreType.DMA((2,)),
            pltpu.SemaphoreType.DMA((2,)),
            pltpu.SemaphoreType.DMA(()),
        ],
        compiler_params=_cparams(("arbitrary",)),
        name="mid",
    )(an, m, x2d, wout, goutm, gxa, wmq, kmem, vmem, wmo, gmoe, wr, br, dflt)


PAD_BITS = tuple(1 << b for b in reversed(range(BM.bit_length() - 1)))


def _padfill_kernel(padfrom_ref, padn_ref, npages_ref, xs_in_ref, xs_ref, zero_buf, zsem):
    del xs_in_ref
    zero_buf[...] = jnp.zeros(zero_buf.shape, zero_buf.dtype)

    def pad_copy(e, bit):
        n = padn_ref[e]
        off = padfrom_ref[e] + (n & ~(2 * bit - 1))
        return n & bit, pltpu.make_async_copy(
            zero_buf.at[pl.ds(0, bit * ROW_SUB), :],
            xs_ref.at[pl.ds(pl.multiple_of(off * ROW_SUB, ROW_SUB), bit * ROW_SUB), :], zsem)

    def fill(e, c):
        for bit in PAD_BITS:
            on, cp = pad_copy(e, bit)

            @pl.when(on != 0)
            def _():
                cp.start()
        return c

    def fill_wait(e, c):
        for bit in PAD_BITS:
            on, cp = pad_copy(e, bit)

            @pl.when(on != 0)
            def _():
                cp.wait()
        return c

    def tail_copy(b):
        return pltpu.make_async_copy(
            zero_buf, xs_ref.at[pl.ds(pl.multiple_of(b * (BM * ROW_SUB), BM * ROW_SUB), BM * ROW_SUB), :],
            zsem)

    def tail(b, c):
        tail_copy(b).start()
        return c

    def tail_wait(b, c):
        tail_copy(b).wait()
        return c

    n_pages = (xs_ref.shape[0] // ROW_SUB - TOP_K * TM_MID) // BM
    lax.fori_loop(0, N_EXPERTS, fill, 0)
    lax.fori_loop(npages_ref[0], n_pages, tail, 0)
    lax.fori_loop(0, N_EXPERTS, fill_wait, 0)
    lax.fori_loop(npages_ref[0], n_pages, tail_wait, 0)


def _padfill(pad_from, pad_n, n_pages, xs):
    grid_spec = pltpu.PrefetchScalarGridSpec(
        num_scalar_prefetch=3,
        grid=(1,),
        in_specs=[pl.BlockSpec(memory_space=pl.ANY)],
        out_specs=pl.BlockSpec(memory_space=pl.ANY),
        scratch_shapes=[
            pltpu.VMEM((BM * ROW_SUB, LANES), U32),
            pltpu.SemaphoreType.DMA(()),
        ],
    )
    return pl.pallas_call(
        _padfill_kernel,
        grid_spec=grid_spec,
        out_shape=jax.ShapeDtypeStruct(xs.shape, xs.dtype),
        input_output_aliases={3: 0},
        compiler_params=_cparams(("arbitrary",)),
        name="padfill",
    )(pad_from, pad_n, n_pages, xs)


def _row_scatter_start(src_vmem, idx_ref, dst_hbm, sem):
    for r in range(BM):
        d0 = pl.multiple_of(idx_ref[0, 0, r], ROW_SUB)
        pltpu.make_async_copy(src_vmem.at[pl.ds(r * ROW_SUB, ROW_SUB), :],
                              dst_hbm.at[pl.ds(d0, ROW_SUB), :], sem).start()


def _scatter_wait(src_vmem, dst_hbm, sem):
    pltpu.make_async_copy(src_vmem, dst_hbm.at[pl.ds(0, BM * ROW_SUB), :], sem).wait()


def _cast_pair_rows(dst_s, src_st):
    half = D_MODEL // 2
    for j in range(ROW_SUB):
        a = j * LANES
        dst_s[2 * a:2 * a + LANES, :] = src_st[a:a + LANES, :].astype(BF16)
        dst_s[2 * a + LANES:2 * a + 2 * LANES, :] = src_st[half + a:half + a + LANES, :].astype(BF16)


def _expert_kernel(be_ref, nxt_ref, nused_ref, segpar_ref, xblk_ref, dblk_ref,
                   dstp_ref, xs_ref, wg_ref, wu_ref, wd_ref,
                   out_ref,
                   wg_st, wu_st, wd_st, wg_s, wu_s, wd_s, ybuf0, ybuf1,
                   wsem, ssem):
    i = pl.program_id(0)
    nused = nused_ref[0]
    used = i < nused
    par = i % 2
    ic = jnp.minimum(i, be_ref.shape[0] - 1)
    e = be_ref[ic]
    first = jnp.logical_or(i == 0, e != be_ref[jnp.maximum(i - 1, 0)])

    def weight_copies(ex, slot):
        return (pltpu.make_async_copy(wg_ref.at[ex], wg_st.at[slot], wsem.at[slot, 0]),
                pltpu.make_async_copy(wu_ref.at[ex], wu_st.at[slot], wsem.at[slot, 1]),
                pltpu.make_async_copy(wd_ref.at[ex], wd_st.at[slot], wsem.at[slot, 2]))

    @pl.when(i == 0)
    def _():
        for cp in weight_copies(e, 0):
            cp.start()
        ybuf1[...] = jnp.zeros(ybuf1.shape, ybuf1.dtype)

    def load_weights(slot):
        nxt = nxt_ref[ic]

        @pl.when(nxt >= 0)
        def _():
            for cp in weight_copies(nxt, 1 - slot):
                cp.start()

        for cp in weight_copies(0, slot):
            cp.wait()
        _cast_pair_rows(wg_s, wg_st.at[slot])
        _cast_pair_rows(wu_s, wu_st.at[slot])
        wd_s[...] = wd_st[slot].astype(BF16)

    for slot in range(2):
        @pl.when(jnp.logical_and(jnp.logical_and(first, used), segpar_ref[ic] == slot))
        def _():
            load_weights(slot)

    def step(yb, yo, s):
        o = 1 - s

        @pl.when(i >= 1)
        def _():
            _scatter_wait(yb, out_ref, ssem.at[s])

        _row_scatter_start(yo, dstp_ref, out_ref, ssem.at[o])
        half = D_MODEL // 2
        cols = []
        for w in _load_row_tiles(xs_ref):
            hi, lo = _unpack_pair(w)
            cols += [hi.astype(BF16), lo.astype(BF16)]
        xrow = jnp.concatenate(cols, axis=1)
        g = _dot(xrow, wg_s[...])
        u = _dot(xrow, wu_s[...])
        hm = (g * jax.nn.sigmoid(g) * u).astype(BF16)
        y = _dot(hm, wd_s[...])
        _store_row_tiles(yb, _pack_pair(y[:, :half], y[:, half:]))

    @pl.when(jnp.logical_and(used, par == 0))
    def _():
        step(ybuf0, ybuf1, 0)

    @pl.when(jnp.logical_and(used, par == 1))
    def _():
        step(ybuf1, ybuf0, 1)

    def drain(yb, yo, s):
        o = 1 - s
        _scatter_wait(yb, out_ref, ssem.at[s])
        _row_scatter_start(yo, dstp_ref, out_ref, ssem.at[o])
        _scatter_wait(yo, out_ref, ssem.at[o])

    @pl.when(jnp.logical_and(i == nused, par == 0))
    def _():
        drain(ybuf0, ybuf1, 0)

    @pl.when(jnp.logical_and(i == nused, par == 1))
    def _():
        drain(ybuf1, ybuf0, 1)


def _experts(block_expert, next_expert, n_used, seg_par, x_blk, d_blk, rows_dst, xs, wg, wu, wd, n_out_rows):
    nb = block_expert.shape[0]
    dst3 = rows_dst.reshape(nb + 1, 1, BM)
    grid_spec = pltpu.PrefetchScalarGridSpec(
        num_scalar_prefetch=6,
        grid=(nb + 1,),
        in_specs=[
            pl.BlockSpec((1, 1, BM), lambda i, be, nx, nu, sp, xb, db: (db[i], 0, 0), memory_space=pltpu.SMEM),
            pl.BlockSpec((BM * ROW_SUB, LANES), lambda i, be, nx, nu, sp, xb, db: (xb[i], 0)),
            pl.BlockSpec(memory_space=pl.ANY),
            pl.BlockSpec(memory_space=pl.ANY),
            pl.BlockSpec(memory_space=pl.ANY),
        ],
        out_specs=pl.BlockSpec(memory_space=pl.ANY),
        scratch_shapes=[
            pltpu.VMEM((2, D_MODEL, D_EXPERT), F32),
            pltpu.VMEM((2, D_MODEL, D_EXPERT), F32),
            pltpu.VMEM((2, D_EXPERT, D_MODEL), F32),
            pltpu.VMEM((D_MODEL, D_EXPERT), BF16),
            pltpu.VMEM((D_MODEL, D_EXPERT), BF16),
            pltpu.VMEM((D_EXPERT, D_MODEL), BF16),
            pltpu.VMEM((BM * ROW_SUB, LANES), U32),
            pltpu.VMEM((BM * ROW_SUB, LANES), U32),
            pltpu.SemaphoreType.DMA((2, 3)),
            pltpu.SemaphoreType.DMA((2,)),
        ],
    )
    return pl.pallas_call(
        _expert_kernel,
        grid_spec=grid_spec,
        out_shape=jax.ShapeDtypeStruct((n_out_rows * ROW_SUB, LANES), U32),
        compiler_params=_cparams(("arbitrary",)),
        name="experts",
    )(block_expert, next_expert, n_used, seg_par, x_blk, d_blk, dst3, xs, wg, wu, wd)


def _combine_kernel(x2_ref, gate_ref, gfin_ref, y0_ref, y1_ref, o_ref):
    gate = gate_ref[...]
    half = D_MODEL // 2
    y_hi = x2_ref[:, :half]
    y_lo = x2_ref[:, half:]
    for k, y_ref in enumerate((y0_ref, y1_ref)):
        pairs = [_unpack_pair(w) for w in _load_row_tiles(y_ref)]
        hi = jnp.concatenate([p[0] for p in pairs], axis=1)
        lo = jnp.concatenate([p[1] for p in pairs], axis=1)
        gk = gate[:, k:k + 1]
        y_hi = y_hi + gk * hi
        y_lo = y_lo + gk * lo
    ms = (jnp.sum(y_hi * y_hi, axis=-1, keepdims=True)
          + jnp.sum(y_lo * y_lo, axis=-1, keepdims=True)) * (1.0 / D_MODEL)
    r = lax.rsqrt(ms + EPS)
    o_ref[:, :half] = y_hi * r * gfin_ref[:, :half]
    o_ref[:, half:] = y_lo * r * gfin_ref[:, half:]


def _combine(x2, gate, gfin, out2):
    t, d = x2.shape
    tm = TM_ROW
    nb = t // tm
    row = lambda i: (i, 0)
    return pl.pallas_call(
        _combine_kernel,
        grid=(nb,),
        in_specs=[
            pl.BlockSpec((tm, d), row),
            pl.BlockSpec((tm, LANES), row),
            _const_spec((1, d)),
            pl.BlockSpec((tm * ROW_SUB, LANES), row),
            pl.BlockSpec((tm * ROW_SUB, LANES), lambda i: (nb + i, 0)),
        ],
        out_specs=pl.BlockSpec((tm, d), row),
        out_shape=jax.ShapeDtypeStruct((t, d), F32),
        compiler_params=_cparams(("arbitrary",)),
        name="combine",
    )(x2, gate, gfin, out2, out2)


def _rot_half_cols(w):
    half = QK_ROPE // 2
    return jnp.concatenate([-w[..., half:], w[..., :half]], axis=-1)


def kernel(x, mem, g_norm_mix, w_in, g_v, b_v, w_spatial, b_spatial, g_q_lora, w_uq, g_kv_lora, w_ukv, g_out_gmlp, g_out_mla, w_out, g_norm_xattn, g_norm_mem, w_mq, w_mk, w_mv, w_mo, g_norm_moe, w_router_group, b_router_group, w_router_expert, b_router_expert, w_exp_gate, w_exp_up, w_exp_down, g_final):
    batch, seq, d = x.shape
    t = batch * seq
    x2d = x.reshape(t, d)
    r2 = lambda a: a.reshape(1, -1)

    w_kr = w_in[:, C_KR:C_KR + QK_ROPE]
    win = w_in[:, :C_KR].astype(BF16)
    wkr = jnp.concatenate([w_kr, _rot_half_cols(w_kr)], axis=1).astype(BF16)
    wq3 = w_uq.reshape(Q_LORA, MLA_HEADS, QK_NOPE + QK_ROPE)
    wq_rope = wq3[..., QK_NOPE:]
    wuq_ext = jnp.concatenate([wq3, _rot_half_cols(wq_rope)], axis=-1)
    wuq_ext = wuq_ext.reshape(Q_LORA, MLA_HEADS * QK_PAD).astype(BF16)
    wukv = w_ukv.astype(BF16)
    causal = jnp.tril(jnp.ones((CHUNK, CHUNK), dtype=bool))
    ws = jnp.where(causal[None], w_spatial, 0.0).astype(BF16)
    bsp = jnp.repeat(b_spatial.T, LANES, axis=1)
    wr = jnp.concatenate(
        [w_router_group, w_router_expert,
         jnp.zeros((d, LANES - N_GROUPS - N_EXPERTS), F32)], axis=1).astype(BF16)
    br = jnp.concatenate(
        [b_router_group, b_router_expert, jnp.zeros((LANES - N_GROUPS - N_EXPERTS,), F32)]).reshape(1, LANES)

    pos = jnp.arange(seq, dtype=F32)
    inv_freq = ROPE_THETA ** (-jnp.arange(0, QK_ROPE, 2, dtype=F32) / QK_ROPE)
    ang = pos[:, None] * inv_freq[None, :]
    cos, sin = jnp.cos(ang), jnp.sin(ang)
    cs = jnp.concatenate([cos, cos, sin, sin], axis=1)

    an, q, k, v = _front(x2d, r2(g_norm_mix), win, wkr, r2(g_v), r2(b_v), ws, bsp, r2(g_out_gmlp),
                         r2(g_q_lora), wuq_ext, r2(g_kv_lora), wukv, cs, seq)
    m = _mla_attn(q, k, v, batch, seq)
    kmem, vmem = _mem_kv(mem, r2(g_norm_mem), w_mk.astype(BF16), w_mv.astype(BF16))
    n_assign = t * TOP_K
    p_rows = n_assign + N_EXPERTS * BM
    nb = p_rows // BM
    n_dst = -(-(BM + p_rows + TOP_K * TM_MID) // SMEM_1D_TILE) * SMEM_1D_TILE
    dflt = (n_assign + jnp.arange(n_dst, dtype=jnp.int32) % BM) * ROW_SUB

    x2, gate, cnt, tbl, xs, rows_dst = _mid(
        an, m, x2d, w_out.astype(BF16), r2(g_out_mla), r2(g_norm_xattn), w_mq.astype(BF16), kmem, vmem,
        w_mo.astype(BF16), r2(g_norm_moe), wr, br, dflt, seq, p_rows)

    ar_e = jnp.arange(N_EXPERTS, dtype=jnp.int32)
    counts = cnt[0, :N_EXPERTS]
    nblk = (counts + BM - 1) // BM
    blk_end = jnp.cumsum(nblk)
    blk_start = blk_end - nblk
    n_used = blk_end[-1:].astype(jnp.int32)
    blk = jnp.arange(nb, dtype=jnp.int32)
    block_expert = jnp.minimum(jnp.sum(blk_end[None, :] <= blk[:, None], axis=1), N_EXPERTS - 1).astype(jnp.int32)
    be_onehot = block_expert[:, None] == ar_e[None, :]
    pick = lambda v: jnp.sum(jnp.where(be_onehot, v[None, :], 0), axis=1)
    nxt_blk = pick(blk_end)
    nxt_onehot = nxt_blk[:, None] == blk[None, :]
    next_expert = jnp.where(nxt_blk < n_used[0],
                            jnp.sum(jnp.where(nxt_onehot, block_expert[None, :], 0), axis=1),
                            -1).astype(jnp.int32)
    seg_id = jnp.cumsum((counts > 0).astype(jnp.int32)) - 1
    seg_par = (pick(seg_id) & 1).astype(jnp.int32)
    ordinal = blk - pick(blk_start)
    tbl_rows = jnp.sum(jnp.where(be_onehot[:, :, None], tbl[None, :N_EXPERTS, :N_EXPERTS], 0), axis=1)
    page = jnp.sum(jnp.where(ordinal[:, None] == ar_e[None, :], tbl_rows, 0), axis=1)
    last_used = jnp.sum(jnp.where(blk == n_used[0] - 1, page, 0))
    page = jnp.where(blk < n_used[0], page, last_used)
    x_blk = jnp.concatenate([page, page[-1:]]).astype(jnp.int32)
    d_blk = jnp.concatenate([jnp.zeros((1,), jnp.int32), page + 1]).astype(jnp.int32)
    fill = counts % BM
    last_page = jnp.sum(jnp.where((nblk - 1)[:, None] == ar_e[None, :], tbl[:N_EXPERTS, :N_EXPERTS], 0), axis=1)
    pad_from = (last_page * BM + fill).astype(jnp.int32)
    pad_n = jnp.where(fill > 0, BM - fill, 0).astype(jnp.int32)

    xs = _padfill(pad_from, pad_n, n_used, xs)
    out2 = _experts(block_expert, next_expert, n_used, seg_par, x_blk, d_blk, rows_dst[:p_rows + BM], xs,
                    w_exp_gate, w_exp_up, w_exp_down, n_assign + BM)
    out = _combine(x2, gate, r2(g_final), out2)
    return out.reshape(batch, seq, d)
```

```python
import math

import jax
import jax.numpy as jnp
import numpy as np
from jax import lax
from jax.experimental import pallas as pl
from jax.experimental.pallas import tpu as pltpu

D_MODEL = 2048
CHUNK = 128
GMLP_HEADS = 8
GMLP_WIDTH = 1024
MLA_HEADS = 8
Q_LORA = 512
KV_LORA = 256
QK_NOPE = 128
QK_ROPE = 64
V_DIM = 128
MLA_WIDTH = MLA_HEADS * V_DIM
ROPE_THETA = 10000.0
MEM_HEADS = 4
MEM_HEAD_DIM = 128
MEM_WIDTH = MEM_HEADS * MEM_HEAD_DIM
N_GROUPS = 8
EXPERTS_PER_GROUP = 8
N_EXPERTS = 64
TOP_K = 2
D_EXPERT = 512
EPS = 1e-6
LN_EPS = 1e-5

LANES = 128
QK_PAD = 256
VMEM_LIMIT = 56 * 1024 * 1024

C_UV = 0
C_Q = 2 * GMLP_WIDTH
C_KV = C_Q + Q_LORA
C_KR = C_KV + KV_LORA
SMEM_1D_TILE = 1024

TM_FRONT = 512
TM_MID = 512
SUB_MID = 256
TQ = 256
ATTN_GROUP = 2
BM = 256
TM_ROW = 256
MID_ISSUE_CHUNKS = 8

F32 = jnp.float32
BF16 = jnp.bfloat16
U32 = jnp.uint32
NEG = float(np.finfo(np.float32).min)


def _cparams(sem):
    return pltpu.CompilerParams(dimension_semantics=sem, vmem_limit_bytes=VMEM_LIMIT)


def _const_spec(shape):
    n = len(shape)
    return pl.BlockSpec(shape, lambda *_: (0,) * n, pipeline_mode=pl.Buffered(1))


def _rms(x, g):
    ms = jnp.mean(x * x, axis=-1, keepdims=True)
    return x * lax.rsqrt(ms + EPS) * g


def _gelu_tanh(x):
    c = math.sqrt(2.0 / math.pi)
    return 0.5 * x * (1.0 + jnp.tanh(c * (x + 0.044715 * (x * x * x))))


def _dot(a, b):
    return jnp.dot(a, b, preferred_element_type=F32)


def _dot_nt(a, b):
    return lax.dot_general(a, b, (((1,), (1,)), ((), ())), preferred_element_type=F32)


def _pack_pair(hi, lo):
    hb = pltpu.bitcast(hi.astype(BF16).astype(F32), U32)
    lb = pltpu.bitcast(lo.astype(BF16).astype(F32), U32)
    return hb | (lb >> 16)


def _unpack_pair(w):
    hi = pltpu.bitcast(w & jnp.uint32(0xFFFF0000), F32)
    lo = pltpu.bitcast(w << 16, F32)
    return hi, lo


ROW_SUB = 8
ROW_WORDS = ROW_SUB * LANES


def _store_row_tiles(ref, packed):
    m = packed.shape[0]
    for j in range(ROW_SUB):
        ref[pl.ds(j, m, stride=ROW_SUB), :] = packed[:, j * LANES:(j + 1) * LANES]


def _load_row_tiles(ref):
    m = ref.shape[0] // ROW_SUB
    return [ref[pl.ds(j, m, stride=ROW_SUB), :] for j in range(ROW_SUB)]


def _front_kernel(x_ref, gmix_ref, win_ref, wkr_ref, gv_ref, bv_ref, ws_ref, bsp_ref, goutg_ref,
                  gq_ref, wuq_ref, gkv_ref, wukv_ref, cs_ref,
                  an_ref, q_ref, k_ref, v_ref):
    tm = x_ref.shape[0]
    xn = _rms(x_ref[...], gmix_ref[...]).astype(BF16)
    z = _dot(xn, win_ref[...])
    zk = _dot(xn, wkr_ref[...])

    u = _gelu_tanh(z[:, :GMLP_WIDTH])
    v = _gelu_tanh(z[:, GMLP_WIDTH:2 * GMLP_WIDTH])
    mu = jnp.mean(v, axis=-1, keepdims=True)
    vc = v - mu
    var = jnp.mean(vc * vc, axis=-1, keepdims=True)
    vn = (vc * lax.rsqrt(var + LN_EPS) * gv_ref[...] + bv_ref[...]).astype(BF16)
    a_chunks = []
    for c in range(tm // CHUNK):
        r0 = c * CHUNK
        cols = []
        for g in range(GMLP_HEADS):
            c0 = g * LANES
            sv = _dot(ws_ref[g], vn[r0:r0 + CHUNK, c0:c0 + LANES])
            cols.append(sv)
        sv_all = jnp.concatenate(cols, axis=1) + bsp_ref[...]
        a_chunks.append(u[r0:r0 + CHUNK, :] * sv_all)
    a = jnp.concatenate(a_chunks, axis=0)
    an_ref[...] = _rms(a, goutg_ref[...]).astype(BF16)

    cs = cs_ref[...]
    scale = 1.0 / math.sqrt(QK_NOPE + QK_ROPE)
    cqn = _rms(z[:, C_Q:C_KV], gq_ref[...]).astype(BF16)
    q = _dot(cqn, wuq_ref[...]) * scale
    ckvn = _rms(z[:, C_KV:C_KR], gkv_ref[...]).astype(BF16)
    kv = _dot(ckvn, wukv_ref[...])
    lane = lax.broadcasted_iota(jnp.int32, (tm, LANES), 1)
    kr = zk * cs
    kr = jnp.where(lane < QK_ROPE, kr + pltpu.roll(kr, QK_ROPE, 1), 0.0).astype(BF16)
    for h in range(MLA_HEADS):
        b0 = h * QK_PAD
        qr = q[:, b0 + LANES:b0 + QK_PAD] * cs
        qr = qr + pltpu.roll(qr, QK_ROPE, 1)
        q_ref[:, b0:b0 + LANES] = q[:, b0:b0 + LANES].astype(BF16)
        q_ref[:, b0 + LANES:b0 + QK_PAD] = qr.astype(BF16)
        k_ref[:, b0:b0 + LANES] = kv[:, b0:b0 + LANES].astype(BF16)
        k_ref[:, b0 + LANES:b0 + QK_PAD] = kr
        v_ref[:, h * V_DIM:(h + 1) * V_DIM] = kv[:, b0 + LANES:b0 + QK_PAD].astype(BF16)


def _front(x2d, gmix, win, wkr, gv, bv, ws, bsp, goutg, gq, wuq_ext, gkv, wukv, cs, seq):
    t = x2d.shape[0]
    tm = TM_FRONT
    nseq = seq // tm
    row = lambda i: (i, 0)
    return pl.pallas_call(
        _front_kernel,
        grid=(t // tm,),
        in_specs=[
            pl.BlockSpec((tm, D_MODEL), row),
            _const_spec((1, D_MODEL)),
            _const_spec((D_MODEL, C_KR)),
            _const_spec((D_MODEL, LANES)),
            _const_spec((1, GMLP_WIDTH)),
            _const_spec((1, GMLP_WIDTH)),
            _const_spec((GMLP_HEADS, CHUNK, CHUNK)),
            _const_spec((CHUNK, GMLP_WIDTH)),
            _const_spec((1, GMLP_WIDTH)),
            _const_spec((1, Q_LORA)),
            _const_spec((Q_LORA, MLA_HEADS * QK_PAD)),
            _const_spec((1, KV_LORA)),
            _const_spec((KV_LORA, MLA_HEADS * QK_PAD)),
            pl.BlockSpec((tm, LANES), lambda i: (i % nseq, 0)),
        ],
        out_specs=[
            pl.BlockSpec((tm, GMLP_WIDTH), row),
            pl.BlockSpec((tm, MLA_HEADS * QK_PAD), row),
            pl.BlockSpec((tm, MLA_HEADS * QK_PAD), row),
            pl.BlockSpec((tm, MLA_WIDTH), row),
        ],
        out_shape=[
            jax.ShapeDtypeStruct((t, GMLP_WIDTH), BF16),
            jax.ShapeDtypeStruct((t, MLA_HEADS * QK_PAD), BF16),
            jax.ShapeDtypeStruct((t, MLA_HEADS * QK_PAD), BF16),
            jax.ShapeDtypeStruct((t, MLA_WIDTH), BF16),
        ],
        compiler_params=_cparams(("arbitrary",)),
        name="front",
    )(x2d, gmix, win, wkr, gv, bv, ws, bsp, goutg, gq, wuq_ext, gkv, wukv, cs)


def _attn_kernel(q_ref, k_ref, v_ref, o_ref):
    s_len = q_ref.shape[0]
    row = lax.broadcasted_iota(jnp.int32, (TQ, TQ), 0)
    col = lax.broadcasted_iota(jnp.int32, (TQ, TQ), 1)
    causal = col <= row
    n_blk = s_len // TQ
    for g0 in range(0, n_blk, ATTN_GROUP):
        blocks = list(range(g0, min(g0 + ATTN_GROUP, n_blk)))
        qb = {qi: q_ref[qi * TQ:(qi + 1) * TQ, :] for qi in blocks}
        sd = {qi: jnp.where(causal, _dot_nt(qb[qi], k_ref[qi * TQ:(qi + 1) * TQ, :]), NEG)
              for qi in blocks}
        so = {qi: _dot_nt(qb[qi], k_ref[0:qi * TQ, :]) for qi in blocks if qi > 0}
        m = {qi: jnp.max(sd[qi], axis=-1, keepdims=True) for qi in blocks}
        for qi in so:
            m[qi] = jnp.maximum(m[qi], jnp.max(so[qi], axis=-1, keepdims=True))
        pd = {qi: jnp.exp(sd[qi] - m[qi]) for qi in blocks}
        po = {qi: jnp.exp(so[qi] - m[qi]) for qi in so}
        l = {qi: jnp.sum(pd[qi], axis=-1, keepdims=True) for qi in blocks}
        acc = {qi: _dot(pd[qi].astype(BF16), v_ref[qi * TQ:(qi + 1) * TQ, :]) for qi in blocks}
        for qi in so:
            l[qi] = l[qi] + jnp.sum(po[qi], axis=-1, keepdims=True)
            acc[qi] = acc[qi] + _dot(po[qi].astype(BF16), v_ref[0:qi * TQ, :])
        for qi in blocks:
            o_ref[qi * TQ:(qi + 1) * TQ, :] = (acc[qi] / l[qi]).astype(o_ref.dtype)


def _mla_attn(q, k, v, batch, seq):
    q3 = q.reshape(batch, seq, MLA_HEADS * QK_PAD)
    k3 = k.reshape(batch, seq, MLA_HEADS * QK_PAD)
    v3 = v.reshape(batch, seq, MLA_WIDTH)
    hmap = lambda b, h: (b, 0, h)
    out = pl.pallas_call(
        _attn_kernel,
        grid=(batch, MLA_HEADS),
        in_specs=[
            pl.BlockSpec((None, seq, QK_PAD), hmap),
            pl.BlockSpec((None, seq, QK_PAD), hmap),
            pl.BlockSpec((None, seq, V_DIM), hmap),
        ],
        out_specs=pl.BlockSpec((None, seq, V_DIM), hmap),
        out_shape=jax.ShapeDtypeStruct((batch, seq, MLA_WIDTH), BF16),
        compiler_params=_cparams(("arbitrary", "arbitrary")),
        name="mla_attn",
    )(q3, k3, v3)
    return out.reshape(batch * seq, MLA_WIDTH)


def _memkv_kernel(mem_ref, g_ref, wk_ref, wv_ref, k_ref, v_ref):
    mn = _rms(mem_ref[...], g_ref[...]).astype(BF16)
    k_ref[...] = _dot(mn, wk_ref[...]).astype(BF16)
    v_ref[...] = _dot(mn, wv_ref[...]).astype(BF16)


def _mem_kv(mem, g, wk, wv):
    b, m, d = mem.shape
    bmap = lambda i: (i, 0, 0)
    return pl.pallas_call(
        _memkv_kernel,
        grid=(b,),
        in_specs=[
            pl.BlockSpec((None, m, d), bmap),
            _const_spec((1, d)),
            _const_spec((d, MEM_WIDTH)),
            _const_spec((d, MEM_WIDTH)),
        ],
        out_specs=[pl.BlockSpec((None, m, MEM_WIDTH), bmap)] * 2,
        out_shape=[jax.ShapeDtypeStruct((b, m, MEM_WIDTH), BF16)] * 2,
        compiler_params=_cparams(("arbitrary",)),
        name="mem_kv",
    )(mem, g, wk, wv)


def _row_to_col(x_row):
    r = lax.broadcasted_iota(jnp.int32, (LANES, LANES), 0)
    c = lax.broadcasted_iota(jnp.int32, (LANES, LANES), 1)
    return jnp.sum(jnp.where(r == c, jnp.broadcast_to(x_row, (LANES, LANES)), 0.0), axis=1, keepdims=True)


def _route(lg, state):
    tm = lg.shape[0]
    lane_i = lax.broadcasted_iota(jnp.int32, (tm, LANES), 1)
    lane = lane_i.astype(F32)
    big = float(LANES)
    gmask = lane_i < N_GROUPS
    gl = jnp.where(gmask, lg, NEG)
    gmax = jnp.max(gl, axis=-1, keepdims=True)
    grp = jnp.min(jnp.where(gl == gmax, lane, big), axis=-1, keepdims=True)
    gsum = jnp.sum(jnp.where(gmask, jnp.exp(gl - gmax), 0.0), axis=-1, keepdims=True)
    p_grp = 1.0 / gsum

    lo = (grp + 1.0) * EXPERTS_PER_GROUP
    emask = jnp.logical_and(lane >= lo, lane < lo + EXPERTS_PER_GROUP)
    el = jnp.where(emask, lg, NEG)
    emax = jnp.max(el, axis=-1, keepdims=True)
    ee = jnp.where(emask, jnp.exp(el - emax), 0.0)
    ep = ee / jnp.sum(ee, axis=-1, keepdims=True)
    ep = jnp.where(emask, ep, -1.0)
    p1 = jnp.max(ep, axis=-1, keepdims=True)
    i1 = jnp.min(jnp.where(ep == p1, lane, big), axis=-1, keepdims=True)
    ep2 = jnp.where(lane == i1, -1.0, ep)
    p2 = jnp.max(ep2, axis=-1, keepdims=True)
    i2 = jnp.min(jnp.where(ep2 == p2, lane, big), axis=-1, keepdims=True)
    den = p1 + p2
    g1 = p_grp * (p1 / den)
    g2 = p_grp * (p2 / den)
    e1 = i1 - N_GROUPS
    e2 = i2 - N_GROUPS
    gate = jnp.where(lane_i == 0, g1, jnp.where(lane_i == 1, g2, 0.0))

    cnt, page, npg, tbl = state
    oh1 = (lane == e1).astype(F32)
    oh2 = (lane == e2).astype(F32)
    both = oh1 + oh2
    r_i = lax.broadcasted_iota(jnp.int32, (tm, tm), 0)
    c_i = lax.broadcasted_iota(jnp.int32, (tm, tm), 1)
    before = (c_i < r_i).astype(BF16)
    prefix = _dot(before, both.astype(BF16))
    tc = jnp.sum(both, axis=0, keepdims=True)
    fill = cnt - BM * jnp.floor(cnt * (1.0 / BM))
    need = jnp.logical_and(tc > 0, jnp.logical_or(fill == 0, fill + tc > BM))
    need_f = need.astype(F32)
    e_r = lax.broadcasted_iota(jnp.int32, (LANES, LANES), 0)
    e_c = lax.broadcasted_iota(jnp.int32, (LANES, LANES), 1)
    lower_e = (e_r < e_c).astype(BF16)
    excl = _dot(jnp.broadcast_to(need_f, (ROW_SUB, LANES)).astype(BF16), lower_e)[0:1, :]
    newpage = npg + excl
    pos = fill + prefix
    in_cur = jnp.logical_and(fill > 0, pos < BM)
    slotval = (jnp.where(in_cur, page, newpage) * BM + jnp.where(pos >= BM, pos - BM, pos))
    s1 = jnp.sum(oh1 * slotval, axis=-1, keepdims=True)
    s2 = jnp.sum(oh2 * slotval, axis=-1, keepdims=True)
    slots = jnp.where(lane_i == 0, s1, jnp.where(lane_i == 1, s2, 0.0))

    ordinal = jnp.floor((cnt + (BM - 1)) * (1.0 / BM))
    hit = jnp.logical_and(_row_to_col(need_f) > 0, e_c.astype(F32) == _row_to_col(ordinal))
    tbl = jnp.where(hit, _row_to_col(newpage), tbl)
    new_state = (cnt + tc, jnp.where(need, newpage, page),
                 npg + jnp.sum(need_f, axis=-1, keepdims=True), tbl)
    return slots, gate, new_state


def _mid_kernel(an_ref, m_ref, x_ref, wout_ref, goutm_ref, gxa_ref, wmq_ref, kmem_ref, vmem_ref,
                wmo_ref, gmoe_ref, wr_ref, br_ref, dflt_ref,
                x2_ref, gate_ref, cnt_ref, tbl_ref, xs_ref, rdst_ref,
                state_ref, tbl_s, hp_s, slot_v, slot_s, fin_v, fin_s, rsem, ssem, isem, zsem):
    i = pl.program_id(0)
    n_steps = pl.num_programs(0)
    tm = x_ref.shape[0]
    n_tok = n_steps * tm
    par = i % 2

    def rows_wait(p):
        for k in range(TOP_K):
            pltpu.make_async_copy(hp_s.at[p], xs_ref.at[pl.ds(0, tm * ROW_SUB), :], rsem.at[p]).wait()

    def slot_copy(p):
        return pltpu.make_async_copy(slot_v, slot_s.at[p], ssem.at[p])

    def issue_row(p, step, t):
        for k in range(TOP_K):
            slot = slot_s[p, k, t]
            pltpu.make_async_copy(
                hp_s.at[p, pl.ds(t * ROW_SUB, ROW_SUB), :],
                xs_ref.at[pl.ds(pl.multiple_of(slot * ROW_SUB, ROW_SUB), ROW_SUB), :],
                rsem.at[p]).start()
            rdst_ref[BM + slot] = (k * n_tok + step * tm + t) * ROW_SUB

    @pl.when(i == 0)
    def _():
        state_ref[...] = jnp.zeros(state_ref.shape, state_ref.dtype)
        tbl_s[...] = jnp.zeros(tbl_s.shape, tbl_s.dtype)
        preset = pltpu.make_async_copy(dflt_ref, rdst_ref, isem)
        preset.start()
        hp_s[1] = jnp.zeros(hp_s.shape[1:], hp_s.dtype)
        spare0 = xs_ref.shape[0] // ROW_SUB - TOP_K * tm

        def prime(t, c):
            for k in range(TOP_K):
                slot_s[1, k, t] = spare0 + k * tm + t
            return c

        lax.fori_loop(0, tm, prime, 0)
        preset.wait()

    @pl.when(i >= 1)
    def _():
        rows_wait(par)
        slot_copy(1 - par).wait()

    chunk = tm // MID_ISSUE_CHUNKS
    pending = iter(range(MID_ISSUE_CHUNKS))

    def issue_chunks(n):
        for _ in range(n):
            c = next(pending, None)
            if c is not None:
                for t in range(c * chunk, (c + 1) * chunk):
                    issue_row(1 - par, i - 1, t)

    subs = [pl.ds(r0, SUB_MID) for r0 in range(0, x_ref.shape[0], SUB_MID)]
    issue_chunks(1)
    mn = [_rms(m_ref[r, :].astype(F32), goutm_ref[...]).astype(BF16) for r in subs]
    x1 = []
    for r, a in zip(subs, mn):
        issue_chunks(1)
        d_a = _dot(an_ref[r, :], wout_ref[:GMLP_WIDTH, :])
        issue_chunks(1)
        x1.append(x_ref[r, :] + d_a + _dot(a, wout_ref[GMLP_WIDTH:, :]))
    issue_chunks(1)
    h2 = [_rms(a, gxa_ref[...]).astype(BF16) for a in x1]
    q2 = [(_dot(a, wmq_ref[...]) * (1.0 / math.sqrt(MEM_HEAD_DIM))).astype(BF16) for a in h2]
    issue_chunks(1)

    def mem_attention(q):
        outs = []
        for h in range(MEM_HEADS):
            c0 = h * MEM_HEAD_DIM
            s = _dot_nt(q[:, c0:c0 + MEM_HEAD_DIM], kmem_ref[:, c0:c0 + MEM_HEAD_DIM])
            m = jnp.max(s, axis=-1, keepdims=True)
            p = jnp.exp(s - m)
            l = jnp.sum(p, axis=-1, keepdims=True)
            outs.append(_dot(p.astype(BF16), vmem_ref[:, c0:c0 + MEM_HEAD_DIM]) / l)
        return jnp.concatenate(outs, axis=1).astype(BF16)

    o = [mem_attention(q) for q in q2]
    issue_chunks(1)
    x2 = [a + _dot(b, wmo_ref[...]) for a, b in zip(x1, o)]
    issue_chunks(MID_ISSUE_CHUNKS)
    for r, a in zip(subs, x2):
        x2_ref[r, :] = a
    h3 = [_rms(a, gmoe_ref[...]) for a in x2]
    half = D_MODEL // 2
    for n, a in enumerate(h3):
        _store_row_tiles(hp_s.at[par, pl.ds(n * SUB_MID * ROW_SUB, SUB_MID * ROW_SUB), :],
                         _pack_pair(a[:, :half], a[:, half:]))
    lg = [_dot(a.astype(BF16), wr_ref[...]) + br_ref[...] for a in h3]
    state = (state_ref[0:1, :], state_ref[1:2, :], state_ref[2:3, :], tbl_s[...])
    for n, (r, a) in enumerate(zip(subs, lg)):
        slots, gate, state = _route(a, state)
        gate_ref[r, :] = gate
        slot_v[:, n * SUB_MID:(n + 1) * SUB_MID] = slots.T[0:ROW_SUB, :].astype(jnp.int32)
    slot_copy(par).start()
    state_ref[0:1, :] = state[0]
    state_ref[1:2, :] = state[1]
    state_ref[2:3, :] = state[2]
    tbl_s[...] = state[3]
    cnt_ref[...] = jnp.broadcast_to(state[0], cnt_ref.shape).astype(jnp.int32)
    tbl_ref[...] = state[3].astype(jnp.int32)

    @pl.when(i == n_steps - 1)
    def _():
        slot_copy(par).wait()

        def last_rows(t, c):
            issue_row(par, i, t)
            return c

        lax.fori_loop(0, tm, last_rows, 0)
        rows_wait(1 - par)
        rows_wait(par)

        cnt_f, page_f, npg_f = state[0], state[1], state[2]
        fill_f = cnt_f - BM * jnp.floor(cnt_f * (1.0 / BM))
        fin_v[...] = jnp.concatenate(
            [page_f * BM + fill_f, jnp.where(fill_f > 0, BM - fill_f, 0.0), npg_f,
             jnp.zeros((ROW_SUB - 3, LANES), F32)], axis=0).astype(jnp.int32)
        fin_cp = pltpu.make_async_copy(fin_v, fin_s, isem)
        fin_cp.start()
        zero_rows = hp_s.at[0, pl.ds(0, BM * ROW_SUB), :]
        zero_rows[...] = jnp.zeros((BM * ROW_SUB, LANES), hp_s.dtype)
        fin_cp.wait()
        _zero_page_tails(fin_s, zero_rows, xs_ref, zsem)


def _mid(an, m, x2d, wout, goutm, gxa, wmq, kmem, vmem, wmo, gmoe, wr, br, dflt, seq, p_rows):
    t = x2d.shape[0]
    tm = TM_MID
    xs_rows = p_rows + TOP_K * tm
    nseq = seq // tm
    mem_len = kmem.shape[1]
    row = lambda i: (i, 0)
    bmap = lambda i: (i // nseq, 0, 0)
    return pl.pallas_call(
        _mid_kernel,
        grid=(t // tm,),
        in_specs=[
            pl.BlockSpec((tm, GMLP_WIDTH), row),
            pl.BlockSpec((tm, MLA_WIDTH), row),
            pl.BlockSpec((tm, D_MODEL), row),
            _const_spec((D_MODEL, D_MODEL)),
            _const_spec((1, MLA_WIDTH)),
            _const_spec((1, D_MODEL)),
            _const_spec((D_MODEL, MEM_WIDTH)),
            pl.BlockSpec((None, mem_len, MEM_WIDTH), bmap),
            pl.BlockSpec((None, mem_len, MEM_WIDTH), bmap),
            _const_spec((MEM_WIDTH, D_MODEL)),
            _const_spec((1, D_MODEL)),
            _const_spec((D_MODEL, LANES)),
            _const_spec((1, LANES)),
            pl.BlockSpec(memory_space=pl.ANY),
        ],
        out_specs=[
            pl.BlockSpec((tm, D_MODEL), row),
            pl.BlockSpec((tm, LANES), row),
            pl.BlockSpec((ROW_SUB, LANES), lambda i: (0, 0)),
            pl.BlockSpec((LANES, LANES), lambda i: (0, 0)),
            pl.BlockSpec(memory_space=pl.ANY),
            pl.BlockSpec(memory_space=pltpu.SMEM),
        ],
        out_shape=[
            jax.ShapeDtypeStruct((t, D_MODEL), F32),
            jax.ShapeDtypeStruct((t, LANES), F32),
            jax.ShapeDtypeStruct((ROW_SUB, LANES), jnp.int32),
            jax.ShapeDtypeStruct((LANES, LANES), jnp.int32),
            jax.ShapeDtypeStruct((xs_rows * ROW_SUB, LANES), U32),
            jax.ShapeDtypeStruct(dflt.shape, jnp.int32),
        ],
        scratch_shapes=[
            pltpu.VMEM((ROW_SUB, LANES), F32),
            pltpu.VMEM((LANES, LANES), F32),
            pltpu.VMEM((2, tm * ROW_SUB, LANES), U32),
            pltpu.VMEM((ROW_SUB, tm), jnp.int32),
            pltpu.SMEM((2, ROW_SUB, tm), jnp.int32),
            pltpu.VMEM((ROW_SUB, LANES), jnp.int32),
            pltpu.SMEM((ROW_SUB, LANES), jnp.int32),
            pltpu.SemaphoreType.DMA((2,)),
            pltpu.SemaphoreType.DMA((2,)),
            pltpu.SemaphoreType.DMA(()),
            pltpu.SemaphoreType.DMA(()),
        ],
        compiler_params=_cparams(("arbitrary",)),
        name="mid",
    )(an, m, x2d, wout, goutm, gxa, wmq, kmem, vmem, wmo, gmoe, wr, br, dflt)


PAD_BITS = tuple(1 << b for b in reversed(range(BM.bit_length() - 1)))


def _zero_page_tails(fin_s, zero_buf, xs_ref, zsem):
    def pad_copy(e, bit):
        n = fin_s[1, e]
        off = fin_s[0, e] + (n & ~(2 * bit - 1))
        return n & bit, pltpu.make_async_copy(
            zero_buf.at[pl.ds(0, bit * ROW_SUB), :],
            xs_ref.at[pl.ds(pl.multiple_of(off * ROW_SUB, ROW_SUB), bit * ROW_SUB), :], zsem)

    def fill(e, c):
        for bit in PAD_BITS:
            on, cp = pad_copy(e, bit)

            @pl.when(on != 0)
            def _():
                cp.start()
        return c

    def fill_wait(e, c):
        for bit in PAD_BITS:
            on, cp = pad_copy(e, bit)

            @pl.when(on != 0)
            def _():
                cp.wait()
        return c

    def tail_copy(b):
        return pltpu.make_async_copy(
            zero_buf, xs_ref.at[pl.ds(pl.multiple_of(b * (BM * ROW_SUB), BM * ROW_SUB), BM * ROW_SUB), :],
            zsem)

    def tail(b, c):
        tail_copy(b).start()
        return c

    def tail_wait(b, c):
        tail_copy(b).wait()
        return c

    n_pages = (xs_ref.shape[0] // ROW_SUB - TOP_K * TM_MID) // BM
    lax.fori_loop(0, N_EXPERTS, fill, 0)
    lax.fori_loop(fin_s[2, 0], n_pages, tail, 0)
    lax.fori_loop(0, N_EXPERTS, fill_wait, 0)
    lax.fori_loop(fin_s[2, 0], n_pages, tail_wait, 0)


def _row_scatter_start(src_vmem, idx_ref, dst_hbm, sem):
    for r in range(BM):
        d0 = pl.multiple_of(idx_ref[0, 0, r], ROW_SUB)
        pltpu.make_async_copy(src_vmem.at[pl.ds(r * ROW_SUB, ROW_SUB), :],
                              dst_hbm.at[pl.ds(d0, ROW_SUB), :], sem).start()


def _scatter_wait(src_vmem, dst_hbm, sem):
    pltpu.make_async_copy(src_vmem, dst_hbm.at[pl.ds(0, BM * ROW_SUB), :], sem).wait()


def _cast_pair_rows(dst_s, src_st):
    half = D_MODEL // 2
    for j in range(ROW_SUB):
        a = j * LANES
        dst_s[2 * a:2 * a + LANES, :] = src_st[a:a + LANES, :].astype(BF16)
        dst_s[2 * a + LANES:2 * a + 2 * LANES, :] = src_st[half + a:half + a + LANES, :].astype(BF16)


def _expert_kernel(be_ref, nxt_ref, nused_ref, segpar_ref, xblk_ref, dblk_ref,
                   dstp_ref, xs_ref, wg_ref, wu_ref, wd_ref,
                   out_ref,
                   wg_st, wu_st, wd_st, wg_s, wu_s, wd_s, ybuf0, ybuf1,
                   wsem, ssem):
    i = pl.program_id(0)
    nused = nused_ref[0]
    used = i < nused
    par = i % 2
    ic = jnp.minimum(i, be_ref.shape[0] - 1)
    e = be_ref[ic]
    first = jnp.logical_or(i == 0, e != be_ref[jnp.maximum(i - 1, 0)])

    def weight_copies(ex, slot):
        return (pltpu.make_async_copy(wg_ref.at[ex], wg_st.at[slot], wsem.at[slot, 0]),
                pltpu.make_async_copy(wu_ref.at[ex], wu_st.at[slot], wsem.at[slot, 1]),
                pltpu.make_async_copy(wd_ref.at[ex], wd_st.at[slot], wsem.at[slot, 2]))

    @pl.when(i == 0)
    def _():
        for cp in weight_copies(e, 0):
            cp.start()
        ybuf1[...] = jnp.zeros(ybuf1.shape, ybuf1.dtype)

    def load_weights(slot):
        nxt = nxt_ref[ic]

        @pl.when(nxt >= 0)
        def _():
            for cp in weight_copies(nxt, 1 - slot):
                cp.start()

        for cp in weight_copies(0, slot):
            cp.wait()
        _cast_pair_rows(wg_s, wg_st.at[slot])
        _cast_pair_rows(wu_s, wu_st.at[slot])
        wd_s[...] = wd_st[slot].astype(BF16)

    for slot in range(2):
        @pl.when(jnp.logical_and(jnp.logical_and(first, used), segpar_ref[ic] == slot))
        def _():
            load_weights(slot)

    def step(yb, yo, s):
        o = 1 - s

        @pl.when(i >= 1)
        def _():
            _scatter_wait(yb, out_ref, ssem.at[s])

        _row_scatter_start(yo, dstp_ref, out_ref, ssem.at[o])
        half = D_MODEL // 2
        cols = []
        for w in _load_row_tiles(xs_ref):
            hi, lo = _unpack_pair(w)
            cols += [hi.astype(BF16), lo.astype(BF16)]
        xrow = jnp.concatenate(cols, axis=1)
        g = _dot(xrow, wg_s[...])
        u = _dot(xrow, wu_s[...])
        hm = (g * jax.nn.sigmoid(g) * u).astype(BF16)
        y = _dot(hm, wd_s[...])
        _store_row_tiles(yb, _pack_pair(y[:, :half], y[:, half:]))

    @pl.when(jnp.logical_and(used, par == 0))
    def _():
        step(ybuf0, ybuf1, 0)

    @pl.when(jnp.logical_and(used, par == 1))
    def _():
        step(ybuf1, ybuf0, 1)

    def drain(yb, yo, s):
        o = 1 - s
        _scatter_wait(yb, out_ref, ssem.at[s])
        _row_scatter_start(yo, dstp_ref, out_ref, ssem.at[o])
        _scatter_wait(yo, out_ref, ssem.at[o])

    @pl.when(jnp.logical_and(i == nused, par == 0))
    def _():
        drain(ybuf0, ybuf1, 0)

    @pl.when(jnp.logical_and(i == nused, par == 1))
    def _():
        drain(ybuf1, ybuf0, 1)


def _experts(block_expert, next_expert, n_used, seg_par, x_blk, d_blk, rows_dst, xs, wg, wu, wd, n_out_rows):
    nb = block_expert.shape[0]
    dst3 = rows_dst.reshape(nb + 1, 1, BM)
    grid_spec = pltpu.PrefetchScalarGridSpec(
        num_scalar_prefetch=6,
        grid=(nb + 1,),
        in_specs=[
            pl.BlockSpec((1, 1, BM), lambda i, be, nx, nu, sp, xb, db: (db[i], 0, 0), memory_space=pltpu.SMEM),
            pl.BlockSpec((BM * ROW_SUB, LANES), lambda i, be, nx, nu, sp, xb, db: (xb[i], 0)),
            pl.BlockSpec(memory_space=pl.ANY),
            pl.BlockSpec(memory_space=pl.ANY),
            pl.BlockSpec(memory_space=pl.ANY),
        ],
        out_specs=pl.BlockSpec(memory_space=pl.ANY),
        scratch_shapes=[
            pltpu.VMEM((2, D_MODEL, D_EXPERT), F32),
            pltpu.VMEM((2, D_MODEL, D_EXPERT), F32),
            pltpu.VMEM((2, D_EXPERT, D_MODEL), F32),
            pltpu.VMEM((D_MODEL, D_EXPERT), BF16),
            pltpu.VMEM((D_MODEL, D_EXPERT), BF16),
            pltpu.VMEM((D_EXPERT, D_MODEL), BF16),
            pltpu.VMEM((BM * ROW_SUB, LANES), U32),
            pltpu.VMEM((BM * ROW_SUB, LANES), U32),
            pltpu.SemaphoreType.DMA((2, 3)),
            pltpu.SemaphoreType.DMA((2,)),
        ],
    )
    return pl.pallas_call(
        _expert_kernel,
        grid_spec=grid_spec,
        out_shape=jax.ShapeDtypeStruct((n_out_rows * ROW_SUB, LANES), U32),
        compiler_params=_cparams(("arbitrary",)),
        name="experts",
    )(block_expert, next_expert, n_used, seg_par, x_blk, d_blk, dst3, xs, wg, wu, wd)


def _combine_kernel(x2_ref, gate_ref, gfin_ref, y0_ref, y1_ref, o_ref):
    gate = gate_ref[...]
    half = D_MODEL // 2
    y_hi = x2_ref[:, :half]
    y_lo = x2_ref[:, half:]
    for k, y_ref in enumerate((y0_ref, y1_ref)):
        pairs = [_unpack_pair(w) for w in _load_row_tiles(y_ref)]
        hi = jnp.concatenate([p[0] for p in pairs], axis=1)
        lo = jnp.concatenate([p[1] for p in pairs], axis=1)
        gk = gate[:, k:k + 1]
        y_hi = y_hi + gk * hi
        y_lo = y_lo + gk * lo
    ms = (jnp.sum(y_hi * y_hi, axis=-1, keepdims=True)
          + jnp.sum(y_lo * y_lo, axis=-1, keepdims=True)) * (1.0 / D_MODEL)
    r = lax.rsqrt(ms + EPS)
    o_ref[:, :half] = y_hi * r * gfin_ref[:, :half]
    o_ref[:, half:] = y_lo * r * gfin_ref[:, half:]


def _combine(x2, gate, gfin, out2):
    t, d = x2.shape
    tm = TM_ROW
    nb = t // tm
    row = lambda i: (i, 0)
    return pl.pallas_call(
        _combine_kernel,
        grid=(nb,),
        in_specs=[
            pl.BlockSpec((tm, d), row),
            pl.BlockSpec((tm, LANES), row),
            _const_spec((1, d)),
            pl.BlockSpec((tm * ROW_SUB, LANES), row),
            pl.BlockSpec((tm * ROW_SUB, LANES), lambda i: (nb + i, 0)),
        ],
        out_specs=pl.BlockSpec((tm, d), row),
        out_shape=jax.ShapeDtypeStruct((t, d), F32),
        compiler_params=_cparams(("arbitrary",)),
        name="combine",
    )(x2, gate, gfin, out2, out2)


def _rot_half_cols(w):
    half = QK_ROPE // 2
    return jnp.concatenate([-w[..., half:], w[..., :half]], axis=-1)


def kernel(x, mem, g_norm_mix, w_in, g_v, b_v, w_spatial, b_spatial, g_q_lora, w_uq, g_kv_lora, w_ukv, g_out_gmlp, g_out_mla, w_out, g_norm_xattn, g_norm_mem, w_mq, w_mk, w_mv, w_mo, g_norm_moe, w_router_group, b_router_group, w_router_expert, b_router_expert, w_exp_gate, w_exp_up, w_exp_down, g_final):
    batch, seq, d = x.shape
    t = batch * seq
    x2d = x.reshape(t, d)
    r2 = lambda a: a.reshape(1, -1)

    w_kr = w_in[:, C_KR:C_KR + QK_ROPE]
    win = w_in[:, :C_KR].astype(BF16)
    wkr = jnp.concatenate([w_kr, _rot_half_cols(w_kr)], axis=1).astype(BF16)
    wq3 = w_uq.reshape(Q_LORA, MLA_HEADS, QK_NOPE + QK_ROPE)
    wq_rope = wq3[..., QK_NOPE:]
    wuq_ext = jnp.concatenate([wq3, _rot_half_cols(wq_rope)], axis=-1)
    wuq_ext = wuq_ext.reshape(Q_LORA, MLA_HEADS * QK_PAD).astype(BF16)
    wukv = w_ukv.astype(BF16)
    causal = jnp.tril(jnp.ones((CHUNK, CHUNK), dtype=bool))
    ws = jnp.where(causal[None], w_spatial, 0.0).astype(BF16)
    bsp = jnp.repeat(b_spatial.T, LANES, axis=1)
    wr = jnp.concatenate(
        [w_router_group, w_router_expert,
         jnp.zeros((d, LANES - N_GROUPS - N_EXPERTS), F32)], axis=1).astype(BF16)
    br = jnp.concatenate(
        [b_router_group, b_router_expert, jnp.zeros((LANES - N_GROUPS - N_EXPERTS,), F32)]).reshape(1, LANES)

    pos = jnp.arange(seq, dtype=F32)
    inv_freq = ROPE_THETA ** (-jnp.arange(0, QK_ROPE, 2, dtype=F32) / QK_ROPE)
    ang = pos[:, None] * inv_freq[None, :]
    cos, sin = jnp.cos(ang), jnp.sin(ang)
    cs = jnp.concatenate([cos, cos, sin, sin], axis=1)

    an, q, k, v = _front(x2d, r2(g_norm_mix), win, wkr, r2(g_v), r2(b_v), ws, bsp, r2(g_out_gmlp),
                         r2(g_q_lora), wuq_ext, r2(g_kv_lora), wukv, cs, seq)
    m = _mla_attn(q, k, v, batch, seq)
    kmem, vmem = _mem_kv(mem, r2(g_norm_mem), w_mk.astype(BF16), w_mv.astype(BF16))
    n_assign = t * TOP_K
    p_rows = n_assign + N_EXPERTS * BM
    nb = p_rows // BM
    n_dst = -(-(BM + p_rows + TOP_K * TM_MID) // SMEM_1D_TILE) * SMEM_1D_TILE
    dflt = (n_assign + jnp.arange(n_dst, dtype=jnp.int32) % BM) * ROW_SUB

    x2, gate, cnt, tbl, xs, rows_dst = _mid(
        an, m, x2d, w_out.astype(BF16), r2(g_out_mla), r2(g_norm_xattn), w_mq.astype(BF16), kmem, vmem,
        w_mo.astype(BF16), r2(g_norm_moe), wr, br, dflt, seq, p_rows)

    ar_e = jnp.arange(N_EXPERTS, dtype=jnp.int32)
    counts = cnt[0, :N_EXPERTS]
    nblk = (counts + BM - 1) // BM
    blk_end = jnp.cumsum(nblk)
    blk_start = blk_end - nblk
    n_used = blk_end[-1:].astype(jnp.int32)
    blk = jnp.arange(nb, dtype=jnp.int32)
    block_expert = jnp.minimum(jnp.sum(blk_end[None, :] <= blk[:, None], axis=1), N_EXPERTS - 1).astype(jnp.int32)
    be_onehot = block_expert[:, None] == ar_e[None, :]
    pick = lambda v: jnp.sum(jnp.where(be_onehot, v[None, :], 0), axis=1)
    nxt_blk = pick(blk_end)
    nxt_onehot = nxt_blk[:, None] == blk[None, :]
    next_expert = jnp.where(nxt_blk < n_used[0],
                            jnp.sum(jnp.where(nxt_onehot, block_expert[None, :], 0), axis=1),
                            -1).astype(jnp.int32)
    seg_id = jnp.cumsum((counts > 0).astype(jnp.int32)) - 1
    seg_par = (pick(seg_id) & 1).astype(jnp.int32)
    ordinal = blk - pick(blk_start)
    tbl_rows = jnp.sum(jnp.where(be_onehot[:, :, None], tbl[None, :N_EXPERTS, :N_EXPERTS], 0), axis=1)
    page = jnp.sum(jnp.where(ordinal[:, None] == ar_e[None, :], tbl_rows, 0), axis=1)
    last_used = jnp.sum(jnp.where(blk == n_used[0] - 1, page, 0))
    page = jnp.where(blk < n_used[0], page, last_used)
    x_blk = jnp.concatenate([page, page[-1:]]).astype(jnp.int32)
    d_blk = jnp.concatenate([jnp.zeros((1,), jnp.int32), page + 1]).astype(jnp.int32)

    out2 = _experts(block_expert, next_expert, n_used, seg_par, x_blk, d_blk, rows_dst[:p_rows + BM], xs,
                    w_exp_gate, w_exp_up, w_exp_down, n_assign + BM)
    out = _combine(x2, gate, r2(g_final), out2)
    return out.reshape(batch, seq, d)
```

```python
import math

import jax
import jax.numpy as jnp
import numpy as np
from jax import lax
from jax.experimental import pallas as pl
from jax.experimental.pallas import tpu as pltpu

D_MODEL = 2048
CHUNK = 128
GMLP_HEADS = 8
GMLP_WIDTH = 1024
MLA_HEADS = 8
Q_LORA = 512
KV_LORA = 256
QK_NOPE = 128
QK_ROPE = 64
V_DIM = 128
MLA_WIDTH = MLA_HEADS * V_DIM
ROPE_THETA = 10000.0
MEM_HEADS = 4
MEM_HEAD_DIM = 128
MEM_WIDTH = MEM_HEADS * MEM_HEAD_DIM
N_GROUPS = 8
EXPERTS_PER_GROUP = 8
N_EXPERTS = 64
TOP_K = 2
D_EXPERT = 512
EPS = 1e-6
LN_EPS = 1e-5

LANES = 128
QK_PAD = 256
VMEM_LIMIT = 56 * 1024 * 1024
VMEM_LIMIT_EXPERTS = 60 * 1024 * 1024
W_LOOKAHEAD = 2

C_UV = 0
C_Q = 2 * GMLP_WIDTH
C_KV = C_Q + Q_LORA
C_KR = C_KV + KV_LORA
SMEM_1D_TILE = 1024

TM_FRONT = 512
TM_MID = 512
SUB_MID = 256
TQ = 256
ATTN_GROUP = 2
BM = 256
TM_ROW = 256

F32 = jnp.float32
BF16 = jnp.bfloat16
U32 = jnp.uint32
NEG = float(np.finfo(np.float32).min)


def _cparams(sem):
    return pltpu.CompilerParams(dimension_semantics=sem, vmem_limit_bytes=VMEM_LIMIT)


def _const_spec(shape):
    n = len(shape)
    return pl.BlockSpec(shape, lambda *_: (0,) * n, pipeline_mode=pl.Buffered(1))


def _rms(x, g):
    ms = jnp.mean(x * x, axis=-1, keepdims=True)
    return x * lax.rsqrt(ms + EPS) * g


def _gelu_tanh(x):
    c = math.sqrt(2.0 / math.pi)
    return 0.5 * x * (1.0 + jnp.tanh(c * (x + 0.044715 * (x * x * x))))


def _dot(a, b):
    return jnp.dot(a, b, preferred_element_type=F32)


def _dot_nt(a, b):
    return lax.dot_general(a, b, (((1,), (1,)), ((), ())), preferred_element_type=F32)


def _pack_pair(hi, lo):
    hb = pltpu.bitcast(hi.astype(BF16).astype(F32), U32)
    lb = pltpu.bitcast(lo.astype(BF16).astype(F32), U32)
    return hb | (lb >> 16)


def _unpack_pair(w):
    hi = pltpu.bitcast(w & jnp.uint32(0xFFFF0000), F32)
    lo = pltpu.bitcast(w << 16, F32)
    return hi, lo


ROW_SUB = 8
ROW_WORDS = ROW_SUB * LANES


def _store_row_tiles(ref, packed):
    m = packed.shape[0]
    for j in range(ROW_SUB):
        ref[pl.ds(j, m, stride=ROW_SUB), :] = packed[:, j * LANES:(j + 1) * LANES]


def _load_row_tiles(ref):
    m = ref.shape[0] // ROW_SUB
    return [ref[pl.ds(j, m, stride=ROW_SUB), :] for j in range(ROW_SUB)]


def _front_kernel(x_ref, gmix_ref, win_ref, wkr_ref, gv_ref, bv_ref, ws_ref, bsp_ref, goutg_ref,
                  gq_ref, wuq_ref, gkv_ref, wukv_ref, cs_ref,
                  an_ref, q_ref, k_ref, v_ref):
    tm = x_ref.shape[0]
    xn = _rms(x_ref[...], gmix_ref[...]).astype(BF16)
    z = _dot(xn, win_ref[...])
    zk = _dot(xn, wkr_ref[...])

    u = _gelu_tanh(z[:, :GMLP_WIDTH])
    v = _gelu_tanh(z[:, GMLP_WIDTH:2 * GMLP_WIDTH])
    mu = jnp.mean(v, axis=-1, keepdims=True)
    vc = v - mu
    var = jnp.mean(vc * vc, axis=-1, keepdims=True)
    vn = (vc * lax.rsqrt(var + LN_EPS) * gv_ref[...] + bv_ref[...]).astype(BF16)
    a_chunks = []
    for c in range(tm // CHUNK):
        r0 = c * CHUNK
        cols = []
        for g in range(GMLP_HEADS):
            c0 = g * LANES
            sv = _dot(ws_ref[g], vn[r0:r0 + CHUNK, c0:c0 + LANES])
            cols.append(sv)
        sv_all = jnp.concatenate(cols, axis=1) + bsp_ref[...]
        a_chunks.append(u[r0:r0 + CHUNK, :] * sv_all)
    a = jnp.concatenate(a_chunks, axis=0)
    an_ref[...] = _rms(a, goutg_ref[...]).astype(BF16)

    cs = cs_ref[...]
    scale = 1.0 / math.sqrt(QK_NOPE + QK_ROPE)
    cqn = _rms(z[:, C_Q:C_KV], gq_ref[...]).astype(BF16)
    q = _dot(cqn, wuq_ref[...]) * scale
    ckvn = _rms(z[:, C_KV:C_KR], gkv_ref[...]).astype(BF16)
    kv = _dot(ckvn, wukv_ref[...])
    lane = lax.broadcasted_iota(jnp.int32, (tm, LANES), 1)
    kr = zk * cs
    kr = jnp.where(lane < QK_ROPE, kr + pltpu.roll(kr, QK_ROPE, 1), 0.0).astype(BF16)
    for h in range(MLA_HEADS):
        b0 = h * QK_PAD
        qr = q[:, b0 + LANES:b0 + QK_PAD] * cs
        qr = qr + pltpu.roll(qr, QK_ROPE, 1)
        q_ref[:, b0:b0 + LANES] = q[:, b0:b0 + LANES].astype(BF16)
        q_ref[:, b0 + LANES:b0 + QK_PAD] = qr.astype(BF16)
        k_ref[:, b0:b0 + LANES] = kv[:, b0:b0 + LANES].astype(BF16)
        k_ref[:, b0 + LANES:b0 + QK_PAD] = kr
        v_ref[:, h * V_DIM:(h + 1) * V_DIM] = kv[:, b0 + LANES:b0 + QK_PAD].astype(BF16)


def _front(x2d, gmix, win, wkr, gv, bv, ws, bsp, goutg, gq, wuq_ext, gkv, wukv, cs, seq):
    t = x2d.shape[0]
    tm = TM_FRONT
    nseq = seq // tm
    row = lambda i: (i, 0)
    return pl.pallas_call(
        _front_kernel,
        grid=(t // tm,),
        in_specs=[
            pl.BlockSpec((tm, D_MODEL), row),
            _const_spec((1, D_MODEL)),
            _const_spec((D_MODEL, C_KR)),
            _const_spec((D_MODEL, LANES)),
            _const_spec((1, GMLP_WIDTH)),
            _const_spec((1, GMLP_WIDTH)),
            _const_spec((GMLP_HEADS, CHUNK, CHUNK)),
            _const_spec((CHUNK, GMLP_WIDTH)),
            _const_spec((1, GMLP_WIDTH)),
            _const_spec((1, Q_LORA)),
            _const_spec((Q_LORA, MLA_HEADS * QK_PAD)),
            _const_spec((1, KV_LORA)),
            _const_spec((KV_LORA, MLA_HEADS * QK_PAD)),
            pl.BlockSpec((tm, LANES), lambda i: (i % nseq, 0)),
        ],
        out_specs=[
            pl.BlockSpec((tm, GMLP_WIDTH), row),
            pl.BlockSpec((tm, MLA_HEADS * QK_PAD), row),
            pl.BlockSpec((tm, MLA_HEADS * QK_PAD), row),
            pl.BlockSpec((tm, MLA_WIDTH), row),
        ],
        out_shape=[
            jax.ShapeDtypeStruct((t, GMLP_WIDTH), BF16),
            jax.ShapeDtypeStruct((t, MLA_HEADS * QK_PAD), BF16),
            jax.ShapeDtypeStruct((t, MLA_HEADS * QK_PAD), BF16),
            jax.ShapeDtypeStruct((t, MLA_WIDTH), BF16),
        ],
        compiler_params=_cparams(("arbitrary",)),
        name="front",
    )(x2d, gmix, win, wkr, gv, bv, ws, bsp, goutg, gq, wuq_ext, gkv, wukv, cs)


def _attn_kernel(q_ref, k_ref, v_ref, o_ref):
    s_len = q_ref.shape[0]
    row = lax.broadcasted_iota(jnp.int32, (TQ, TQ), 0)
    col = lax.broadcasted_iota(jnp.int32, (TQ, TQ), 1)
    causal = col <= row
    n_blk = s_len // TQ
    for g0 in range(0, n_blk, ATTN_GROUP):
        blocks = list(range(g0, min(g0 + ATTN_GROUP, n_blk)))
        qb = {qi: q_ref[qi * TQ:(qi + 1) * TQ, :] for qi in blocks}
        sd = {qi: jnp.where(causal, _dot_nt(qb[qi], k_ref[qi * TQ:(qi + 1) * TQ, :]), NEG)
              for qi in blocks}
        so = {qi: _dot_nt(qb[qi], k_ref[0:qi * TQ, :]) for qi in blocks if qi > 0}
        m = {qi: jnp.max(sd[qi], axis=-1, keepdims=True) for qi in blocks}
        for qi in so:
            m[qi] = jnp.maximum(m[qi], jnp.max(so[qi], axis=-1, keepdims=True))
        pd = {qi: jnp.exp(sd[qi] - m[qi]) for qi in blocks}
        po = {qi: jnp.exp(so[qi] - m[qi]) for qi in so}
        l = {qi: jnp.sum(pd[qi], axis=-1, keepdims=True) for qi in blocks}
        acc = {qi: _dot(pd[qi].astype(BF16), v_ref[qi * TQ:(qi + 1) * TQ, :]) for qi in blocks}
        for qi in so:
            l[qi] = l[qi] + jnp.sum(po[qi], axis=-1, keepdims=True)
            acc[qi] = acc[qi] + _dot(po[qi].astype(BF16), v_ref[0:qi * TQ, :])
        for qi in blocks:
            o_ref[qi * TQ:(qi + 1) * TQ, :] = (acc[qi] / l[qi]).astype(o_ref.dtype)


def _mla_attn(q, k, v, batch, seq):
    q3 = q.reshape(batch, seq, MLA_HEADS * QK_PAD)
    k3 = k.reshape(batch, seq, MLA_HEADS * QK_PAD)
    v3 = v.reshape(batch, seq, MLA_WIDTH)
    hmap = lambda b, h: (b, 0, h)
    out = pl.pallas_call(
        _attn_kernel,
        grid=(batch, MLA_HEADS),
        in_specs=[
            pl.BlockSpec((None, seq, QK_PAD), hmap),
            pl.BlockSpec((None, seq, QK_PAD), hmap),
            pl.BlockSpec((None, seq, V_DIM), hmap),
        ],
        out_specs=pl.BlockSpec((None, seq, V_DIM), hmap),
        out_shape=jax.ShapeDtypeStruct((batch, seq, MLA_WIDTH), BF16),
        compiler_params=_cparams(("arbitrary", "arbitrary")),
        name="mla_attn",
    )(q3, k3, v3)
    return out.reshape(batch * seq, MLA_WIDTH)


def _memkv_kernel(mem_ref, g_ref, wk_ref, wv_ref, k_ref, v_ref):
    mn = _rms(mem_ref[...], g_ref[...]).astype(BF16)
    k_ref[...] = _dot(mn, wk_ref[...]).astype(BF16)
    v_ref[...] = _dot(mn, wv_ref[...]).astype(BF16)


def _mem_kv(mem, g, wk, wv):
    b, m, d = mem.shape
    bmap = lambda i: (i, 0, 0)
    return pl.pallas_call(
        _memkv_kernel,
        grid=(b,),
        in_specs=[
            pl.BlockSpec((None, m, d), bmap),
            _const_spec((1, d)),
            _const_spec((d, MEM_WIDTH)),
            _const_spec((d, MEM_WIDTH)),
        ],
        out_specs=[pl.BlockSpec((None, m, MEM_WIDTH), bmap)] * 2,
        out_shape=[jax.ShapeDtypeStruct((b, m, MEM_WIDTH), BF16)] * 2,
        compiler_params=_cparams(("arbitrary",)),
        name="mem_kv",
    )(mem, g, wk, wv)


def _row_to_col(x_row):
    r = lax.broadcasted_iota(jnp.int32, (LANES, LANES), 0)
    c = lax.broadcasted_iota(jnp.int32, (LANES, LANES), 1)
    return jnp.sum(jnp.where(r == c, jnp.broadcast_to(x_row, (LANES, LANES)), 0.0), axis=1, keepdims=True)


def _route(lg, state):
    tm = lg.shape[0]
    lane_i = lax.broadcasted_iota(jnp.int32, (tm, LANES), 1)
    lane = lane_i.astype(F32)
    big = float(LANES)
    gmask = lane_i < N_GROUPS
    gl = jnp.where(gmask, lg, NEG)
    gmax = jnp.max(gl, axis=-1, keepdims=True)
    grp = jnp.min(jnp.where(gl == gmax, lane, big), axis=-1, keepdims=True)
    gsum = jnp.sum(jnp.where(gmask, jnp.exp(gl - gmax), 0.0), axis=-1, keepdims=True)
    p_grp = 1.0 / gsum

    lo = (grp + 1.0) * EXPERTS_PER_GROUP
    emask = jnp.logical_and(lane >= lo, lane < lo + EXPERTS_PER_GROUP)
    el = jnp.where(emask, lg, NEG)
    emax = jnp.max(el, axis=-1, keepdims=True)
    ee = jnp.where(emask, jnp.exp(el - emax), 0.0)
    ep = ee / jnp.sum(ee, axis=-1, keepdims=True)
    ep = jnp.where(emask, ep, -1.0)
    p1 = jnp.max(ep, axis=-1, keepdims=True)
    i1 = jnp.min(jnp.where(ep == p1, lane, big), axis=-1, keepdims=True)
    ep2 = jnp.where(lane == i1, -1.0, ep)
    p2 = jnp.max(ep2, axis=-1, keepdims=True)
    i2 = jnp.min(jnp.where(ep2 == p2, lane, big), axis=-1, keepdims=True)
    den = p1 + p2
    g1 = p_grp * (p1 / den)
    g2 = p_grp * (p2 / den)
    e1 = i1 - N_GROUPS
    e2 = i2 - N_GROUPS
    gate = jnp.where(lane_i == 0, g1, jnp.where(lane_i == 1, g2, 0.0))

    cnt, page, npg, tbl = state
    oh1 = (lane == e1).astype(F32)
    oh2 = (lane == e2).astype(F32)
    both = oh1 + oh2
    r_i = lax.broadcasted_iota(jnp.int32, (tm, tm), 0)
    c_i = lax.broadcasted_iota(jnp.int32, (tm, tm), 1)
    before = (c_i < r_i).astype(BF16)
    prefix = _dot(before, both.astype(BF16))
    tc = jnp.sum(both, axis=0, keepdims=True)
    fill = cnt - BM * jnp.floor(cnt * (1.0 / BM))
    need = jnp.logical_and(tc > 0, jnp.logical_or(fill == 0, fill + tc > BM))
    need_f = need.astype(F32)
    e_r = lax.broadcasted_iota(jnp.int32, (LANES, LANES), 0)
    e_c = lax.broadcasted_iota(jnp.int32, (LANES, LANES), 1)
    lower_e = (e_r < e_c).astype(BF16)
    excl = _dot(jnp.broadcast_to(need_f, (ROW_SUB, LANES)).astype(BF16), lower_e)[0:1, :]
    newpage = npg + excl
    pos = fill + prefix
    in_cur = jnp.logical_and(fill > 0, pos < BM)
    slotval = (jnp.where(in_cur, page, newpage) * BM + jnp.where(pos >= BM, pos - BM, pos))
    s1 = jnp.sum(oh1 * slotval, axis=-1, keepdims=True)
    s2 = jnp.sum(oh2 * slotval, axis=-1, keepdims=True)
    slots = jnp.where(lane_i == 0, s1, jnp.where(lane_i == 1, s2, 0.0))

    ordinal = jnp.floor((cnt + (BM - 1)) * (1.0 / BM))
    hit = jnp.logical_and(_row_to_col(need_f) > 0, e_c.astype(F32) == _row_to_col(ordinal))
    tbl = jnp.where(hit, _row_to_col(newpage), tbl)
    new_state = (cnt + tc, jnp.where(need, newpage, page),
                 npg + jnp.sum(need_f, axis=-1, keepdims=True), tbl)
    return slots, gate, new_state


def _mid_kernel(an_ref, m_ref, x_ref, wout_ref, goutm_ref, gxa_ref, wmq_ref, kmem_ref, vmem_ref,
                wmo_ref, gmoe_ref, wr_ref, br_ref, dflt_ref,
                x2_ref, gate_ref, cnt_ref, tbl_ref, xs_ref, rdst_ref,
                state_ref, tbl_s, hp_s, slot_v, slot_s, fin_v, fin_s, rsem, ssem, isem, zsem):
    i = pl.program_id(0)
    n_steps = pl.num_programs(0)
    tm = x_ref.shape[0]
    n_tok = n_steps * tm
    par = i % 2

    def rows_wait(p):
        for k in range(TOP_K):
            pltpu.make_async_copy(hp_s.at[p], xs_ref.at[pl.ds(0, tm * ROW_SUB), :], rsem.at[p]).wait()

    def slot_copy(p):
        return pltpu.make_async_copy(slot_v, slot_s.at[p], ssem.at[p])

    def issue_row(p, step, t):
        for k in range(TOP_K):
            slot = slot_s[p, k, t]
            pltpu.make_async_copy(
                hp_s.at[p, pl.ds(t * ROW_SUB, ROW_SUB), :],
                xs_ref.at[pl.ds(pl.multiple_of(slot * ROW_SUB, ROW_SUB), ROW_SUB), :],
                rsem.at[p]).start()
            rdst_ref[BM + slot] = (k * n_tok + step * tm + t) * ROW_SUB

    @pl.when(i == 0)
    def _():
        state_ref[...] = jnp.zeros(state_ref.shape, state_ref.dtype)
        tbl_s[...] = jnp.zeros(tbl_s.shape, tbl_s.dtype)
        preset = pltpu.make_async_copy(dflt_ref, rdst_ref, isem)
        preset.start()
        hp_s[1] = jnp.zeros(hp_s.shape[1:], hp_s.dtype)
        spare0 = xs_ref.shape[0] // ROW_SUB - TOP_K * tm

        def prime(t, c):
            for k in range(TOP_K):
                slot_s[1, k, t] = spare0 + k * tm + t
            return c

        lax.fori_loop(0, tm, prime, 0)
        preset.wait()

    @pl.when(i >= 1)
    def _():
        rows_wait(par)
        slot_copy(1 - par).wait()

    for t in range(tm):
        issue_row(1 - par, i - 1, t)

    subs = [pl.ds(r0, SUB_MID) for r0 in range(0, x_ref.shape[0], SUB_MID)]
    mn = [_rms(m_ref[r, :].astype(F32), goutm_ref[...]).astype(BF16) for r in subs]
    x1 = [x_ref[r, :] + _dot(an_ref[r, :], wout_ref[:GMLP_WIDTH, :]) + _dot(a, wout_ref[GMLP_WIDTH:, :])
          for r, a in zip(subs, mn)]
    h2 = [_rms(a, gxa_ref[...]).astype(BF16) for a in x1]
    q2 = [(_dot(a, wmq_ref[...]) * (1.0 / math.sqrt(MEM_HEAD_DIM))).astype(BF16) for a in h2]

    def mem_attention(q):
        outs = []
        for h in range(MEM_HEADS):
            c0 = h * MEM_HEAD_DIM
            s = _dot_nt(q[:, c0:c0 + MEM_HEAD_DIM], kmem_ref[:, c0:c0 + MEM_HEAD_DIM])
            m = jnp.max(s, axis=-1, keepdims=True)
            p = jnp.exp(s - m)
            l = jnp.sum(p, axis=-1, keepdims=True)
            outs.append(_dot(p.astype(BF16), vmem_ref[:, c0:c0 + MEM_HEAD_DIM]) / l)
        return jnp.concatenate(outs, axis=1).astype(BF16)

    o = [mem_attention(q) for q in q2]
    x2 = [a + _dot(b, wmo_ref[...]) for a, b in zip(x1, o)]
    for r, a in zip(subs, x2):
        x2_ref[r, :] = a
    h3 = [_rms(a, gmoe_ref[...]) for a in x2]
    half = D_MODEL // 2
    for n, a in enumerate(h3):
        _store_row_tiles(hp_s.at[par, pl.ds(n * SUB_MID * ROW_SUB, SUB_MID * ROW_SUB), :],
                         _pack_pair(a[:, :half], a[:, half:]))
    lg = [_dot(a.astype(BF16), wr_ref[...]) + br_ref[...] for a in h3]
    state = (state_ref[0:1, :], state_ref[1:2, :], state_ref[2:3, :], tbl_s[...])
    for n, (r, a) in enumerate(zip(subs, lg)):
        slots, gate, state = _route(a, state)
        gate_ref[r, :] = gate
        slot_v[:, n * SUB_MID:(n + 1) * SUB_MID] = slots.T[0:ROW_SUB, :].astype(jnp.int32)
    slot_copy(par).start()
    state_ref[0:1, :] = state[0]
    state_ref[1:2, :] = state[1]
    state_ref[2:3, :] = state[2]
    tbl_s[...] = state[3]
    cnt_ref[...] = jnp.broadcast_to(state[0], cnt_ref.shape).astype(jnp.int32)
    tbl_ref[...] = state[3].astype(jnp.int32)

    @pl.when(i == n_steps - 1)
    def _():
        slot_copy(par).wait()

        def last_rows(t, c):
            issue_row(par, i, t)
            return c

        lax.fori_loop(0, tm, last_rows, 0)
        rows_wait(1 - par)
        rows_wait(par)

        cnt_f, page_f, npg_f = state[0], state[1], state[2]
        fill_f = cnt_f - BM * jnp.floor(cnt_f * (1.0 / BM))
        fin_v[...] = jnp.concatenate(
            [page_f * BM + fill_f, jnp.where(fill_f > 0, BM - fill_f, 0.0), npg_f,
             jnp.zeros((ROW_SUB - 3, LANES), F32)], axis=0).astype(jnp.int32)
        fin_cp = pltpu.make_async_copy(fin_v, fin_s, isem)
        fin_cp.start()
        zero_rows = hp_s.at[0, pl.ds(0, BM * ROW_SUB), :]
        zero_rows[...] = jnp.zeros((BM * ROW_SUB, LANES), hp_s.dtype)
        fin_cp.wait()
        _zero_page_tails(fin_s, zero_rows, xs_ref, zsem)


def _mid(an, m, x2d, wout, goutm, gxa, wmq, kmem, vmem, wmo, gmoe, wr, br, dflt, seq, p_rows):
    t = x2d.shape[0]
    tm = TM_MID
    xs_rows = p_rows + TOP_K * tm
    nseq = seq // tm
    mem_len = kmem.shape[1]
    row = lambda i: (i, 0)
    bmap = lambda i: (i // nseq, 0, 0)
    return pl.pallas_call(
        _mid_kernel,
        grid=(t // tm,),
        in_specs=[
            pl.BlockSpec((tm, GMLP_WIDTH), row),
            pl.BlockSpec((tm, MLA_WIDTH), row),
            pl.BlockSpec((tm, D_MODEL), row),
            _const_spec((D_MODEL, D_MODEL)),
            _const_spec((1, MLA_WIDTH)),
            _const_spec((1, D_MODEL)),
            _const_spec((D_MODEL, MEM_WIDTH)),
            pl.BlockSpec((None, mem_len, MEM_WIDTH), bmap),
            pl.BlockSpec((None, mem_len, MEM_WIDTH), bmap),
            _const_spec((MEM_WIDTH, D_MODEL)),
            _const_spec((1, D_MODEL)),
            _const_spec((D_MODEL, LANES)),
            _const_spec((1, LANES)),
            pl.BlockSpec(memory_space=pl.ANY),
        ],
        out_specs=[
            pl.BlockSpec((tm, D_MODEL), row),
            pl.BlockSpec((tm, LANES), row),
            pl.BlockSpec((ROW_SUB, LANES), lambda i: (0, 0)),
            pl.BlockSpec((LANES, LANES), lambda i: (0, 0)),
            pl.BlockSpec(memory_space=pl.ANY),
            pl.BlockSpec(memory_space=pltpu.SMEM),
        ],
        out_shape=[
            jax.ShapeDtypeStruct((t, D_MODEL), F32),
            jax.ShapeDtypeStruct((t, LANES), F32),
            jax.ShapeDtypeStruct((ROW_SUB, LANES), jnp.int32),
            jax.ShapeDtypeStruct((LANES, LANES), jnp.int32),
            jax.ShapeDtypeStruct((xs_rows * ROW_SUB, LANES), U32),
            jax.ShapeDtypeStruct(dflt.shape, jnp.int32),
        ],
        scratch_shapes=[
            pltpu.VMEM((ROW_SUB, LANES), F32),
            pltpu.VMEM((LANES, LANES), F32),
            pltpu.VMEM((2, tm * ROW_SUB, LANES), U32),
            pltpu.VMEM((ROW_SUB, tm), jnp.int32),
            pltpu.SMEM((2, ROW_SUB, tm), jnp.int32),
            pltpu.VMEM((ROW_SUB, LANES), jnp.int32),
            pltpu.SMEM((ROW_SUB, LANES), jnp.int32),
            pltpu.SemaphoreType.DMA((2,)),
            pltpu.SemaphoreType.DMA((2,)),
            pltpu.SemaphoreType.DMA(()),
            pltpu.SemaphoreType.DMA(()),
        ],
        compiler_params=_cparams(("arbitrary",)),
        name="mid",
    )(an, m, x2d, wout, goutm, gxa, wmq, kmem, vmem, wmo, gmoe, wr, br, dflt)


PAD_BITS = tuple(1 << b for b in reversed(range(BM.bit_length() - 1)))


def _zero_page_tails(fin_s, zero_buf, xs_ref, zsem):
    def pad_copy(e, bit):
        n = fin_s[1, e]
        off = fin_s[0, e] + (n & ~(2 * bit - 1))
        return n & bit, pltpu.make_async_copy(
            zero_buf.at[pl.ds(0, bit * ROW_SUB), :],
            xs_ref.at[pl.ds(pl.multiple_of(off * ROW_SUB, ROW_SUB), bit * ROW_SUB), :], zsem)

    def fill(e, c):
        for bit in PAD_BITS:
            on, cp = pad_copy(e, bit)

            @pl.when(on != 0)
            def _():
                cp.start()
        return c

    def fill_wait(e, c):
        for bit in PAD_BITS:
            on, cp = pad_copy(e, bit)

            @pl.when(on != 0)
            def _():
                cp.wait()
        return c

    def tail_copy(b):
        return pltpu.make_async_copy(
            zero_buf, xs_ref.at[pl.ds(pl.multiple_of(b * (BM * ROW_SUB), BM * ROW_SUB), BM * ROW_SUB), :],
            zsem)

    def tail(b, c):
        tail_copy(b).start()
        return c

    def tail_wait(b, c):
        tail_copy(b).wait()
        return c

    n_pages = (xs_ref.shape[0] // ROW_SUB - TOP_K * TM_MID) // BM
    lax.fori_loop(0, N_EXPERTS, fill, 0)
    lax.fori_loop(fin_s[2, 0], n_pages, tail, 0)
    lax.fori_loop(0, N_EXPERTS, fill_wait, 0)
    lax.fori_loop(fin_s[2, 0], n_pages, tail_wait, 0)


def _row_scatter_start(src_vmem, idx_ref, dst_hbm, sem):
    for r in range(BM):
        d0 = pl.multiple_of(idx_ref[0, 0, r], ROW_SUB)
        pltpu.make_async_copy(src_vmem.at[pl.ds(r * ROW_SUB, ROW_SUB), :],
                              dst_hbm.at[pl.ds(d0, ROW_SUB), :], sem).start()


def _scatter_wait(src_vmem, dst_hbm, sem):
    pltpu.make_async_copy(src_vmem, dst_hbm.at[pl.ds(0, BM * ROW_SUB), :], sem).wait()


def _cast_pair_rows(dst_s, src_st):
    half = D_MODEL // 2
    for j in range(ROW_SUB):
        a = j * LANES
        dst_s[2 * a:2 * a + LANES, :] = src_st[a:a + LANES, :].astype(BF16)
        dst_s[2 * a + LANES:2 * a + 2 * LANES, :] = src_st[half + a:half + a + LANES, :].astype(BF16)


def _expert_kernel(be_ref, nxt_ref, nused_ref, segpar_ref, xblk_ref, dblk_ref,
                   dstp_ref, xs_ref, wg_ref, wu_ref, wd_ref,
                   out_ref,
                   wg_st, wu_st, wd_st, wg_s, wu_s, wd_s, ybuf0, ybuf1,
                   wsem, ssem):
    i = pl.program_id(0)
    nused = nused_ref[0]
    used = i < nused
    par = i % 2
    ic = jnp.minimum(i, be_ref.shape[0] - 1)
    e = be_ref[ic]
    first = jnp.logical_or(i == 0, e != be_ref[jnp.maximum(i - 1, 0)])

    def weight_copies(ex, slot):
        return (pltpu.make_async_copy(wg_ref.at[ex], wg_st.at[slot], wsem.at[slot, 0]),
                pltpu.make_async_copy(wu_ref.at[ex], wu_st.at[slot], wsem.at[slot, 1]),
                pltpu.make_async_copy(wd_ref.at[ex], wd_st.at[slot], wsem.at[slot, 2]))

    @pl.when(i == 0)
    def _():
        for cp in weight_copies(e, 0):
            cp.start()
        for a in range(1, W_LOOKAHEAD):
            nxt = nxt_ref[a - 1, 0]

            @pl.when(nxt >= 0)
            def _():
                for cp in weight_copies(nxt, a):
                    cp.start()
        ybuf1[...] = jnp.zeros(ybuf1.shape, ybuf1.dtype)

    def load_weights(slot):
        nxt = nxt_ref[W_LOOKAHEAD - 1, ic]

        @pl.when(nxt >= 0)
        def _():
            for cp in weight_copies(nxt, (slot + W_LOOKAHEAD) % (W_LOOKAHEAD + 1)):
                cp.start()

        for cp in weight_copies(0, slot):
            cp.wait()
        _cast_pair_rows(wg_s, wg_st.at[slot])
        _cast_pair_rows(wu_s, wu_st.at[slot])
        wd_s[...] = wd_st[slot].astype(BF16)

    for slot in range(W_LOOKAHEAD + 1):
        @pl.when(jnp.logical_and(jnp.logical_and(first, used), segpar_ref[ic] == slot))
        def _():
            load_weights(slot)

    def step(yb, yo, s):
        o = 1 - s

        @pl.when(i >= 1)
        def _():
            _scatter_wait(yb, out_ref, ssem.at[s])

        _row_scatter_start(yo, dstp_ref, out_ref, ssem.at[o])
        half = D_MODEL // 2
        cols = []
        for w in _load_row_tiles(xs_ref):
            hi, lo = _unpack_pair(w)
            cols += [hi.astype(BF16), lo.astype(BF16)]
        xrow = jnp.concatenate(cols, axis=1)
        g = _dot(xrow, wg_s[...])
        u = _dot(xrow, wu_s[...])
        hm = (g * jax.nn.sigmoid(g) * u).astype(BF16)
        y = _dot(hm, wd_s[...])
        _store_row_tiles(yb, _pack_pair(y[:, :half], y[:, half:]))

    @pl.when(jnp.logical_and(used, par == 0))
    def _():
        step(ybuf0, ybuf1, 0)

    @pl.when(jnp.logical_and(used, par == 1))
    def _():
        step(ybuf1, ybuf0, 1)

    def drain(yb, yo, s):
        o = 1 - s
        _scatter_wait(yb, out_ref, ssem.at[s])
        _row_scatter_start(yo, dstp_ref, out_ref, ssem.at[o])
        _scatter_wait(yo, out_ref, ssem.at[o])

    @pl.when(jnp.logical_and(i == nused, par == 0))
    def _():
        drain(ybuf0, ybuf1, 0)

    @pl.when(jnp.logical_and(i == nused, par == 1))
    def _():
        drain(ybuf1, ybuf0, 1)


def _experts(block_expert, next_expert, n_used, seg_par, x_blk, d_blk, rows_dst, xs, wg, wu, wd, n_out_rows):
    nb = block_expert.shape[0]
    dst3 = rows_dst.reshape(nb + 1, 1, BM)
    grid_spec = pltpu.PrefetchScalarGridSpec(
        num_scalar_prefetch=6,
        grid=(nb + 1,),
        in_specs=[
            pl.BlockSpec((1, 1, BM), lambda i, be, nx, nu, sp, xb, db: (db[i], 0, 0), memory_space=pltpu.SMEM),
            pl.BlockSpec((BM * ROW_SUB, LANES), lambda i, be, nx, nu, sp, xb, db: (xb[i], 0)),
            pl.BlockSpec(memory_space=pl.ANY),
            pl.BlockSpec(memory_space=pl.ANY),
            pl.BlockSpec(memory_space=pl.ANY),
        ],
        out_specs=pl.BlockSpec(memory_space=pl.ANY),
        scratch_shapes=[
            pltpu.VMEM((W_LOOKAHEAD + 1, D_MODEL, D_EXPERT), F32),
            pltpu.VMEM((W_LOOKAHEAD + 1, D_MODEL, D_EXPERT), F32),
            pltpu.VMEM((W_LOOKAHEAD + 1, D_EXPERT, D_MODEL), F32),
            pltpu.VMEM((D_MODEL, D_EXPERT), BF16),
            pltpu.VMEM((D_MODEL, D_EXPERT), BF16),
            pltpu.VMEM((D_EXPERT, D_MODEL), BF16),
            pltpu.VMEM((BM * ROW_SUB, LANES), U32),
            pltpu.VMEM((BM * ROW_SUB, LANES), U32),
            pltpu.SemaphoreType.DMA((W_LOOKAHEAD + 1, 3)),
            pltpu.SemaphoreType.DMA((2,)),
        ],
    )
    return pl.pallas_call(
        _expert_kernel,
        grid_spec=grid_spec,
        out_shape=jax.ShapeDtypeStruct((n_out_rows * ROW_SUB, LANES), U32),
        compiler_params=pltpu.CompilerParams(dimension_semantics=("arbitrary",),
                                             vmem_limit_bytes=VMEM_LIMIT_EXPERTS),
        name="experts",
    )(block_expert, next_expert, n_used, seg_par, x_blk, d_blk, dst3, xs, wg, wu, wd)


def _combine_kernel(x2_ref, gate_ref, gfin_ref, y0_ref, y1_ref, o_ref):
    gate = gate_ref[...]
    half = D_MODEL // 2
    y_hi = x2_ref[:, :half]
    y_lo = x2_ref[:, half:]
    for k, y_ref in enumerate((y0_ref, y1_ref)):
        pairs = [_unpack_pair(w) for w in _load_row_tiles(y_ref)]
        hi = jnp.concatenate([p[0] for p in pairs], axis=1)
        lo = jnp.concatenate([p[1] for p in pairs], axis=1)
        gk = gate[:, k:k + 1]
        y_hi = y_hi + gk * hi
        y_lo = y_lo + gk * lo
    ms = (jnp.sum(y_hi * y_hi, axis=-1, keepdims=True)
          + jnp.sum(y_lo * y_lo, axis=-1, keepdims=True)) * (1.0 / D_MODEL)
    r = lax.rsqrt(ms + EPS)
    o_ref[:, :half] = y_hi * r * gfin_ref[:, :half]
    o_ref[:, half:] = y_lo * r * gfin_ref[:, half:]


def _combine(x2, gate, gfin, out2):
    t, d = x2.shape
    tm = TM_ROW
    nb = t // tm
    row = lambda i: (i, 0)
    return pl.pallas_call(
        _combine_kernel,
        grid=(nb,),
        in_specs=[
            pl.BlockSpec((tm, d), row),
            pl.BlockSpec((tm, LANES), row),
            _const_spec((1, d)),
            pl.BlockSpec((tm * ROW_SUB, LANES), row),
            pl.BlockSpec((tm * ROW_SUB, LANES), lambda i: (nb + i, 0)),
        ],
        out_specs=pl.BlockSpec((tm, d), row),
        out_shape=jax.ShapeDtypeStruct((t, d), F32),
        compiler_params=_cparams(("arbitrary",)),
        name="combine",
    )(x2, gate, gfin, out2, out2)


def _rot_half_cols(w):
    half = QK_ROPE // 2
    return jnp.concatenate([-w[..., half:], w[..., :half]], axis=-1)


def kernel(x, mem, g_norm_mix, w_in, g_v, b_v, w_spatial, b_spatial, g_q_lora, w_uq, g_kv_lora, w_ukv, g_out_gmlp, g_out_mla, w_out, g_norm_xattn, g_norm_mem, w_mq, w_mk, w_mv, w_mo, g_norm_moe, w_router_group, b_router_group, w_router_expert, b_router_expert, w_exp_gate, w_exp_up, w_exp_down, g_final):
    batch, seq, d = x.shape
    t = batch * seq
    x2d = x.reshape(t, d)
    r2 = lambda a: a.reshape(1, -1)

    w_kr = w_in[:, C_KR:C_KR + QK_ROPE]
    win = w_in[:, :C_KR].astype(BF16)
    wkr = jnp.concatenate([w_kr, _rot_half_cols(w_kr)], axis=1).astype(BF16)
    wq3 = w_uq.reshape(Q_LORA, MLA_HEADS, QK_NOPE + QK_ROPE)
    wq_rope = wq3[..., QK_NOPE:]
    wuq_ext = jnp.concatenate([wq3, _rot_half_cols(wq_rope)], axis=-1)
    wuq_ext = wuq_ext.reshape(Q_LORA, MLA_HEADS * QK_PAD).astype(BF16)
    wukv = w_ukv.astype(BF16)
    causal = jnp.tril(jnp.ones((CHUNK, CHUNK), dtype=bool))
    ws = jnp.where(causal[None], w_spatial, 0.0).astype(BF16)
    bsp = jnp.repeat(b_spatial.T, LANES, axis=1)
    wr = jnp.concatenate(
        [w_router_group, w_router_expert,
         jnp.zeros((d, LANES - N_GROUPS - N_EXPERTS), F32)], axis=1).astype(BF16)
    br = jnp.concatenate(
        [b_router_group, b_router_expert, jnp.zeros((LANES - N_GROUPS - N_EXPERTS,), F32)]).reshape(1, LANES)

    pos = jnp.arange(seq, dtype=F32)
    inv_freq = ROPE_THETA ** (-jnp.arange(0, QK_ROPE, 2, dtype=F32) / QK_ROPE)
    ang = pos[:, None] * inv_freq[None, :]
    cos, sin = jnp.cos(ang), jnp.sin(ang)
    cs = jnp.concatenate([cos, cos, sin, sin], axis=1)

    an, q, k, v = _front(x2d, r2(g_norm_mix), win, wkr, r2(g_v), r2(b_v), ws, bsp, r2(g_out_gmlp),
                         r2(g_q_lora), wuq_ext, r2(g_kv_lora), wukv, cs, seq)
    m = _mla_attn(q, k, v, batch, seq)
    kmem, vmem = _mem_kv(mem, r2(g_norm_mem), w_mk.astype(BF16), w_mv.astype(BF16))
    n_assign = t * TOP_K
    p_rows = n_assign + N_EXPERTS * BM
    nb = p_rows // BM
    n_dst = -(-(BM + p_rows + TOP_K * TM_MID) // SMEM_1D_TILE) * SMEM_1D_TILE
    dflt = (n_assign + jnp.arange(n_dst, dtype=jnp.int32) % BM) * ROW_SUB

    x2, gate, cnt, tbl, xs, rows_dst = _mid(
        an, m, x2d, w_out.astype(BF16), r2(g_out_mla), r2(g_norm_xattn), w_mq.astype(BF16), kmem, vmem,
        w_mo.astype(BF16), r2(g_norm_moe), wr, br, dflt, seq, p_rows)

    ar_e = jnp.arange(N_EXPERTS, dtype=jnp.int32)
    counts = cnt[0, :N_EXPERTS]
    nblk = (counts + BM - 1) // BM
    blk_end = jnp.cumsum(nblk)
    blk_start = blk_end - nblk
    n_used = blk_end[-1:].astype(jnp.int32)
    blk = jnp.arange(nb, dtype=jnp.int32)
    block_expert = jnp.minimum(jnp.sum(blk_end[None, :] <= blk[:, None], axis=1), N_EXPERTS - 1).astype(jnp.int32)
    be_onehot = block_expert[:, None] == ar_e[None, :]
    pick = lambda v: jnp.sum(jnp.where(be_onehot, v[None, :], 0), axis=1)
    nxt_blk = pick(blk_end)
    hops = []
    for _ in range(W_LOOKAHEAD):
        valid = nxt_blk < n_used[0]
        onehot = jnp.minimum(nxt_blk, nb - 1)[:, None] == blk[None, :]
        exp_at = jnp.sum(jnp.where(onehot, block_expert[None, :], 0), axis=1)
        hops.append(jnp.where(valid, exp_at, -1))
        end_at = jnp.sum(jnp.where(exp_at[:, None] == ar_e[None, :], blk_end[None, :], 0), axis=1)
        nxt_blk = jnp.where(valid, end_at, nb)
    next_expert = jnp.stack(hops).astype(jnp.int32)
    seg_id = jnp.cumsum((counts > 0).astype(jnp.int32)) - 1
    seg_par = (pick(seg_id) % (W_LOOKAHEAD + 1)).astype(jnp.int32)
    ordinal = blk - pick(blk_start)
    tbl_rows = jnp.sum(jnp.where(be_onehot[:, :, None], tbl[None, :N_EXPERTS, :N_EXPERTS], 0), axis=1)
    page = jnp.sum(jnp.where(ordinal[:, None] == ar_e[None, :], tbl_rows, 0), axis=1)
    last_used = jnp.sum(jnp.where(blk == n_used[0] - 1, page, 0))
    page = jnp.where(blk < n_used[0], page, last_used)
    x_blk = jnp.concatenate([page, page[-1:]]).astype(jnp.int32)
    d_blk = jnp.concatenate([jnp.zeros((1,), jnp.int32), page + 1]).astype(jnp.int32)

    out2 = _experts(block_expert, next_expert, n_used, seg_par, x_blk, d_blk, rows_dst[:p_rows + BM], xs,
                    w_exp_gate, w_exp_up, w_exp_down, n_assign + BM)
    out = _combine(x2, gate, r2(g_final), out2)
    return out.reshape(batch, seq, d)
```

```python
import math

import jax
import jax.numpy as jnp
import numpy as np
from jax import lax
from jax.experimental import pallas as pl
from jax.experimental.pallas import tpu as pltpu

D_MODEL = 2048
CHUNK = 128
GMLP_HEADS = 8
GMLP_WIDTH = 1024
MLA_HEADS = 8
Q_LORA = 512
KV_LORA = 256
QK_NOPE = 128
QK_ROPE = 64
V_DIM = 128
MLA_WIDTH = MLA_HEADS * V_DIM
ROPE_THETA = 10000.0
MEM_HEADS = 4
MEM_HEAD_DIM = 128
MEM_WIDTH = MEM_HEADS * MEM_HEAD_DIM
N_GROUPS = 8
EXPERTS_PER_GROUP = 8
N_EXPERTS = 64
TOP_K = 2
D_EXPERT = 512
EPS = 1e-6
LN_EPS = 1e-5

LANES = 128
QK_PAD = 256
VMEM_LIMIT = 56 * 1024 * 1024
W_LOOKAHEAD = 1

C_UV = 0
C_Q = 2 * GMLP_WIDTH
C_KV = C_Q + Q_LORA
C_KR = C_KV + KV_LORA
SMEM_1D_TILE = 1024

TM_FRONT = 512
TM_MID = 512
SUB_MID = 256
TQ = 256
ATTN_HEADS = 2
BM = 256
TM_ROW = 512

F32 = jnp.float32
BF16 = jnp.bfloat16
U32 = jnp.uint32
NEG = float(np.finfo(np.float32).min)


def _cparams(sem):
    return pltpu.CompilerParams(dimension_semantics=sem, vmem_limit_bytes=VMEM_LIMIT)


def _const_spec(shape):
    n = len(shape)
    return pl.BlockSpec(shape, lambda *_: (0,) * n, pipeline_mode=pl.Buffered(1))


def _rms(x, g):
    ms = jnp.mean(x * x, axis=-1, keepdims=True)
    return x * lax.rsqrt(ms + EPS) * g


def _gelu_tanh(x):
    c = math.sqrt(2.0 / math.pi)
    return 0.5 * x * (1.0 + jnp.tanh(c * (x + 0.044715 * (x * x * x))))


def _dot(a, b):
    return jnp.dot(a, b, preferred_element_type=F32)


def _dot_nt(a, b):
    return lax.dot_general(a, b, (((1,), (1,)), ((), ())), preferred_element_type=F32)


def _pack_pair(hi, lo):
    hb = pltpu.bitcast(hi.astype(BF16).astype(F32), U32)
    lb = pltpu.bitcast(lo.astype(BF16).astype(F32), U32)
    return hb | (lb >> 16)


def _unpack_pair(w):
    hi = pltpu.bitcast(w & jnp.uint32(0xFFFF0000), F32)
    lo = pltpu.bitcast(w << 16, F32)
    return hi, lo


ROW_SUB = 8
ROW_WORDS = ROW_SUB * LANES


def _store_row_tiles(ref, packed):
    m = packed.shape[0]
    for j in range(ROW_SUB):
        ref[pl.ds(j, m, stride=ROW_SUB), :] = packed[:, j * LANES:(j + 1) * LANES]


def _load_row_tiles(ref):
    m = ref.shape[0] // ROW_SUB
    return [ref[pl.ds(j, m, stride=ROW_SUB), :] for j in range(ROW_SUB)]


def _front_kernel(x_ref, gmix_ref, win_ref, wkr_ref, gv_ref, bv_ref, ws_ref, bsp_ref, goutg_ref,
                  gq_ref, wuq_ref, gkv_ref, wukv_ref, cs_ref,
                  an_ref, q_ref, k_ref, v_ref):
    tm = x_ref.shape[0]
    xn = _rms(x_ref[...], gmix_ref[...]).astype(BF16)
    z = _dot(xn, win_ref[...])
    zk = _dot(xn, wkr_ref[...])

    u = _gelu_tanh(z[:, :GMLP_WIDTH])
    v = _gelu_tanh(z[:, GMLP_WIDTH:2 * GMLP_WIDTH])
    mu = jnp.mean(v, axis=-1, keepdims=True)
    vc = v - mu
    var = jnp.mean(vc * vc, axis=-1, keepdims=True)
    vn = (vc * lax.rsqrt(var + LN_EPS) * gv_ref[...] + bv_ref[...]).astype(BF16)
    a_chunks = []
    for c in range(tm // CHUNK):
        r0 = c * CHUNK
        cols = []
        for g in range(GMLP_HEADS):
            c0 = g * LANES
            sv = _dot(ws_ref[g], vn[r0:r0 + CHUNK, c0:c0 + LANES])
            cols.append(sv)
        sv_all = jnp.concatenate(cols, axis=1) + bsp_ref[...]
        a_chunks.append(u[r0:r0 + CHUNK, :] * sv_all)
    a = jnp.concatenate(a_chunks, axis=0)
    an_ref[...] = _rms(a, goutg_ref[...]).astype(BF16)

    cs = cs_ref[...]
    scale = 1.0 / math.sqrt(QK_NOPE + QK_ROPE)
    cqn = _rms(z[:, C_Q:C_KV], gq_ref[...]).astype(BF16)
    q = _dot(cqn, wuq_ref[...]) * scale
    ckvn = _rms(z[:, C_KV:C_KR], gkv_ref[...]).astype(BF16)
    kv = _dot(ckvn, wukv_ref[...])
    lane = lax.broadcasted_iota(jnp.int32, (tm, LANES), 1)
    kr = zk * cs
    kr = jnp.where(lane < QK_ROPE, kr + pltpu.roll(kr, QK_ROPE, 1), 0.0).astype(BF16)
    for h in range(MLA_HEADS):
        b0 = h * QK_PAD
        qr = q[:, b0 + LANES:b0 + QK_PAD] * cs
        qr = qr + pltpu.roll(qr, QK_ROPE, 1)
        q_ref[:, b0:b0 + LANES] = q[:, b0:b0 + LANES].astype(BF16)
        q_ref[:, b0 + LANES:b0 + QK_PAD] = qr.astype(BF16)
        k_ref[:, b0:b0 + LANES] = kv[:, b0:b0 + LANES].astype(BF16)
        k_ref[:, b0 + LANES:b0 + QK_PAD] = kr
        v_ref[:, h * V_DIM:(h + 1) * V_DIM] = kv[:, b0 + LANES:b0 + QK_PAD].astype(BF16)


def _front(x2d, gmix, win, wkr, gv, bv, ws, bsp, goutg, gq, wuq_ext, gkv, wukv, cs, seq):
    t = x2d.shape[0]
    tm = TM_FRONT
    nseq = seq // tm
    row = lambda i: (i, 0)
    return pl.pallas_call(
        _front_kernel,
        grid=(t // tm,),
        in_specs=[
            pl.BlockSpec((tm, D_MODEL), row),
            _const_spec((1, D_MODEL)),
            _const_spec((D_MODEL, C_KR)),
            _const_spec((D_MODEL, LANES)),
            _const_spec((1, GMLP_WIDTH)),
            _const_spec((1, GMLP_WIDTH)),
            _const_spec((GMLP_HEADS, CHUNK, CHUNK)),
            _const_spec((CHUNK, GMLP_WIDTH)),
            _const_spec((1, GMLP_WIDTH)),
            _const_spec((1, Q_LORA)),
            _const_spec((Q_LORA, MLA_HEADS * QK_PAD)),
            _const_spec((1, KV_LORA)),
            _const_spec((KV_LORA, MLA_HEADS * QK_PAD)),
            pl.BlockSpec((tm, LANES), lambda i: (i % nseq, 0)),
        ],
        out_specs=[
            pl.BlockSpec((tm, GMLP_WIDTH), row),
            pl.BlockSpec((tm, MLA_HEADS * QK_PAD), row),
            pl.BlockSpec((tm, MLA_HEADS * QK_PAD), row),
            pl.BlockSpec((tm, MLA_WIDTH), row),
        ],
        out_shape=[
            jax.ShapeDtypeStruct((t, GMLP_WIDTH), BF16),
            jax.ShapeDtypeStruct((t, MLA_HEADS * QK_PAD), BF16),
            jax.ShapeDtypeStruct((t, MLA_HEADS * QK_PAD), BF16),
            jax.ShapeDtypeStruct((t, MLA_WIDTH), BF16),
        ],
        compiler_params=_cparams(("arbitrary",)),
        name="front",
    )(x2d, gmix, win, wkr, gv, bv, ws, bsp, goutg, gq, wuq_ext, gkv, wukv, cs)


def _attn_kernel(q_ref, k_ref, v_ref, o_ref):
    s_len = q_ref.shape[0]
    row = lax.broadcasted_iota(jnp.int32, (TQ, TQ), 0)
    col = lax.broadcasted_iota(jnp.int32, (TQ, TQ), 1)
    causal = col <= row
    n_blk = s_len // TQ
    heads = range(ATTN_HEADS)
    qk = lambda h: slice(h * QK_PAD, (h + 1) * QK_PAD)
    vc = lambda h: slice(h * V_DIM, (h + 1) * V_DIM)
    for qi in range(n_blk):
        rows = slice(qi * TQ, (qi + 1) * TQ)
        qb = [q_ref[rows, qk(h)] for h in heads]
        sd = [jnp.where(causal, _dot_nt(qb[h], k_ref[rows, qk(h)]), NEG) for h in heads]
        m = [jnp.max(a, axis=-1, keepdims=True) for a in sd]
        if qi > 0:
            so = [_dot_nt(qb[h], k_ref[0:qi * TQ, qk(h)]) for h in heads]
            m = [jnp.maximum(a, jnp.max(b, axis=-1, keepdims=True)) for a, b in zip(m, so)]
        pd = [jnp.exp(a - b) for a, b in zip(sd, m)]
        l = [jnp.sum(a, axis=-1, keepdims=True) for a in pd]
        acc = [_dot(pd[h].astype(BF16), v_ref[rows, vc(h)]) for h in heads]
        if qi > 0:
            po = [jnp.exp(a - b) for a, b in zip(so, m)]
            l = [a + jnp.sum(b, axis=-1, keepdims=True) for a, b in zip(l, po)]
            acc = [acc[h] + _dot(po[h].astype(BF16), v_ref[0:qi * TQ, vc(h)]) for h in heads]
        for h in heads:
            o_ref[rows, vc(h)] = (acc[h] / l[h]).astype(o_ref.dtype)


def _mla_attn(q, k, v, batch, seq):
    q3 = q.reshape(batch, seq, MLA_HEADS * QK_PAD)
    k3 = k.reshape(batch, seq, MLA_HEADS * QK_PAD)
    v3 = v.reshape(batch, seq, MLA_WIDTH)
    hmap = lambda b, h: (b, 0, h)
    out = pl.pallas_call(
        _attn_kernel,
        grid=(batch, MLA_HEADS // ATTN_HEADS),
        in_specs=[
            pl.BlockSpec((None, seq, ATTN_HEADS * QK_PAD), hmap),
            pl.BlockSpec((None, seq, ATTN_HEADS * QK_PAD), hmap),
            pl.BlockSpec((None, seq, ATTN_HEADS * V_DIM), hmap),
        ],
        out_specs=pl.BlockSpec((None, seq, ATTN_HEADS * V_DIM), hmap),
        out_shape=jax.ShapeDtypeStruct((batch, seq, MLA_WIDTH), BF16),
        compiler_params=_cparams(("arbitrary", "arbitrary")),
        name="mla_attn",
    )(q3, k3, v3)
    return out.reshape(batch * seq, MLA_WIDTH)


def _memkv_kernel(mem_ref, g_ref, wk_ref, wv_ref, k_ref, v_ref):
    mn = _rms(mem_ref[...], g_ref[...]).astype(BF16)
    k_ref[...] = _dot(mn, wk_ref[...]).astype(BF16)
    v_ref[...] = _dot(mn, wv_ref[...]).astype(BF16)


def _mem_kv(mem, g, wk, wv):
    b, m, d = mem.shape
    bmap = lambda i: (i, 0, 0)
    return pl.pallas_call(
        _memkv_kernel,
        grid=(b,),
        in_specs=[
            pl.BlockSpec((None, m, d), bmap),
            _const_spec((1, d)),
            _const_spec((d, MEM_WIDTH)),
            _const_spec((d, MEM_WIDTH)),
        ],
        out_specs=[pl.BlockSpec((None, m, MEM_WIDTH), bmap)] * 2,
        out_shape=[jax.ShapeDtypeStruct((b, m, MEM_WIDTH), BF16)] * 2,
        compiler_params=_cparams(("arbitrary",)),
        name="mem_kv",
    )(mem, g, wk, wv)


def _row_to_col(x_row):
    r = lax.broadcasted_iota(jnp.int32, (LANES, LANES), 0)
    c = lax.broadcasted_iota(jnp.int32, (LANES, LANES), 1)
    return jnp.sum(jnp.where(r == c, jnp.broadcast_to(x_row, (LANES, LANES)), 0.0), axis=1, keepdims=True)


def _route(lg, state):
    tm = lg.shape[0]
    lane_i = lax.broadcasted_iota(jnp.int32, (tm, LANES), 1)
    lane = lane_i.astype(F32)
    big = float(LANES)
    gmask = lane_i < N_GROUPS
    gl = jnp.where(gmask, lg, NEG)
    gmax = jnp.max(gl, axis=-1, keepdims=True)
    grp = jnp.min(jnp.where(gl == gmax, lane, big), axis=-1, keepdims=True)
    gsum = jnp.sum(jnp.where(gmask, jnp.exp(gl - gmax), 0.0), axis=-1, keepdims=True)
    p_grp = 1.0 / gsum

    lo = (grp + 1.0) * EXPERTS_PER_GROUP
    emask = jnp.logical_and(lane >= lo, lane < lo + EXPERTS_PER_GROUP)
    el = jnp.where(emask, lg, NEG)
    emax = jnp.max(el, axis=-1, keepdims=True)
    ee = jnp.where(emask, jnp.exp(el - emax), 0.0)
    ep = ee / jnp.sum(ee, axis=-1, keepdims=True)
    ep = jnp.where(emask, ep, -1.0)
    p1 = jnp.max(ep, axis=-1, keepdims=True)
    i1 = jnp.min(jnp.where(ep == p1, lane, big), axis=-1, keepdims=True)
    ep2 = jnp.where(lane == i1, -1.0, ep)
    p2 = jnp.max(ep2, axis=-1, keepdims=True)
    i2 = jnp.min(jnp.where(ep2 == p2, lane, big), axis=-1, keepdims=True)
    den = p1 + p2
    g1 = p_grp * (p1 / den)
    g2 = p_grp * (p2 / den)
    e1 = i1 - N_GROUPS
    e2 = i2 - N_GROUPS
    gate = jnp.where(lane_i == 0, g1, jnp.where(lane_i == 1, g2, 0.0))

    cnt, page, npg, tbl = state
    oh1 = (lane == e1).astype(F32)
    oh2 = (lane == e2).astype(F32)
    both = oh1 + oh2
    r_i = lax.broadcasted_iota(jnp.int32, (tm, tm), 0)
    c_i = lax.broadcasted_iota(jnp.int32, (tm, tm), 1)
    before = (c_i < r_i).astype(BF16)
    prefix = _dot(before, both.astype(BF16))
    tc = jnp.sum(both, axis=0, keepdims=True)
    fill = cnt - BM * jnp.floor(cnt * (1.0 / BM))
    need = jnp.logical_and(tc > 0, jnp.logical_or(fill == 0, fill + tc > BM))
    need_f = need.astype(F32)
    e_r = lax.broadcasted_iota(jnp.int32, (LANES, LANES), 0)
    e_c = lax.broadcasted_iota(jnp.int32, (LANES, LANES), 1)
    lower_e = (e_r < e_c).astype(BF16)
    excl = _dot(jnp.broadcast_to(need_f, (ROW_SUB, LANES)).astype(BF16), lower_e)[0:1, :]
    newpage = npg + excl
    pos = fill + prefix
    in_cur = jnp.logical_and(fill > 0, pos < BM)
    slotval = (jnp.where(in_cur, page, newpage) * BM + jnp.where(pos >= BM, pos - BM, pos))
    s1 = jnp.sum(oh1 * slotval, axis=-1, keepdims=True)
    s2 = jnp.sum(oh2 * slotval, axis=-1, keepdims=True)
    slots = jnp.where(lane_i == 0, s1, jnp.where(lane_i == 1, s2, 0.0))

    ordinal = jnp.floor((cnt + (BM - 1)) * (1.0 / BM))
    hit = jnp.logical_and(_row_to_col(need_f) > 0, e_c.astype(F32) == _row_to_col(ordinal))
    tbl = jnp.where(hit, _row_to_col(newpage), tbl)
    new_state = (cnt + tc, jnp.where(need, newpage, page),
                 npg + jnp.sum(need_f, axis=-1, keepdims=True), tbl)
    return slots, gate, new_state


def _mid_kernel(an_ref, m_ref, x_ref, wout_ref, goutm_ref, gxa_ref, wmq_ref, kmem_ref, vmem_ref,
                wmo_ref, gmoe_ref, wr_ref, br_ref, dflt_ref,
                x2_ref, gate_ref, cnt_ref, tbl_ref, xs_ref, rdst_ref,
                state_ref, tbl_s, hp_s, slot_v, slot_s, fin_v, fin_s, rsem, ssem, isem, zsem):
    i = pl.program_id(0)
    n_steps = pl.num_programs(0)
    tm = x_ref.shape[0]
    n_tok = n_steps * tm
    par = i % 2

    def rows_wait(p):
        for k in range(TOP_K):
            pltpu.make_async_copy(hp_s.at[p], xs_ref.at[pl.ds(0, tm * ROW_SUB), :], rsem.at[p]).wait()

    def slot_copy(p):
        return pltpu.make_async_copy(slot_v, slot_s.at[p], ssem.at[p])

    def issue_row(p, step, t):
        for k in range(TOP_K):
            slot = slot_s[p, k, t]
            pltpu.make_async_copy(
                hp_s.at[p, pl.ds(t * ROW_SUB, ROW_SUB), :],
                xs_ref.at[pl.ds(pl.multiple_of(slot * ROW_SUB, ROW_SUB), ROW_SUB), :],
                rsem.at[p]).start()
            rdst_ref[BM + slot] = (k * n_tok + step * tm + t) * ROW_SUB

    @pl.when(i == 0)
    def _():
        state_ref[...] = jnp.zeros(state_ref.shape, state_ref.dtype)
        tbl_s[...] = jnp.zeros(tbl_s.shape, tbl_s.dtype)
        preset = pltpu.make_async_copy(dflt_ref, rdst_ref, isem)
        preset.start()
        hp_s[1] = jnp.zeros(hp_s.shape[1:], hp_s.dtype)
        spare0 = xs_ref.shape[0] // ROW_SUB - TOP_K * tm

        def prime(t, c):
            for k in range(TOP_K):
                slot_s[1, k, t] = spare0 + k * tm + t
            return c

        lax.fori_loop(0, tm, prime, 0)
        preset.wait()

    @pl.when(i >= 1)
    def _():
        rows_wait(par)
        slot_copy(1 - par).wait()

    for t in range(tm):
        issue_row(1 - par, i - 1, t)

    subs = [pl.ds(r0, SUB_MID) for r0 in range(0, x_ref.shape[0], SUB_MID)]
    mn = [_rms(m_ref[r, :].astype(F32), goutm_ref[...]).astype(BF16) for r in subs]
    x1 = [x_ref[r, :] + _dot(an_ref[r, :], wout_ref[:GMLP_WIDTH, :]) + _dot(a, wout_ref[GMLP_WIDTH:, :])
          for r, a in zip(subs, mn)]
    h2 = [_rms(a, gxa_ref[...]).astype(BF16) for a in x1]
    q2 = [(_dot(a, wmq_ref[...]) * (1.0 / math.sqrt(MEM_HEAD_DIM))).astype(BF16) for a in h2]

    def mem_attention(q):
        outs = []
        for h in range(MEM_HEADS):
            c0 = h * MEM_HEAD_DIM
            s = _dot_nt(q[:, c0:c0 + MEM_HEAD_DIM], kmem_ref[:, c0:c0 + MEM_HEAD_DIM])
            m = jnp.max(s, axis=-1, keepdims=True)
            p = jnp.exp(s - m)
            l = jnp.sum(p, axis=-1, keepdims=True)
            outs.append(_dot(p.astype(BF16), vmem_ref[:, c0:c0 + MEM_HEAD_DIM]) / l)
        return jnp.concatenate(outs, axis=1).astype(BF16)

    o = [mem_attention(q) for q in q2]
    x2 = [a + _dot(b, wmo_ref[...]) for a, b in zip(x1, o)]
    for r, a in zip(subs, x2):
        x2_ref[r, :] = a
    h3 = [_rms(a, gmoe_ref[...]) for a in x2]
    half = D_MODEL // 2
    for n, a in enumerate(h3):
        _store_row_tiles(hp_s.at[par, pl.ds(n * SUB_MID * ROW_SUB, SUB_MID * ROW_SUB), :],
                         _pack_pair(a[:, :half], a[:, half:]))
    lg = [_dot(a.astype(BF16), wr_ref[...]) + br_ref[...] for a in h3]
    state = (state_ref[0:1, :], state_ref[1:2, :], state_ref[2:3, :], tbl_s[...])
    for n, (r, a) in enumerate(zip(subs, lg)):
        slots, gate, state = _route(a, state)
        gate_ref[r, :] = gate
        slot_v[:, n * SUB_MID:(n + 1) * SUB_MID] = slots.T[0:ROW_SUB, :].astype(jnp.int32)
    slot_copy(par).start()
    state_ref[0:1, :] = state[0]
    state_ref[1:2, :] = state[1]
    state_ref[2:3, :] = state[2]
    tbl_s[...] = state[3]
    cnt_ref[...] = jnp.broadcast_to(state[0], cnt_ref.shape).astype(jnp.int32)
    tbl_ref[...] = state[3].astype(jnp.int32)

    @pl.when(i == n_steps - 1)
    def _():
        slot_copy(par).wait()

        def last_rows(t, c):
            issue_row(par, i, t)
            return c

        lax.fori_loop(0, tm, last_rows, 0)
        rows_wait(1 - par)
        rows_wait(par)

        cnt_f, page_f, npg_f = state[0], state[1], state[2]
        fill_f = cnt_f - BM * jnp.floor(cnt_f * (1.0 / BM))
        fin_v[...] = jnp.concatenate(
            [page_f * BM + fill_f, jnp.where(fill_f > 0, BM - fill_f, 0.0), npg_f,
             jnp.zeros((ROW_SUB - 3, LANES), F32)], axis=0).astype(jnp.int32)
        fin_cp = pltpu.make_async_copy(fin_v, fin_s, isem)
        fin_cp.start()
        zero_rows = hp_s.at[0, pl.ds(0, BM * ROW_SUB), :]
        zero_rows[...] = jnp.zeros((BM * ROW_SUB, LANES), hp_s.dtype)
        fin_cp.wait()
        _zero_page_tails(fin_s, zero_rows, xs_ref, zsem)


def _mid(an, m, x2d, wout, goutm, gxa, wmq, kmem, vmem, wmo, gmoe, wr, br, dflt, seq, p_rows):
    t = x2d.shape[0]
    tm = TM_MID
    xs_rows = p_rows + TOP_K * tm
    nseq = seq // tm
    mem_len = kmem.shape[1]
    row = lambda i: (i, 0)
    bmap = lambda i: (i // nseq, 0, 0)
    return pl.pallas_call(
        _mid_kernel,
        grid=(t // tm,),
        in_specs=[
            pl.BlockSpec((tm, GMLP_WIDTH), row),
            pl.BlockSpec((tm, MLA_WIDTH), row),
            pl.BlockSpec((tm, D_MODEL), row),
            _const_spec((D_MODEL, D_MODEL)),
            _const_spec((1, MLA_WIDTH)),
            _const_spec((1, D_MODEL)),
            _const_spec((D_MODEL, MEM_WIDTH)),
            pl.BlockSpec((None, mem_len, MEM_WIDTH), bmap),
            pl.BlockSpec((None, mem_len, MEM_WIDTH), bmap),
            _const_spec((MEM_WIDTH, D_MODEL)),
            _const_spec((1, D_MODEL)),
            _const_spec((D_MODEL, LANES)),
            _const_spec((1, LANES)),
            pl.BlockSpec(memory_space=pl.ANY),
        ],
        out_specs=[
            pl.BlockSpec((tm, D_MODEL), row),
            pl.BlockSpec((tm, LANES), row),
            pl.BlockSpec((ROW_SUB, LANES), lambda i: (0, 0)),
            pl.BlockSpec((LANES, LANES), lambda i: (0, 0)),
            pl.BlockSpec(memory_space=pl.ANY),
            pl.BlockSpec(memory_space=pltpu.SMEM),
        ],
        out_shape=[
            jax.ShapeDtypeStruct((t, D_MODEL), F32),
            jax.ShapeDtypeStruct((t, LANES), F32),
            jax.ShapeDtypeStruct((ROW_SUB, LANES), jnp.int32),
            jax.ShapeDtypeStruct((LANES, LANES), jnp.int32),
            jax.ShapeDtypeStruct((xs_rows * ROW_SUB, LANES), U32),
            jax.ShapeDtypeStruct(dflt.shape, jnp.int32),
        ],
        scratch_shapes=[
            pltpu.VMEM((ROW_SUB, LANES), F32),
            pltpu.VMEM((LANES, LANES), F32),
            pltpu.VMEM((2, tm * ROW_SUB, LANES), U32),
            pltpu.VMEM((ROW_SUB, tm), jnp.int32),
            pltpu.SMEM((2, ROW_SUB, tm), jnp.int32),
            pltpu.VMEM((ROW_SUB, LANES), jnp.int32),
            pltpu.SMEM((ROW_SUB, LANES), jnp.int32),
            pltpu.SemaphoreType.DMA((2,)),
            pltpu.SemaphoreType.DMA((2,)),
            pltpu.SemaphoreType.DMA(()),
            pltpu.SemaphoreType.DMA(()),
        ],
        compiler_params=_cparams(("arbitrary",)),
        name="mid",
    )(an, m, x2d, wout, goutm, gxa, wmq, kmem, vmem, wmo, gmoe, wr, br, dflt)


PAD_BITS = tuple(1 << b for b in reversed(range(BM.bit_length() - 1)))


def _zero_page_tails(fin_s, zero_buf, xs_ref, zsem):
    def pad_copy(e, bit):
        n = fin_s[1, e]
        off = fin_s[0, e] + (n & ~(2 * bit - 1))
        return n & bit, pltpu.make_async_copy(
            zero_buf.at[pl.ds(0, bit * ROW_SUB), :],
            xs_ref.at[pl.ds(pl.multiple_of(off * ROW_SUB, ROW_SUB), bit * ROW_SUB), :], zsem)

    def fill(e, c):
        for bit in PAD_BITS:
            on, cp = pad_copy(e, bit)

            @pl.when(on != 0)
            def _():
                cp.start()
        return c

    def fill_wait(e, c):
        for bit in PAD_BITS:
            on, cp = pad_copy(e, bit)

            @pl.when(on != 0)
            def _():
                cp.wait()
        return c

    def tail_copy(b):
        return pltpu.make_async_copy(
            zero_buf, xs_ref.at[pl.ds(pl.multiple_of(b * (BM * ROW_SUB), BM * ROW_SUB), BM * ROW_SUB), :],
            zsem)

    def tail(b, c):
        tail_copy(b).start()
        return c

    def tail_wait(b, c):
        tail_copy(b).wait()
        return c

    n_pages = (xs_ref.shape[0] // ROW_SUB - TOP_K * TM_MID) // BM
    lax.fori_loop(0, N_EXPERTS, fill, 0)
    lax.fori_loop(fin_s[2, 0], n_pages, tail, 0)
    lax.fori_loop(0, N_EXPERTS, fill_wait, 0)
    lax.fori_loop(fin_s[2, 0], n_pages, tail_wait, 0)


def _row_scatter_start(src_vmem, idx_ref, dst_hbm, sem):
    for r in range(BM):
        d0 = pl.multiple_of(idx_ref[0, 0, r], ROW_SUB)
        pltpu.make_async_copy(src_vmem.at[pl.ds(r * ROW_SUB, ROW_SUB), :],
                              dst_hbm.at[pl.ds(d0, ROW_SUB), :], sem).start()


def _scatter_wait(src_vmem, dst_hbm, sem):
    pltpu.make_async_copy(src_vmem, dst_hbm.at[pl.ds(0, BM * ROW_SUB), :], sem).wait()


def _cast_pair_rows(dst_s, src_st):
    half = D_MODEL // 2
    for j in range(ROW_SUB):
        a = j * LANES
        dst_s[2 * a:2 * a + LANES, :] = src_st[a:a + LANES, :].astype(BF16)
        dst_s[2 * a + LANES:2 * a + 2 * LANES, :] = src_st[half + a:half + a + LANES, :].astype(BF16)


def _expert_kernel(be_ref, nxt_ref, nused_ref, segpar_ref, xblk_ref, dblk_ref,
                   dstp_ref, xs_ref, wg_ref, wu_ref, wd_ref,
                   out_ref,
                   wg_st, wu_st, wd_st, wg_s, wu_s, wd_s, ybuf0, ybuf1,
                   wsem, ssem):
    i = pl.program_id(0)
    nused = nused_ref[0]
    used = i < nused
    par = i % 2
    ic = jnp.minimum(i, be_ref.shape[0] - 1)
    e = be_ref[ic]
    first = jnp.logical_or(i == 0, e != be_ref[jnp.maximum(i - 1, 0)])

    def weight_copies(ex, slot):
        return (pltpu.make_async_copy(wg_ref.at[ex], wg_st.at[slot], wsem.at[slot, 0]),
                pltpu.make_async_copy(wu_ref.at[ex], wu_st.at[slot], wsem.at[slot, 1]),
                pltpu.make_async_copy(wd_ref.at[ex], wd_st.at[slot], wsem.at[slot, 2]))

    @pl.when(i == 0)
    def _():
        for cp in weight_copies(e, 0):
            cp.start()
        for a in range(1, W_LOOKAHEAD):
            nxt = nxt_ref[a - 1, 0]

            @pl.when(nxt >= 0)
            def _():
                for cp in weight_copies(nxt, a):
                    cp.start()
        ybuf1[...] = jnp.zeros(ybuf1.shape, ybuf1.dtype)

    def load_weights(slot):
        nxt = nxt_ref[W_LOOKAHEAD - 1, ic]

        @pl.when(nxt >= 0)
        def _():
            for cp in weight_copies(nxt, (slot + W_LOOKAHEAD) % (W_LOOKAHEAD + 1)):
                cp.start()

        for cp in weight_copies(0, slot):
            cp.wait()
        _cast_pair_rows(wg_s, wg_st.at[slot])
        _cast_pair_rows(wu_s, wu_st.at[slot])
        wd_s[...] = wd_st[slot].astype(BF16)

    for slot in range(W_LOOKAHEAD + 1):
        @pl.when(jnp.logical_and(jnp.logical_and(first, used), segpar_ref[ic] == slot))
        def _():
            load_weights(slot)

    def step(yb, yo, s):
        o = 1 - s

        @pl.when(i >= 1)
        def _():
            _scatter_wait(yb, out_ref, ssem.at[s])

        _row_scatter_start(yo, dstp_ref, out_ref, ssem.at[o])
        half = D_MODEL // 2
        cols = []
        for w in _load_row_tiles(xs_ref):
            hi, lo = _unpack_pair(w)
            cols += [hi.astype(BF16), lo.astype(BF16)]
        xrow = jnp.concatenate(cols, axis=1)
        g = _dot(xrow, wg_s[...])
        u = _dot(xrow, wu_s[...])
        hm = (g * jax.nn.sigmoid(g) * u).astype(BF16)
        y = _dot(hm, wd_s[...])
        _store_row_tiles(yb, _pack_pair(y[:, :half], y[:, half:]))

    @pl.when(jnp.logical_and(used, par == 0))
    def _():
        step(ybuf0, ybuf1, 0)

    @pl.when(jnp.logical_and(used, par == 1))
    def _():
        step(ybuf1, ybuf0, 1)

    def drain(yb, yo, s):
        o = 1 - s
        _scatter_wait(yb, out_ref, ssem.at[s])
        _row_scatter_start(yo, dstp_ref, out_ref, ssem.at[o])
        _scatter_wait(yo, out_ref, ssem.at[o])

    @pl.when(jnp.logical_and(i == nused, par == 0))
    def _():
        drain(ybuf0, ybuf1, 0)

    @pl.when(jnp.logical_and(i == nused, par == 1))
    def _():
        drain(ybuf1, ybuf0, 1)


def _experts(block_expert, next_expert, n_used, seg_par, x_blk, d_blk, rows_dst, xs, wg, wu, wd, n_out_rows):
    nb = block_expert.shape[0]
    dst3 = rows_dst.reshape(nb + 1, 1, BM)
    grid_spec = pltpu.PrefetchScalarGridSpec(
        num_scalar_prefetch=6,
        grid=(nb + 1,),
        in_specs=[
            pl.BlockSpec((1, 1, BM), lambda i, be, nx, nu, sp, xb, db: (db[i], 0, 0), memory_space=pltpu.SMEM),
            pl.BlockSpec((BM * ROW_SUB, LANES), lambda i, be, nx, nu, sp, xb, db: (xb[i], 0)),
            pl.BlockSpec(memory_space=pl.ANY),
            pl.BlockSpec(memory_space=pl.ANY),
            pl.BlockSpec(memory_space=pl.ANY),
        ],
        out_specs=pl.BlockSpec(memory_space=pl.ANY),
        scratch_shapes=[
            pltpu.VMEM((W_LOOKAHEAD + 1, D_MODEL, D_EXPERT), F32),
            pltpu.VMEM((W_LOOKAHEAD + 1, D_MODEL, D_EXPERT), F32),
            pltpu.VMEM((W_LOOKAHEAD + 1, D_EXPERT, D_MODEL), F32),
            pltpu.VMEM((D_MODEL, D_EXPERT), BF16),
            pltpu.VMEM((D_MODEL, D_EXPERT), BF16),
            pltpu.VMEM((D_EXPERT, D_MODEL), BF16),
            pltpu.VMEM((BM * ROW_SUB, LANES), U32),
            pltpu.VMEM((BM * ROW_SUB, LANES), U32),
            pltpu.SemaphoreType.DMA((W_LOOKAHEAD + 1, 3)),
            pltpu.SemaphoreType.DMA((2,)),
        ],
    )
    return pl.pallas_call(
        _expert_kernel,
        grid_spec=grid_spec,
        out_shape=jax.ShapeDtypeStruct((n_out_rows * ROW_SUB, LANES), U32),
        compiler_params=_cparams(("arbitrary",)),
        name="experts",
    )(block_expert, next_expert, n_used, seg_par, x_blk, d_blk, dst3, xs, wg, wu, wd)


def _combine_kernel(x2_ref, gate_ref, gfin_ref, y0_ref, y1_ref, o_ref):
    gate = gate_ref[...]
    half = D_MODEL // 2
    y_hi = x2_ref[:, :half]
    y_lo = x2_ref[:, half:]
    for k, y_ref in enumerate((y0_ref, y1_ref)):
        pairs = [_unpack_pair(w) for w in _load_row_tiles(y_ref)]
        hi = jnp.concatenate([p[0] for p in pairs], axis=1)
        lo = jnp.concatenate([p[1] for p in pairs], axis=1)
        gk = gate[:, k:k + 1]
        y_hi = y_hi + gk * hi
        y_lo = y_lo + gk * lo
    ms = (jnp.sum(y_hi * y_hi, axis=-1, keepdims=True)
          + jnp.sum(y_lo * y_lo, axis=-1, keepdims=True)) * (1.0 / D_MODEL)
    r = lax.rsqrt(ms + EPS)
    o_ref[:, :half] = y_hi * r * gfin_ref[:, :half]
    o_ref[:, half:] = y_lo * r * gfin_ref[:, half:]


def _combine(x2, gate, gfin, out2):
    t, d = x2.shape
    tm = TM_ROW
    nb = t // tm
    row = lambda i: (i, 0)
    return pl.pallas_call(
        _combine_kernel,
        grid=(nb,),
        in_specs=[
            pl.BlockSpec((tm, d), row),
            pl.BlockSpec((tm, LANES), row),
            _const_spec((1, d)),
            pl.BlockSpec((tm * ROW_SUB, LANES), row),
            pl.BlockSpec((tm * ROW_SUB, LANES), lambda i: (nb + i, 0)),
        ],
        out_specs=pl.BlockSpec((tm, d), row),
        out_shape=jax.ShapeDtypeStruct((t, d), F32),
        compiler_params=_cparams(("arbitrary",)),
        name="combine",
    )(x2, gate, gfin, out2, out2)


def _rot_half_cols(w):
    half = QK_ROPE // 2
    return jnp.concatenate([-w[..., half:], w[..., :half]], axis=-1)


def kernel(x, mem, g_norm_mix, w_in, g_v, b_v, w_spatial, b_spatial, g_q_lora, w_uq, g_kv_lora, w_ukv, g_out_gmlp, g_out_mla, w_out, g_norm_xattn, g_norm_mem, w_mq, w_mk, w_mv, w_mo, g_norm_moe, w_router_group, b_router_group, w_router_expert, b_router_expert, w_exp_gate, w_exp_up, w_exp_down, g_final):
    batch, seq, d = x.shape
    t = batch * seq
    x2d = x.reshape(t, d)
    r2 = lambda a: a.reshape(1, -1)

    w_kr = w_in[:, C_KR:C_KR + QK_ROPE]
    win = w_in[:, :C_KR].astype(BF16)
    wkr = jnp.concatenate([w_kr, _rot_half_cols(w_kr)], axis=1).astype(BF16)
    wq3 = w_uq.reshape(Q_LORA, MLA_HEADS, QK_NOPE + QK_ROPE)
    wq_rope = wq3[..., QK_NOPE:]
    wuq_ext = jnp.concatenate([wq3, _rot_half_cols(wq_rope)], axis=-1)
    wuq_ext = wuq_ext.reshape(Q_LORA, MLA_HEADS * QK_PAD).astype(BF16)
    wukv = w_ukv.astype(BF16)
    causal = jnp.tril(jnp.ones((CHUNK, CHUNK), dtype=bool))
    ws = jnp.where(causal[None], w_spatial, 0.0).astype(BF16)
    bsp = jnp.repeat(b_spatial.T, LANES, axis=1)
    wr = jnp.concatenate(
        [w_router_group, w_router_expert,
         jnp.zeros((d, LANES - N_GROUPS - N_EXPERTS), F32)], axis=1).astype(BF16)
    br = jnp.concatenate(
        [b_router_group, b_router_expert, jnp.zeros((LANES - N_GROUPS - N_EXPERTS,), F32)]).reshape(1, LANES)

    pos = jnp.arange(seq, dtype=F32)
    inv_freq = ROPE_THETA ** (-jnp.arange(0, QK_ROPE, 2, dtype=F32) / QK_ROPE)
    ang = pos[:, None] * inv_freq[None, :]
    cos, sin = jnp.cos(ang), jnp.sin(ang)
    cs = jnp.concatenate([cos, cos, sin, sin], axis=1)

    an, q, k, v = _front(x2d, r2(g_norm_mix), win, wkr, r2(g_v), r2(b_v), ws, bsp, r2(g_out_gmlp),
                         r2(g_q_lora), wuq_ext, r2(g_kv_lora), wukv, cs, seq)
    m = _mla_attn(q, k, v, batch, seq)
    kmem, vmem = _mem_kv(mem, r2(g_norm_mem), w_mk.astype(BF16), w_mv.astype(BF16))
    n_assign = t * TOP_K
    p_rows = n_assign + N_EXPERTS * BM
    nb = p_rows // BM
    n_dst = -(-(BM + p_rows + TOP_K * TM_MID) // SMEM_1D_TILE) * SMEM_1D_TILE
    dflt = (n_assign + jnp.arange(n_dst, dtype=jnp.int32) % BM) * ROW_SUB

    x2, gate, cnt, tbl, xs, rows_dst = _mid(
        an, m, x2d, w_out.astype(BF16), r2(g_out_mla), r2(g_norm_xattn), w_mq.astype(BF16), kmem, vmem,
        w_mo.astype(BF16), r2(g_norm_moe), wr, br, dflt, seq, p_rows)

    ar_e = jnp.arange(N_EXPERTS, dtype=jnp.int32)
    counts = cnt[0, :N_EXPERTS]
    nblk = (counts + BM - 1) // BM
    blk_end = jnp.cumsum(nblk)
    blk_start = blk_end - nblk
    n_used = blk_end[-1:].astype(jnp.int32)
    blk = jnp.arange(nb, dtype=jnp.int32)
    block_expert = jnp.minimum(jnp.sum(blk_end[None, :] <= blk[:, None], axis=1), N_EXPERTS - 1).astype(jnp.int32)
    be_onehot = block_expert[:, None] == ar_e[None, :]
    pick = lambda v: jnp.sum(jnp.where(be_onehot, v[None, :], 0), axis=1)
    nxt_blk = pick(blk_end)
    hops = []
    for _ in range(W_LOOKAHEAD):
        valid = nxt_blk < n_used[0]
        onehot = jnp.minimum(nxt_blk, nb - 1)[:, None] == blk[None, :]
        exp_at = jnp.sum(jnp.where(onehot, block_expert[None, :], 0), axis=1)
        hops.append(jnp.where(valid, exp_at, -1))
        end_at = jnp.sum(jnp.where(exp_at[:, None] == ar_e[None, :], blk_end[None, :], 0), axis=1)
        nxt_blk = jnp.where(valid, end_at, nb)
    next_expert = jnp.stack(hops).astype(jnp.int32)
    seg_id = jnp.cumsum((counts > 0).astype(jnp.int32)) - 1
    seg_par = (pick(seg_id) % (W_LOOKAHEAD + 1)).astype(jnp.int32)
    ordinal = blk - pick(blk_start)
    tbl_rows = jnp.sum(jnp.where(be_onehot[:, :, None], tbl[None, :N_EXPERTS, :N_EXPERTS], 0), axis=1)
    page = jnp.sum(jnp.where(ordinal[:, None] == ar_e[None, :], tbl_rows, 0), axis=1)
    last_used = jnp.sum(jnp.where(blk == n_used[0] - 1, page, 0))
    page = jnp.where(blk < n_used[0], page, last_used)
    x_blk = jnp.concatenate([page, page[-1:]]).astype(jnp.int32)
    d_blk = jnp.concatenate([jnp.zeros((1,), jnp.int32), page + 1]).astype(jnp.int32)

    out2 = _experts(block_expert, next_expert, n_used, seg_par, x_blk, d_blk, rows_dst[:p_rows + BM], xs,
                    w_exp_gate, w_exp_up, w_exp_down, n_assign + BM)
    out = _combine(x2, gate, r2(g_final), out2)
    return out.reshape(batch, seq, d)
```

```python
import math

import jax
import jax.numpy as jnp
import numpy as np
from jax import lax
from jax.experimental import pallas as pl
from jax.experimental.pallas import tpu as pltpu

D_MODEL = 2048
CHUNK = 128
GMLP_HEADS = 8
GMLP_WIDTH = 1024
MLA_HEADS = 8
Q_LORA = 512
KV_LORA = 256
QK_NOPE = 128
QK_ROPE = 64
V_DIM = 128
MLA_WIDTH = MLA_HEADS * V_DIM
ROPE_THETA = 10000.0
MEM_HEADS = 4
MEM_HEAD_DIM = 128
MEM_WIDTH = MEM_HEADS * MEM_HEAD_DIM
N_GROUPS = 8
EXPERTS_PER_GROUP = 8
N_EXPERTS = 64
TOP_K = 2
D_EXPERT = 512
EPS = 1e-6
LN_EPS = 1e-5

LANES = 128
QK_PAD = 256
VMEM_LIMIT = 56 * 1024 * 1024
W_LOOKAHEAD = 1

C_UV = 0
C_Q = 2 * GMLP_WIDTH
C_KV = C_Q + Q_LORA
C_KR = C_KV + KV_LORA
SMEM_1D_TILE = 1024

TM_FRONT = 512
TM_MID = 512
SUB_MID = 256
TQ = 256
ATTN_HEADS = 2
BM = 256
TM_ROW = 512

F32 = jnp.float32
BF16 = jnp.bfloat16
U32 = jnp.uint32
NEG = float(np.finfo(np.float32).min)


def _cparams(sem):
    return pltpu.CompilerParams(dimension_semantics=sem, vmem_limit_bytes=VMEM_LIMIT)


def _const_spec(shape):
    n = len(shape)
    return pl.BlockSpec(shape, lambda *_: (0,) * n, pipeline_mode=pl.Buffered(1))


def _rms(x, g):
    ms = jnp.mean(x * x, axis=-1, keepdims=True)
    return x * lax.rsqrt(ms + EPS) * g


def _gelu_tanh(x):
    c = math.sqrt(2.0 / math.pi)
    return 0.5 * x * (1.0 + jnp.tanh(c * (x + 0.044715 * (x * x * x))))


def _dot(a, b):
    return jnp.dot(a, b, preferred_element_type=F32)


def _dot_nt(a, b):
    return lax.dot_general(a, b, (((1,), (1,)), ((), ())), preferred_element_type=F32)


def _pack_pair(hi, lo):
    hb = pltpu.bitcast(hi.astype(BF16).astype(F32), U32)
    lb = pltpu.bitcast(lo.astype(BF16).astype(F32), U32)
    return hb | (lb >> 16)


def _unpack_pair(w):
    hi = pltpu.bitcast(w & jnp.uint32(0xFFFF0000), F32)
    lo = pltpu.bitcast(w << 16, F32)
    return hi, lo


ROW_SUB = 8
ROW_WORDS = ROW_SUB * LANES


def _store_row_tiles(ref, packed):
    m = packed.shape[0]
    for j in range(ROW_SUB):
        ref[pl.ds(j, m, stride=ROW_SUB), :] = packed[:, j * LANES:(j + 1) * LANES]


def _load_row_tiles(ref):
    m = ref.shape[0] // ROW_SUB
    return [ref[pl.ds(j, m, stride=ROW_SUB), :] for j in range(ROW_SUB)]


def _front_kernel(x_ref, gmix_ref, win_ref, wkr_ref, gv_ref, bv_ref, ws_ref, bsp_ref, goutg_ref,
                  gq_ref, wuq_ref, gkv_ref, wukv_ref, cs_ref,
                  an_ref, q_ref, k_ref, v_ref):
    tm = x_ref.shape[0]
    xn = _rms(x_ref[...], gmix_ref[...]).astype(BF16)
    z = _dot(xn, win_ref[...])
    zk = _dot(xn, wkr_ref[...])

    u = _gelu_tanh(z[:, :GMLP_WIDTH])
    v = _gelu_tanh(z[:, GMLP_WIDTH:2 * GMLP_WIDTH])
    mu = jnp.mean(v, axis=-1, keepdims=True)
    vc = v - mu
    var = jnp.mean(vc * vc, axis=-1, keepdims=True)
    vn = (vc * lax.rsqrt(var + LN_EPS) * gv_ref[...] + bv_ref[...]).astype(BF16)
    a_chunks = []
    for c in range(tm // CHUNK):
        r0 = c * CHUNK
        cols = []
        for g in range(GMLP_HEADS):
            c0 = g * LANES
            sv = _dot(ws_ref[g], vn[r0:r0 + CHUNK, c0:c0 + LANES])
            cols.append(sv)
        sv_all = jnp.concatenate(cols, axis=1) + bsp_ref[...]
        a_chunks.append(u[r0:r0 + CHUNK, :] * sv_all)
    a = jnp.concatenate(a_chunks, axis=0)
    an_ref[...] = _rms(a, goutg_ref[...]).astype(BF16)

    cs = cs_ref[...]
    scale = 1.0 / math.sqrt(QK_NOPE + QK_ROPE)
    cqn = _rms(z[:, C_Q:C_KV], gq_ref[...]).astype(BF16)
    q = _dot(cqn, wuq_ref[...]) * scale
    ckvn = _rms(z[:, C_KV:C_KR], gkv_ref[...]).astype(BF16)
    kv = _dot(ckvn, wukv_ref[...])
    lane = lax.broadcasted_iota(jnp.int32, (tm, LANES), 1)
    kr = zk * cs
    kr = jnp.where(lane < QK_ROPE, kr + pltpu.roll(kr, QK_ROPE, 1), 0.0).astype(BF16)
    for h in range(MLA_HEADS):
        b0 = h * QK_PAD
        qr = q[:, b0 + LANES:b0 + QK_PAD] * cs
        qr = qr + pltpu.roll(qr, QK_ROPE, 1)
        q_ref[:, b0:b0 + LANES] = q[:, b0:b0 + LANES].astype(BF16)
        q_ref[:, b0 + LANES:b0 + QK_PAD] = qr.astype(BF16)
        k_ref[:, b0:b0 + LANES] = kv[:, b0:b0 + LANES].astype(BF16)
        k_ref[:, b0 + LANES:b0 + QK_PAD] = kr
        v_ref[:, h * V_DIM:(h + 1) * V_DIM] = kv[:, b0 + LANES:b0 + QK_PAD].astype(BF16)


def _front(x2d, gmix, win, wkr, gv, bv, ws, bsp, goutg, gq, wuq_ext, gkv, wukv, cs, seq):
    t = x2d.shape[0]
    tm = TM_FRONT
    nseq = seq // tm
    row = lambda i: (i, 0)
    return pl.pallas_call(
        _front_kernel,
        grid=(t // tm,),
        in_specs=[
            pl.BlockSpec((tm, D_MODEL), row),
            _const_spec((1, D_MODEL)),
            _const_spec((D_MODEL, C_KR)),
            _const_spec((D_MODEL, LANES)),
            _const_spec((1, GMLP_WIDTH)),
            _const_spec((1, GMLP_WIDTH)),
            _const_spec((GMLP_HEADS, CHUNK, CHUNK)),
            _const_spec((CHUNK, GMLP_WIDTH)),
            _const_spec((1, GMLP_WIDTH)),
            _const_spec((1, Q_LORA)),
            _const_spec((Q_LORA, MLA_HEADS * QK_PAD)),
            _const_spec((1, KV_LORA)),
            _const_spec((KV_LORA, MLA_HEADS * QK_PAD)),
            pl.BlockSpec((tm, LANES), lambda i: (i % nseq, 0)),
        ],
        out_specs=[
            pl.BlockSpec((tm, GMLP_WIDTH), row),
            pl.BlockSpec((tm, MLA_HEADS * QK_PAD), row),
            pl.BlockSpec((tm, MLA_HEADS * QK_PAD), row),
            pl.BlockSpec((tm, MLA_WIDTH), row),
        ],
        out_shape=[
            jax.ShapeDtypeStruct((t, GMLP_WIDTH), BF16),
            jax.ShapeDtypeStruct((t, MLA_HEADS * QK_PAD), BF16),
            jax.ShapeDtypeStruct((t, MLA_HEADS * QK_PAD), BF16),
            jax.ShapeDtypeStruct((t, MLA_WIDTH), BF16),
        ],
        compiler_params=_cparams(("arbitrary",)),
        name="front",
    )(x2d, gmix, win, wkr, gv, bv, ws, bsp, goutg, gq, wuq_ext, gkv, wukv, cs)


def _attn_kernel(q_ref, k_ref, v_ref, o_ref):
    s_len = q_ref.shape[0]
    row = lax.broadcasted_iota(jnp.int32, (TQ, TQ), 0)
    col = lax.broadcasted_iota(jnp.int32, (TQ, TQ), 1)
    causal = col <= row
    n_blk = s_len // TQ
    heads = range(ATTN_HEADS)
    qk = lambda h: slice(h * QK_PAD, (h + 1) * QK_PAD)
    vc = lambda h: slice(h * V_DIM, (h + 1) * V_DIM)
    for qi in range(n_blk):
        rows = slice(qi * TQ, (qi + 1) * TQ)
        qb = [q_ref[rows, qk(h)] for h in heads]
        sd = [jnp.where(causal, _dot_nt(qb[h], k_ref[rows, qk(h)]), NEG) for h in heads]
        m = [jnp.max(a, axis=-1, keepdims=True) for a in sd]
        if qi > 0:
            so = [_dot_nt(qb[h], k_ref[0:qi * TQ, qk(h)]) for h in heads]
            m = [jnp.maximum(a, jnp.max(b, axis=-1, keepdims=True)) for a, b in zip(m, so)]
        pd = [jnp.exp(a - b) for a, b in zip(sd, m)]
        l = [jnp.sum(a, axis=-1, keepdims=True) for a in pd]
        acc = [_dot(pd[h].astype(BF16), v_ref[rows, vc(h)]) for h in heads]
        if qi > 0:
            po = [jnp.exp(a - b) for a, b in zip(so, m)]
            l = [a + jnp.sum(b, axis=-1, keepdims=True) for a, b in zip(l, po)]
            acc = [acc[h] + _dot(po[h].astype(BF16), v_ref[0:qi * TQ, vc(h)]) for h in heads]
        for h in heads:
            o_ref[rows, vc(h)] = (acc[h] / l[h]).astype(o_ref.dtype)


def _mla_attn(q, k, v, batch, seq):
    q3 = q.reshape(batch, seq, MLA_HEADS * QK_PAD)
    k3 = k.reshape(batch, seq, MLA_HEADS * QK_PAD)
    v3 = v.reshape(batch, seq, MLA_WIDTH)
    hmap = lambda b, h: (b, 0, h)
    out = pl.pallas_call(
        _attn_kernel,
        grid=(batch, MLA_HEADS // ATTN_HEADS),
        in_specs=[
            pl.BlockSpec((None, seq, ATTN_HEADS * QK_PAD), hmap),
            pl.BlockSpec((None, seq, ATTN_HEADS * QK_PAD), hmap),
            pl.BlockSpec((None, seq, ATTN_HEADS * V_DIM), hmap),
        ],
        out_specs=pl.BlockSpec((None, seq, ATTN_HEADS * V_DIM), hmap),
        out_shape=jax.ShapeDtypeStruct((batch, seq, MLA_WIDTH), BF16),
        compiler_params=_cparams(("arbitrary", "arbitrary")),
        name="mla_attn",
    )(q3, k3, v3)
    return out.reshape(batch * seq, MLA_WIDTH)


def _memkv_kernel(mem_ref, g_ref, wk_ref, wv_ref, k_ref, v_ref):
    mn = _rms(mem_ref[...], g_ref[...]).astype(BF16)
    k_ref[...] = _dot(mn, wk_ref[...]).astype(BF16)
    v_ref[...] = _dot(mn, wv_ref[...]).astype(BF16)


def _mem_kv(mem, g, wk, wv):
    b, m, d = mem.shape
    bmap = lambda i: (i, 0, 0)
    return pl.pallas_call(
        _memkv_kernel,
        grid=(b,),
        in_specs=[
            pl.BlockSpec((None, m, d), bmap),
            _const_spec((1, d)),
            _const_spec((d, MEM_WIDTH)),
            _const_spec((d, MEM_WIDTH)),
        ],
        out_specs=[pl.BlockSpec((None, m, MEM_WIDTH), bmap)] * 2,
        out_shape=[jax.ShapeDtypeStruct((b, m, MEM_WIDTH), BF16)] * 2,
        compiler_params=_cparams(("arbitrary",)),
        name="mem_kv",
    )(mem, g, wk, wv)


def _row_to_col(x_row):
    r = lax.broadcasted_iota(jnp.int32, (LANES, LANES), 0)
    c = lax.broadcasted_iota(jnp.int32, (LANES, LANES), 1)
    return jnp.sum(jnp.where(r == c, jnp.broadcast_to(x_row, (LANES, LANES)), 0.0), axis=1, keepdims=True)


def _route(lg, state):
    tm = lg.shape[0]
    lane_i = lax.broadcasted_iota(jnp.int32, (tm, LANES), 1)
    lane = lane_i.astype(F32)
    big = float(LANES)
    gmask = lane_i < N_GROUPS
    gl = jnp.where(gmask, lg, NEG)
    gmax = jnp.max(gl, axis=-1, keepdims=True)
    grp = jnp.min(jnp.where(gl == gmax, lane, big), axis=-1, keepdims=True)
    gsum = jnp.sum(jnp.where(gmask, jnp.exp(gl - gmax), 0.0), axis=-1, keepdims=True)
    p_grp = 1.0 / gsum

    lo = (grp + 1.0) * EXPERTS_PER_GROUP
    emask = jnp.logical_and(lane >= lo, lane < lo + EXPERTS_PER_GROUP)
    el = jnp.where(emask, lg, NEG)
    emax = jnp.max(el, axis=-1, keepdims=True)
    ee = jnp.where(emask, jnp.exp(el - emax), 0.0)
    ep = ee / jnp.sum(ee, axis=-1, keepdims=True)
    ep = jnp.where(emask, ep, -1.0)
    p1 = jnp.max(ep, axis=-1, keepdims=True)
    i1 = jnp.min(jnp.where(ep == p1, lane, big), axis=-1, keepdims=True)
    ep2 = jnp.where(lane == i1, -1.0, ep)
    p2 = jnp.max(ep2, axis=-1, keepdims=True)
    i2 = jnp.min(jnp.where(ep2 == p2, lane, big), axis=-1, keepdims=True)
    den = p1 + p2
    g1 = p_grp * (p1 / den)
    g2 = p_grp * (p2 / den)
    e1 = i1 - N_GROUPS
    e2 = i2 - N_GROUPS
    gate = jnp.where(lane_i == 0, g1, jnp.where(lane_i == 1, g2, 0.0))

    cnt, page, npg, tbl = state
    oh1 = (lane == e1).astype(F32)
    oh2 = (lane == e2).astype(F32)
    both = oh1 + oh2
    r_i = lax.broadcasted_iota(jnp.int32, (tm, tm), 0)
    c_i = lax.broadcasted_iota(jnp.int32, (tm, tm), 1)
    before = (c_i < r_i).astype(BF16)
    prefix = _dot(before, both.astype(BF16))
    tc = jnp.sum(both, axis=0, keepdims=True)
    fill = cnt - BM * jnp.floor(cnt * (1.0 / BM))
    need = jnp.logical_and(tc > 0, jnp.logical_or(fill == 0, fill + tc > BM))
    need_f = need.astype(F32)
    e_r = lax.broadcasted_iota(jnp.int32, (LANES, LANES), 0)
    e_c = lax.broadcasted_iota(jnp.int32, (LANES, LANES), 1)
    lower_e = (e_r < e_c).astype(BF16)
    excl = _dot(jnp.broadcast_to(need_f, (ROW_SUB, LANES)).astype(BF16), lower_e)[0:1, :]
    newpage = npg + excl
    pos = fill + prefix
    in_cur = jnp.logical_and(fill > 0, pos < BM)
    slotval = (jnp.where(in_cur, page, newpage) * BM + jnp.where(pos >= BM, pos - BM, pos))
    s1 = jnp.sum(oh1 * slotval, axis=-1, keepdims=True)
    s2 = jnp.sum(oh2 * slotval, axis=-1, keepdims=True)
    slots = jnp.where(lane_i == 0, s1, jnp.where(lane_i == 1, s2, 0.0))

    ordinal = jnp.floor((cnt + (BM - 1)) * (1.0 / BM))
    hit = jnp.logical_and(_row_to_col(need_f) > 0, e_c.astype(F32) == _row_to_col(ordinal))
    tbl = jnp.where(hit, _row_to_col(newpage), tbl)
    new_state = (cnt + tc, jnp.where(need, newpage, page),
                 npg + jnp.sum(need_f, axis=-1, keepdims=True), tbl)
    return slots, gate, new_state


def _mid_kernel(an_ref, m_ref, x_ref, wout_ref, goutm_ref, gxa_ref, wmq_ref, kmem_ref, vmem_ref,
                wmo_ref, gmoe_ref, wr_ref, br_ref, dflt_ref,
                x2_ref, gate_ref, cnt_ref, tbl_ref, xs_ref, rdst_ref,
                state_ref, tbl_s, hp_s, slot_v, slot_s, fin_v, fin_s, rsem, ssem, isem, zsem):
    i = pl.program_id(0)
    n_steps = pl.num_programs(0)
    tm = x_ref.shape[0]
    n_tok = n_steps * tm
    par = i % 2

    def rows_wait(p):
        for k in range(TOP_K):
            pltpu.make_async_copy(hp_s.at[p], xs_ref.at[pl.ds(0, tm * ROW_SUB), :], rsem.at[p]).wait()

    def slot_copy(p):
        return pltpu.make_async_copy(slot_v, slot_s.at[p], ssem.at[p])

    def issue_row(p, step_row8, t):
        for k in range(TOP_K):
            slot = slot_s[p, k, t]
            pltpu.make_async_copy(
                hp_s.at[p, pl.ds(t * ROW_SUB, ROW_SUB), :],
                xs_ref.at[pl.ds(pl.multiple_of(slot * ROW_SUB, ROW_SUB), ROW_SUB), :],
                rsem.at[p]).start()
            rdst_ref[BM + slot] = step_row8 + (k * n_tok + t) * ROW_SUB

    @pl.when(i == 0)
    def _():
        state_ref[...] = jnp.zeros(state_ref.shape, state_ref.dtype)
        tbl_s[...] = jnp.zeros(tbl_s.shape, tbl_s.dtype)
        preset = pltpu.make_async_copy(dflt_ref, rdst_ref, isem)
        preset.start()
        hp_s[1] = jnp.zeros(hp_s.shape[1:], hp_s.dtype)
        spare0 = xs_ref.shape[0] // ROW_SUB - TOP_K * tm

        def prime(t, c):
            for k in range(TOP_K):
                slot_s[1, k, t] = spare0 + k * tm + t
            return c

        lax.fori_loop(0, tm, prime, 0)
        preset.wait()

    @pl.when(i >= 1)
    def _():
        rows_wait(par)
        slot_copy(1 - par).wait()

    def step_body(p):
        prev_row8 = (i - 1) * (tm * ROW_SUB)
        for t in range(tm):
            issue_row(1 - p, prev_row8, t)

        subs = [pl.ds(r0, SUB_MID) for r0 in range(0, x_ref.shape[0], SUB_MID)]
        mn = [_rms(m_ref[r, :].astype(F32), goutm_ref[...]).astype(BF16) for r in subs]
        x1 = [x_ref[r, :] + _dot(an_ref[r, :], wout_ref[:GMLP_WIDTH, :]) + _dot(a, wout_ref[GMLP_WIDTH:, :])
              for r, a in zip(subs, mn)]
        h2 = [_rms(a, gxa_ref[...]).astype(BF16) for a in x1]
        q2 = [(_dot(a, wmq_ref[...]) * (1.0 / math.sqrt(MEM_HEAD_DIM))).astype(BF16) for a in h2]

        def mem_attention(q):
            outs = []
            for h in range(MEM_HEADS):
                c0 = h * MEM_HEAD_DIM
                s = _dot_nt(q[:, c0:c0 + MEM_HEAD_DIM], kmem_ref[:, c0:c0 + MEM_HEAD_DIM])
                m = jnp.max(s, axis=-1, keepdims=True)
                e = jnp.exp(s - m)
                l = jnp.sum(e, axis=-1, keepdims=True)
                outs.append(_dot(e.astype(BF16), vmem_ref[:, c0:c0 + MEM_HEAD_DIM]) / l)
            return jnp.concatenate(outs, axis=1).astype(BF16)

        o = [mem_attention(q) for q in q2]
        x2 = [a + _dot(b, wmo_ref[...]) for a, b in zip(x1, o)]
        for r, a in zip(subs, x2):
            x2_ref[r, :] = a
        h3 = [_rms(a, gmoe_ref[...]) for a in x2]
        half = D_MODEL // 2
        for n, a in enumerate(h3):
            _store_row_tiles(hp_s.at[p, pl.ds(n * SUB_MID * ROW_SUB, SUB_MID * ROW_SUB), :],
                             _pack_pair(a[:, :half], a[:, half:]))
        lg = [_dot(a.astype(BF16), wr_ref[...]) + br_ref[...] for a in h3]
        state = (state_ref[0:1, :], state_ref[1:2, :], state_ref[2:3, :], tbl_s[...])
        for n, (r, a) in enumerate(zip(subs, lg)):
            slots, gate, state = _route(a, state)
            gate_ref[r, :] = gate
            slot_v[:, n * SUB_MID:(n + 1) * SUB_MID] = slots.T[0:ROW_SUB, :].astype(jnp.int32)
        slot_copy(p).start()
        state_ref[0:1, :] = state[0]
        state_ref[1:2, :] = state[1]
        state_ref[2:3, :] = state[2]
        tbl_s[...] = state[3]
        cnt_ref[...] = jnp.broadcast_to(state[0], cnt_ref.shape).astype(jnp.int32)
        tbl_ref[...] = state[3].astype(jnp.int32)

    for p in range(2):
        pl.when(par == p)(lambda p=p: step_body(p))

    @pl.when(i == n_steps - 1)
    def _():
        slot_copy(par).wait()

        def last_rows(t, c):
            issue_row(par, i * (tm * ROW_SUB), t)
            return c

        lax.fori_loop(0, tm, last_rows, 0)
        rows_wait(1 - par)
        rows_wait(par)

        cnt_f, page_f, npg_f = state_ref[0:1, :], state_ref[1:2, :], state_ref[2:3, :]
        fill_f = cnt_f - BM * jnp.floor(cnt_f * (1.0 / BM))
        fin_v[...] = jnp.concatenate(
            [page_f * BM + fill_f, jnp.where(fill_f > 0, BM - fill_f, 0.0), npg_f,
             jnp.zeros((ROW_SUB - 3, LANES), F32)], axis=0).astype(jnp.int32)
        fin_cp = pltpu.make_async_copy(fin_v, fin_s, isem)
        fin_cp.start()
        zero_rows = hp_s.at[0, pl.ds(0, BM * ROW_SUB), :]
        zero_rows[...] = jnp.zeros((BM * ROW_SUB, LANES), hp_s.dtype)
        fin_cp.wait()
        _zero_page_tails(fin_s, zero_rows, xs_ref, zsem)


def _mid(an, m, x2d, wout, goutm, gxa, wmq, kmem, vmem, wmo, gmoe, wr, br, dflt, seq, p_rows):
    t = x2d.shape[0]
    tm = TM_MID
    xs_rows = p_rows + TOP_K * tm
    nseq = seq // tm
    mem_len = kmem.shape[1]
    row = lambda i: (i, 0)
    bmap = lambda i: (i // nseq, 0, 0)
    return pl.pallas_call(
        _mid_kernel,
        grid=(t // tm,),
        in_specs=[
            pl.BlockSpec((tm, GMLP_WIDTH), row),
            pl.BlockSpec((tm, MLA_WIDTH), row),
            pl.BlockSpec((tm, D_MODEL), row),
            _const_spec((D_MODEL, D_MODEL)),
            _const_spec((1, MLA_WIDTH)),
            _const_spec((1, D_MODEL)),
            _const_spec((D_MODEL, MEM_WIDTH)),
            pl.BlockSpec((None, mem_len, MEM_WIDTH), bmap),
            pl.BlockSpec((None, mem_len, MEM_WIDTH), bmap),
            _const_spec((MEM_WIDTH, D_MODEL)),
            _const_spec((1, D_MODEL)),
            _const_spec((D_MODEL, LANES)),
            _const_spec((1, LANES)),
            pl.BlockSpec(memory_space=pl.ANY),
        ],
        out_specs=[
            pl.BlockSpec((tm, D_MODEL), row),
            pl.BlockSpec((tm, LANES), row),
            pl.BlockSpec((ROW_SUB, LANES), lambda i: (0, 0)),
            pl.BlockSpec((LANES, LANES), lambda i: (0, 0)),
            pl.BlockSpec(memory_space=pl.ANY),
            pl.BlockSpec(memory_space=pltpu.SMEM),
        ],
        out_shape=[
            jax.ShapeDtypeStruct((t, D_MODEL), F32),
            jax.ShapeDtypeStruct((t, LANES), F32),
            jax.ShapeDtypeStruct((ROW_SUB, LANES), jnp.int32),
            jax.ShapeDtypeStruct((LANES, LANES), jnp.int32),
            jax.ShapeDtypeStruct((xs_rows * ROW_SUB, LANES), U32),
            jax.ShapeDtypeStruct(dflt.shape, jnp.int32),
        ],
        scratch_shapes=[
            pltpu.VMEM((ROW_SUB, LANES), F32),
            pltpu.VMEM((LANES, LANES), F32),
            pltpu.VMEM((2, tm * ROW_SUB, LANES), U32),
            pltpu.VMEM((ROW_SUB, tm), jnp.int32),
            pltpu.SMEM((2, ROW_SUB, tm), jnp.int32),
            pltpu.VMEM((ROW_SUB, LANES), jnp.int32),
            pltpu.SMEM((ROW_SUB, LANES), jnp.int32),
            pltpu.SemaphoreType.DMA((2,)),
            pltpu.SemaphoreType.DMA((2,)),
            pltpu.SemaphoreType.DMA(()),
            pltpu.SemaphoreType.DMA(()),
        ],
        compiler_params=_cparams(("arbitrary",)),
        name="mid",
    )(an, m, x2d, wout, goutm, gxa, wmq, kmem, vmem, wmo, gmoe, wr, br, dflt)


PAD_BITS = tuple(1 << b for b in reversed(range(BM.bit_length() - 1)))


def _zero_page_tails(fin_s, zero_buf, xs_ref, zsem):
    def pad_copy(e, bit):
        n = fin_s[1, e]
        off = fin_s[0, e] + (n & ~(2 * bit - 1))
        return n & bit, pltpu.make_async_copy(
            zero_buf.at[pl.ds(0, bit * ROW_SUB), :],
            xs_ref.at[pl.ds(pl.multiple_of(off * ROW_SUB, ROW_SUB), bit * ROW_SUB), :], zsem)

    def fill(e, c):
        for bit in PAD_BITS:
            on, cp = pad_copy(e, bit)

            @pl.when(on != 0)
            def _():
                cp.start()
        return c

    def fill_wait(e, c):
        for bit in PAD_BITS:
            on, cp = pad_copy(e, bit)

            @pl.when(on != 0)
            def _():
                cp.wait()
        return c

    def tail_copy(b):
        return pltpu.make_async_copy(
            zero_buf, xs_ref.at[pl.ds(pl.multiple_of(b * (BM * ROW_SUB), BM * ROW_SUB), BM * ROW_SUB), :],
            zsem)

    def tail(b, c):
        tail_copy(b).start()
        return c

    def tail_wait(b, c):
        tail_copy(b).wait()
        return c

    n_pages = (xs_ref.shape[0] // ROW_SUB - TOP_K * TM_MID) // BM
    lax.fori_loop(0, N_EXPERTS, fill, 0)
    lax.fori_loop(fin_s[2, 0], n_pages, tail, 0)
    lax.fori_loop(0, N_EXPERTS, fill_wait, 0)
    lax.fori_loop(fin_s[2, 0], n_pages, tail_wait, 0)


def _row_scatter_start(src_vmem, idx_ref, dst_hbm, sem):
    for r in range(BM):
        d0 = pl.multiple_of(idx_ref[0, 0, r], ROW_SUB)
        pltpu.make_async_copy(src_vmem.at[pl.ds(r * ROW_SUB, ROW_SUB), :],
                              dst_hbm.at[pl.ds(d0, ROW_SUB), :], sem).start()


def _scatter_wait(src_vmem, dst_hbm, sem):
    pltpu.make_async_copy(src_vmem, dst_hbm.at[pl.ds(0, BM * ROW_SUB), :], sem).wait()


def _cast_pair_rows(dst_s, src_st):
    half = D_MODEL // 2
    for j in range(ROW_SUB):
        a = j * LANES
        dst_s[2 * a:2 * a + LANES, :] = src_st[a:a + LANES, :].astype(BF16)
        dst_s[2 * a + LANES:2 * a + 2 * LANES, :] = src_st[half + a:half + a + LANES, :].astype(BF16)


def _expert_kernel(be_ref, nxt_ref, nused_ref, segpar_ref, xblk_ref, dblk_ref,
                   dstp_ref, xs_ref, wg_ref, wu_ref, wd_ref,
                   out_ref,
                   wg_st, wu_st, wd_st, wg_s, wu_s, wd_s, ybuf0, ybuf1,
                   wsem, ssem):
    i = pl.program_id(0)
    nused = nused_ref[0]
    used = i < nused
    par = i % 2
    ic = jnp.minimum(i, be_ref.shape[0] - 1)
    e = be_ref[ic]
    first = jnp.logical_or(i == 0, e != be_ref[jnp.maximum(i - 1, 0)])

    def weight_copies(ex, slot):
        return (pltpu.make_async_copy(wg_ref.at[ex], wg_st.at[slot], wsem.at[slot, 0]),
                pltpu.make_async_copy(wu_ref.at[ex], wu_st.at[slot], wsem.at[slot, 1]),
                pltpu.make_async_copy(wd_ref.at[ex], wd_st.at[slot], wsem.at[slot, 2]))

    @pl.when(i == 0)
    def _():
        for cp in weight_copies(e, 0):
            cp.start()
        for a in range(1, W_LOOKAHEAD):
            nxt = nxt_ref[a - 1, 0]

            @pl.when(nxt >= 0)
            def _():
                for cp in weight_copies(nxt, a):
                    cp.start()
        ybuf1[...] = jnp.zeros(ybuf1.shape, ybuf1.dtype)

    def load_weights(slot):
        nxt = nxt_ref[W_LOOKAHEAD - 1, ic]

        @pl.when(nxt >= 0)
        def _():
            for cp in weight_copies(nxt, (slot + W_LOOKAHEAD) % (W_LOOKAHEAD + 1)):
                cp.start()

        for cp in weight_copies(0, slot):
            cp.wait()
        _cast_pair_rows(wg_s, wg_st.at[slot])
        _cast_pair_rows(wu_s, wu_st.at[slot])
        wd_s[...] = wd_st[slot].astype(BF16)

    for slot in range(W_LOOKAHEAD + 1):
        @pl.when(jnp.logical_and(jnp.logical_and(first, used), segpar_ref[ic] == slot))
        def _():
            load_weights(slot)

    def step(yb, yo, s):
        o = 1 - s

        @pl.when(i >= 1)
        def _():
            _scatter_wait(yb, out_ref, ssem.at[s])

        _row_scatter_start(yo, dstp_ref, out_ref, ssem.at[o])
        half = D_MODEL // 2
        cols = []
        for w in _load_row_tiles(xs_ref):
            hi, lo = _unpack_pair(w)
            cols += [hi.astype(BF16), lo.astype(BF16)]
        xrow = jnp.concatenate(cols, axis=1)
        g = _dot(xrow, wg_s[...])
        u = _dot(xrow, wu_s[...])
        hm = (g * jax.nn.sigmoid(g) * u).astype(BF16)
        y = _dot(hm, wd_s[...])
        _store_row_tiles(yb, _pack_pair(y[:, :half], y[:, half:]))

    @pl.when(jnp.logical_and(used, par == 0))
    def _():
        step(ybuf0, ybuf1, 0)

    @pl.when(jnp.logical_and(used, par == 1))
    def _():
        step(ybuf1, ybuf0, 1)

    def drain(yb, yo, s):
        o = 1 - s
        _scatter_wait(yb, out_ref, ssem.at[s])
        _row_scatter_start(yo, dstp_ref, out_ref, ssem.at[o])
        _scatter_wait(yo, out_ref, ssem.at[o])

    @pl.when(jnp.logical_and(i == nused, par == 0))
    def _():
        drain(ybuf0, ybuf1, 0)

    @pl.when(jnp.logical_and(i == nused, par == 1))
    def _():
        drain(ybuf1, ybuf0, 1)


def _experts(block_expert, next_expert, n_used, seg_par, x_blk, d_blk, rows_dst, xs, wg, wu, wd, n_out_rows):
    nb = block_expert.shape[0]
    dst3 = rows_dst.reshape(nb + 1, 1, BM)
    grid_spec = pltpu.PrefetchScalarGridSpec(
        num_scalar_prefetch=6,
        grid=(nb + 1,),
        in_specs=[
            pl.BlockSpec((1, 1, BM), lambda i, be, nx, nu, sp, xb, db: (db[i], 0, 0), memory_space=pltpu.SMEM),
            pl.BlockSpec((BM * ROW_SUB, LANES), lambda i, be, nx, nu, sp, xb, db: (xb[i], 0)),
            pl.BlockSpec(memory_space=pl.ANY),
            pl.BlockSpec(memory_space=pl.ANY),
            pl.BlockSpec(memory_space=pl.ANY),
        ],
        out_specs=pl.BlockSpec(memory_space=pl.ANY),
        scratch_shapes=[
            pltpu.VMEM((W_LOOKAHEAD + 1, D_MODEL, D_EXPERT), F32),
            pltpu.VMEM((W_LOOKAHEAD + 1, D_MODEL, D_EXPERT), F32),
            pltpu.VMEM((W_LOOKAHEAD + 1, D_EXPERT, D_MODEL), F32),
            pltpu.VMEM((D_MODEL, D_EXPERT), BF16),
            pltpu.VMEM((D_MODEL, D_EXPERT), BF16),
            pltpu.VMEM((D_EXPERT, D_MODEL), BF16),
            pltpu.VMEM((BM * ROW_SUB, LANES), U32),
            pltpu.VMEM((BM * ROW_SUB, LANES), U32),
            pltpu.SemaphoreType.DMA((W_LOOKAHEAD + 1, 3)),
            pltpu.SemaphoreType.DMA((2,)),
        ],
    )
    return pl.pallas_call(
        _expert_kernel,
        grid_spec=grid_spec,
        out_shape=jax.ShapeDtypeStruct((n_out_rows * ROW_SUB, LANES), U32),
        compiler_params=_cparams(("arbitrary",)),
        name="experts",
    )(block_expert, next_expert, n_used, seg_par, x_blk, d_blk, dst3, xs, wg, wu, wd)


def _combine_kernel(x2_ref, gate_ref, gfin_ref, y0_ref, y1_ref, o_ref):
    gate = gate_ref[...]
    half = D_MODEL // 2
    y_hi = x2_ref[:, :half]
    y_lo = x2_ref[:, half:]
    for k, y_ref in enumerate((y0_ref, y1_ref)):
        pairs = [_unpack_pair(w) for w in _load_row_tiles(y_ref)]
        hi = jnp.concatenate([p[0] for p in pairs], axis=1)
        lo = jnp.concatenate([p[1] for p in pairs], axis=1)
        gk = gate[:, k:k + 1]
        y_hi = y_hi + gk * hi
        y_lo = y_lo + gk * lo
    ms = (jnp.sum(y_hi * y_hi, axis=-1, keepdims=True)
          + jnp.sum(y_lo * y_lo, axis=-1, keepdims=True)) * (1.0 / D_MODEL)
    r = lax.rsqrt(ms + EPS)
    o_ref[:, :half] = y_hi * r * gfin_ref[:, :half]
    o_ref[:, half:] = y_lo * r * gfin_ref[:, half:]


def _combine(x2, gate, gfin, out2):
    t, d = x2.shape
    tm = TM_ROW
    nb = t // tm
    row = lambda i: (i, 0)
    return pl.pallas_call(
        _combine_kernel,
        grid=(nb,),
        in_specs=[
            pl.BlockSpec((tm, d), row),
            pl.BlockSpec((tm, LANES), row),
            _const_spec((1, d)),
            pl.BlockSpec((tm * ROW_SUB, LANES), row),
            pl.BlockSpec((tm * ROW_SUB, LANES), lambda i: (nb + i, 0)),
        ],
        out_specs=pl.BlockSpec((tm, d), row),
        out_shape=jax.ShapeDtypeStruct((t, d), F32),
        compiler_params=_cparams(("arbitrary",)),
        name="combine",
    )(x2, gate, gfin, out2, out2)


def _rot_half_cols(w):
    half = QK_ROPE // 2
    return jnp.concatenate([-w[..., half:], w[..., :half]], axis=-1)


def kernel(x, mem, g_norm_mix, w_in, g_v, b_v, w_spatial, b_spatial, g_q_lora, w_uq, g_kv_lora, w_ukv, g_out_gmlp, g_out_mla, w_out, g_norm_xattn, g_norm_mem, w_mq, w_mk, w_mv, w_mo, g_norm_moe, w_router_group, b_router_group, w_router_expert, b_router_expert, w_exp_gate, w_exp_up, w_exp_down, g_final):
    batch, seq, d = x.shape
    t = batch * seq
    x2d = x.reshape(t, d)
    r2 = lambda a: a.reshape(1, -1)

    w_kr = w_in[:, C_KR:C_KR + QK_ROPE]
    win = w_in[:, :C_KR].astype(BF16)
    wkr = jnp.concatenate([w_kr, _rot_half_cols(w_kr)], axis=1).astype(BF16)
    wq3 = w_uq.reshape(Q_LORA, MLA_HEADS, QK_NOPE + QK_ROPE)
    wq_rope = wq3[..., QK_NOPE:]
    wuq_ext = jnp.concatenate([wq3, _rot_half_cols(wq_rope)], axis=-1)
    wuq_ext = wuq_ext.reshape(Q_LORA, MLA_HEADS * QK_PAD).astype(BF16)
    wukv = w_ukv.astype(BF16)
    causal = jnp.tril(jnp.ones((CHUNK, CHUNK), dtype=bool))
    ws = jnp.where(causal[None], w_spatial, 0.0).astype(BF16)
    bsp = jnp.repeat(b_spatial.T, LANES, axis=1)
    wr = jnp.concatenate(
        [w_router_group, w_router_expert,
         jnp.zeros((d, LANES - N_GROUPS - N_EXPERTS), F32)], axis=1).astype(BF16)
    br = jnp.concatenate(
        [b_router_group, b_router_expert, jnp.zeros((LANES - N_GROUPS - N_EXPERTS,), F32)]).reshape(1, LANES)

    pos = jnp.arange(seq, dtype=F32)
    inv_freq = ROPE_THETA ** (-jnp.arange(0, QK_ROPE, 2, dtype=F32) / QK_ROPE)
    ang = pos[:, None] * inv_freq[None, :]
    cos, sin = jnp.cos(ang), jnp.sin(ang)
    cs = jnp.concatenate([cos, cos, sin, sin], axis=1)

    an, q, k, v = _front(x2d, r2(g_norm_mix), win, wkr, r2(g_v), r2(b_v), ws, bsp, r2(g_out_gmlp),
                         r2(g_q_lora), wuq_ext, r2(g_kv_lora), wukv, cs, seq)
    m = _mla_attn(q, k, v, batch, seq)
    kmem, vmem = _mem_kv(mem, r2(g_norm_mem), w_mk.astype(BF16), w_mv.astype(BF16))
    n_assign = t * TOP_K
    p_rows = n_assign + N_EXPERTS * BM
    nb = p_rows // BM
    n_dst = -(-(BM + p_rows + TOP_K * TM_MID) // SMEM_1D_TILE) * SMEM_1D_TILE
    dflt = (n_assign + jnp.arange(n_dst, dtype=jnp.int32) % BM) * ROW_SUB

    x2, gate, cnt, tbl, xs, rows_dst = _mid(
        an, m, x2d, w_out.astype(BF16), r2(g_out_mla), r2(g_norm_xattn), w_mq.astype(BF16), kmem, vmem,
        w_mo.astype(BF16), r2(g_norm_moe), wr, br, dflt, seq, p_rows)

    ar_e = jnp.arange(N_EXPERTS, dtype=jnp.int32)
    counts = cnt[0, :N_EXPERTS]
    nblk = (counts + BM - 1) // BM
    blk_end = jnp.cumsum(nblk)
    blk_start = blk_end - nblk
    n_used = blk_end[-1:].astype(jnp.int32)
    blk = jnp.arange(nb, dtype=jnp.int32)
    block_expert = jnp.minimum(jnp.sum(blk_end[None, :] <= blk[:, None], axis=1), N_EXPERTS - 1).astype(jnp.int32)
    be_onehot = block_expert[:, None] == ar_e[None, :]
    pick = lambda v: jnp.sum(jnp.where(be_onehot, v[None, :], 0), axis=1)
    nxt_blk = pick(blk_end)
    hops = []
    for _ in range(W_LOOKAHEAD):
        valid = nxt_blk < n_used[0]
        onehot = jnp.minimum(nxt_blk, nb - 1)[:, None] == blk[None, :]
        exp_at = jnp.sum(jnp.where(onehot, block_expert[None, :], 0), axis=1)
        hops.append(jnp.where(valid, exp_at, -1))
        end_at = jnp.sum(jnp.where(exp_at[:, None] == ar_e[None, :], blk_end[None, :], 0), axis=1)
        nxt_blk = jnp.where(valid, end_at, nb)
    next_expert = jnp.stack(hops).astype(jnp.int32)
    seg_id = jnp.cumsum((counts > 0).astype(jnp.int32)) - 1
    seg_par = (pick(seg_id) % (W_LOOKAHEAD + 1)).astype(jnp.int32)
    ordinal = blk - pick(blk_start)
    tbl_rows = jnp.sum(jnp.where(be_onehot[:, :, None], tbl[None, :N_EXPERTS, :N_EXPERTS], 0), axis=1)
    page = jnp.sum(jnp.where(ordinal[:, None] == ar_e[None, :], tbl_rows, 0), axis=1)
    last_used = jnp.sum(jnp.where(blk == n_used[0] - 1, page, 0))
    page = jnp.where(blk < n_used[0], page, last_used)
    x_blk = jnp.concatenate([page, page[-1:]]).astype(jnp.int32)
    d_blk = jnp.concatenate([jnp.zeros((1,), jnp.int32), page + 1]).astype(jnp.int32)

    out2 = _experts(block_expert, next_expert, n_used, seg_par, x_blk, d_blk, rows_dst[:p_rows + BM], xs,
                    w_exp_gate, w_exp_up, w_exp_down, n_assign + BM)
    out = _combine(x2, gate, r2(g_final), out2)
    return out.reshape(batch, seq, d)
```

```python
import math

import jax
import jax.numpy as jnp
import numpy as np
from jax import lax
from jax.experimental import pallas as pl
from jax.experimental.pallas import tpu as pltpu

D_MODEL = 2048
CHUNK = 128
GMLP_HEADS = 8
GMLP_WIDTH = 1024
MLA_HEADS = 8
Q_LORA = 512
KV_LORA = 256
QK_NOPE = 128
QK_ROPE = 64
V_DIM = 128
MLA_WIDTH = MLA_HEADS * V_DIM
ROPE_THETA = 10000.0
MEM_HEADS = 4
MEM_HEAD_DIM = 128
MEM_WIDTH = MEM_HEADS * MEM_HEAD_DIM
N_GROUPS = 8
EXPERTS_PER_GROUP = 8
N_EXPERTS = 64
TOP_K = 2
D_EXPERT = 512
EPS = 1e-6
LN_EPS = 1e-5

LANES = 128
QK_PAD = 256
VMEM_LIMIT = 56 * 1024 * 1024
W_LOOKAHEAD = 1

C_UV = 0
C_Q = 2 * GMLP_WIDTH
C_KV = C_Q + Q_LORA
C_KR = C_KV + KV_LORA
SMEM_1D_TILE = 1024

TM_FRONT = 512
TM_MID = 512
SUB_MID = 256
TQ = 256
ATTN_HEADS = 2
BM = 256
TM_ROW = 512

F32 = jnp.float32
BF16 = jnp.bfloat16
U32 = jnp.uint32
NEG = float(np.finfo(np.float32).min)


def _cparams(sem):
    return pltpu.CompilerParams(dimension_semantics=sem, vmem_limit_bytes=VMEM_LIMIT)


def _const_spec(shape):
    n = len(shape)
    return pl.BlockSpec(shape, lambda *_: (0,) * n, pipeline_mode=pl.Buffered(1))


def _rms(x, g):
    ms = jnp.mean(x * x, axis=-1, keepdims=True)
    return x * lax.rsqrt(ms + EPS) * g


def _gelu_tanh(x):
    c = math.sqrt(2.0 / math.pi)
    return 0.5 * x * (1.0 + jnp.tanh(c * (x + 0.044715 * (x * x * x))))


def _dot(a, b):
    return jnp.dot(a, b, preferred_element_type=F32)


def _dot_nt(a, b):
    return lax.dot_general(a, b, (((1,), (1,)), ((), ())), preferred_element_type=F32)


def _pack_pair(hi, lo):
    hb = pltpu.bitcast(hi.astype(BF16).astype(F32), U32)
    lb = pltpu.bitcast(lo.astype(BF16).astype(F32), U32)
    return hb | (lb >> 16)


def _unpack_pair(w):
    hi = pltpu.bitcast(w & jnp.uint32(0xFFFF0000), F32)
    lo = pltpu.bitcast(w << 16, F32)
    return hi, lo


ROW_SUB = 8
ROW_WORDS = ROW_SUB * LANES


def _store_row_tiles(ref, packed):
    m = packed.shape[0]
    for j in range(ROW_SUB):
        ref[pl.ds(j, m, stride=ROW_SUB), :] = packed[:, j * LANES:(j + 1) * LANES]


def _load_row_tiles(ref):
    m = ref.shape[0] // ROW_SUB
    return [ref[pl.ds(j, m, stride=ROW_SUB), :] for j in range(ROW_SUB)]


def _front_kernel(x_ref, gmix_ref, win_ref, wkr_ref, gv_ref, bv_ref, ws_ref, bsp_ref, goutg_ref,
                  gq_ref, wuq_ref, gkv_ref, wukv_ref, cs_ref,
                  an_ref, q_ref, k_ref, v_ref):
    tm = x_ref.shape[0]
    xn = _rms(x_ref[...], gmix_ref[...]).astype(BF16)
    z = _dot_nt(xn, win_ref[...])
    zk = _dot(xn, wkr_ref[...])

    u = _gelu_tanh(z[:, :GMLP_WIDTH])
    v = _gelu_tanh(z[:, GMLP_WIDTH:2 * GMLP_WIDTH])
    mu = jnp.mean(v, axis=-1, keepdims=True)
    vc = v - mu
    var = jnp.mean(vc * vc, axis=-1, keepdims=True)
    vn = (vc * lax.rsqrt(var + LN_EPS) * gv_ref[...] + bv_ref[...]).astype(BF16)
    a_chunks = []
    for c in range(tm // CHUNK):
        r0 = c * CHUNK
        cols = []
        for g in range(GMLP_HEADS):
            c0 = g * LANES
            sv = _dot(ws_ref[g], vn[r0:r0 + CHUNK, c0:c0 + LANES])
            cols.append(sv)
        sv_all = jnp.concatenate(cols, axis=1) + bsp_ref[...]
        a_chunks.append(u[r0:r0 + CHUNK, :] * sv_all)
    a = jnp.concatenate(a_chunks, axis=0)
    an_ref[...] = _rms(a, goutg_ref[...]).astype(BF16)

    cs = cs_ref[...]
    scale = 1.0 / math.sqrt(QK_NOPE + QK_ROPE)
    cqn = _rms(z[:, C_Q:C_KV], gq_ref[...]).astype(BF16)
    q = _dot(cqn, wuq_ref[...]) * scale
    ckvn = _rms(z[:, C_KV:C_KR], gkv_ref[...]).astype(BF16)
    kv = _dot(ckvn, wukv_ref[...])
    lane = lax.broadcasted_iota(jnp.int32, (tm, LANES), 1)
    kr = zk * cs
    kr = jnp.where(lane < QK_ROPE, kr + pltpu.roll(kr, QK_ROPE, 1), 0.0).astype(BF16)
    for h in range(MLA_HEADS):
        b0 = h * QK_PAD
        qr = q[:, b0 + LANES:b0 + QK_PAD] * cs
        qr = qr + pltpu.roll(qr, QK_ROPE, 1)
        q_ref[:, b0:b0 + LANES] = q[:, b0:b0 + LANES].astype(BF16)
        q_ref[:, b0 + LANES:b0 + QK_PAD] = qr.astype(BF16)
        k_ref[:, b0:b0 + LANES] = kv[:, b0:b0 + LANES].astype(BF16)
        k_ref[:, b0 + LANES:b0 + QK_PAD] = kr
        v_ref[:, h * V_DIM:(h + 1) * V_DIM] = kv[:, b0 + LANES:b0 + QK_PAD].astype(BF16)


def _front(x2d, gmix, win, wkr, gv, bv, ws, bsp, goutg, gq, wuq_ext, gkv, wukv, cs, seq):
    t = x2d.shape[0]
    tm = TM_FRONT
    nseq = seq // tm
    row = lambda i: (i, 0)
    return pl.pallas_call(
        _front_kernel,
        grid=(t // tm,),
        in_specs=[
            pl.BlockSpec((tm, D_MODEL), row),
            _const_spec((1, D_MODEL)),
            _const_spec((C_KR, D_MODEL)),
            _const_spec((D_MODEL, LANES)),
            _const_spec((1, GMLP_WIDTH)),
            _const_spec((1, GMLP_WIDTH)),
            _const_spec((GMLP_HEADS, CHUNK, CHUNK)),
            _const_spec((CHUNK, GMLP_WIDTH)),
            _const_spec((1, GMLP_WIDTH)),
            _const_spec((1, Q_LORA)),
            _const_spec((Q_LORA, MLA_HEADS * QK_PAD)),
            _const_spec((1, KV_LORA)),
            _const_spec((KV_LORA, MLA_HEADS * QK_PAD)),
            pl.BlockSpec((tm, LANES), lambda i: (i % nseq, 0)),
        ],
        out_specs=[
            pl.BlockSpec((tm, GMLP_WIDTH), row),
            pl.BlockSpec((tm, MLA_HEADS * QK_PAD), row),
            pl.BlockSpec((tm, MLA_HEADS * QK_PAD), row),
            pl.BlockSpec((tm, MLA_WIDTH), row),
        ],
        out_shape=[
            jax.ShapeDtypeStruct((t, GMLP_WIDTH), BF16),
            jax.ShapeDtypeStruct((t, MLA_HEADS * QK_PAD), BF16),
            jax.ShapeDtypeStruct((t, MLA_HEADS * QK_PAD), BF16),
            jax.ShapeDtypeStruct((t, MLA_WIDTH), BF16),
        ],
        compiler_params=_cparams(("arbitrary",)),
        name="front",
    )(x2d, gmix, win, wkr, gv, bv, ws, bsp, goutg, gq, wuq_ext, gkv, wukv, cs)


def _attn_kernel(q_ref, k_ref, v_ref, o_ref):
    s_len = q_ref.shape[0]
    row = lax.broadcasted_iota(jnp.int32, (TQ, TQ), 0)
    col = lax.broadcasted_iota(jnp.int32, (TQ, TQ), 1)
    causal = col <= row
    n_blk = s_len // TQ
    heads = range(ATTN_HEADS)
    qk = lambda h: slice(h * QK_PAD, (h + 1) * QK_PAD)
    vc = lambda h: slice(h * V_DIM, (h + 1) * V_DIM)
    for qi in range(n_blk):
        rows = slice(qi * TQ, (qi + 1) * TQ)
        qb = [q_ref[rows, qk(h)] for h in heads]
        sd = [jnp.where(causal, _dot_nt(qb[h], k_ref[rows, qk(h)]), NEG) for h in heads]
        m = [jnp.max(a, axis=-1, keepdims=True) for a in sd]
        if qi > 0:
            so = [_dot_nt(qb[h], k_ref[0:qi * TQ, qk(h)]) for h in heads]
            m = [jnp.maximum(a, jnp.max(b, axis=-1, keepdims=True)) for a, b in zip(m, so)]
        pd = [jnp.exp(a - b) for a, b in zip(sd, m)]
        l = [jnp.sum(a, axis=-1, keepdims=True) for a in pd]
        acc = [_dot(pd[h].astype(BF16), v_ref[rows, vc(h)]) for h in heads]
        if qi > 0:
            po = [jnp.exp(a - b) for a, b in zip(so, m)]
            l = [a + jnp.sum(b, axis=-1, keepdims=True) for a, b in zip(l, po)]
            acc = [acc[h] + _dot(po[h].astype(BF16), v_ref[0:qi * TQ, vc(h)]) for h in heads]
        for h in heads:
            o_ref[rows, vc(h)] = (acc[h] / l[h]).astype(o_ref.dtype)


def _mla_attn(q, k, v, batch, seq):
    q3 = q.reshape(batch, seq, MLA_HEADS * QK_PAD)
    k3 = k.reshape(batch, seq, MLA_HEADS * QK_PAD)
    v3 = v.reshape(batch, seq, MLA_WIDTH)
    hmap = lambda b, h: (b, 0, h)
    out = pl.pallas_call(
        _attn_kernel,
        grid=(batch, MLA_HEADS // ATTN_HEADS),
        in_specs=[
            pl.BlockSpec((None, seq, ATTN_HEADS * QK_PAD), hmap),
            pl.BlockSpec((None, seq, ATTN_HEADS * QK_PAD), hmap),
            pl.BlockSpec((None, seq, ATTN_HEADS * V_DIM), hmap),
        ],
        out_specs=pl.BlockSpec((None, seq, ATTN_HEADS * V_DIM), hmap),
        out_shape=jax.ShapeDtypeStruct((batch, seq, MLA_WIDTH), BF16),
        compiler_params=_cparams(("arbitrary", "arbitrary")),
        name="mla_attn",
    )(q3, k3, v3)
    return out.reshape(batch * seq, MLA_WIDTH)


def _memkv_kernel(mem_ref, g_ref, wk_ref, wv_ref, k_ref, v_ref):
    mn = _rms(mem_ref[...], g_ref[...]).astype(BF16)
    k_ref[...] = _dot(mn, wk_ref[...]).astype(BF16)
    v_ref[...] = _dot(mn, wv_ref[...]).astype(BF16)


def _mem_kv(mem, g, wk, wv):
    b, m, d = mem.shape
    bmap = lambda i: (i, 0, 0)
    return pl.pallas_call(
        _memkv_kernel,
        grid=(b,),
        in_specs=[
            pl.BlockSpec((None, m, d), bmap),
            _const_spec((1, d)),
            _const_spec((d, MEM_WIDTH)),
            _const_spec((d, MEM_WIDTH)),
        ],
        out_specs=[pl.BlockSpec((None, m, MEM_WIDTH), bmap)] * 2,
        out_shape=[jax.ShapeDtypeStruct((b, m, MEM_WIDTH), BF16)] * 2,
        compiler_params=_cparams(("arbitrary",)),
        name="mem_kv",
    )(mem, g, wk, wv)


def _row_to_col(x_row):
    r = lax.broadcasted_iota(jnp.int32, (LANES, LANES), 0)
    c = lax.broadcasted_iota(jnp.int32, (LANES, LANES), 1)
    return jnp.sum(jnp.where(r == c, jnp.broadcast_to(x_row, (LANES, LANES)), 0.0), axis=1, keepdims=True)


def _route(lg, state):
    tm = lg.shape[0]
    lane_i = lax.broadcasted_iota(jnp.int32, (tm, LANES), 1)
    lane = lane_i.astype(F32)
    big = float(LANES)
    gmask = lane_i < N_GROUPS
    gl = jnp.where(gmask, lg, NEG)
    gmax = jnp.max(gl, axis=-1, keepdims=True)
    grp = jnp.min(jnp.where(gl == gmax, lane, big), axis=-1, keepdims=True)
    gsum = jnp.sum(jnp.where(gmask, jnp.exp(gl - gmax), 0.0), axis=-1, keepdims=True)
    p_grp = 1.0 / gsum

    lo = (grp + 1.0) * EXPERTS_PER_GROUP
    emask = jnp.logical_and(lane >= lo, lane < lo + EXPERTS_PER_GROUP)
    el = jnp.where(emask, lg, NEG)
    emax = jnp.max(el, axis=-1, keepdims=True)
    ee = jnp.where(emask, jnp.exp(el - emax), 0.0)
    ep = ee / jnp.sum(ee, axis=-1, keepdims=True)
    ep = jnp.where(emask, ep, -1.0)
    p1 = jnp.max(ep, axis=-1, keepdims=True)
    i1 = jnp.min(jnp.where(ep == p1, lane, big), axis=-1, keepdims=True)
    ep2 = jnp.where(lane == i1, -1.0, ep)
    p2 = jnp.max(ep2, axis=-1, keepdims=True)
    i2 = jnp.min(jnp.where(ep2 == p2, lane, big), axis=-1, keepdims=True)
    den = p1 + p2
    g1 = p_grp * (p1 / den)
    g2 = p_grp * (p2 / den)
    e1 = i1 - N_GROUPS
    e2 = i2 - N_GROUPS
    gate = jnp.where(lane_i == 0, g1, jnp.where(lane_i == 1, g2, 0.0))

    cnt, page, npg, tbl = state
    oh1 = (lane == e1).astype(F32)
    oh2 = (lane == e2).astype(F32)
    both = oh1 + oh2
    r_i = lax.broadcasted_iota(jnp.int32, (tm, tm), 0)
    c_i = lax.broadcasted_iota(jnp.int32, (tm, tm), 1)
    before = (c_i < r_i).astype(BF16)
    prefix = _dot(before, both.astype(BF16))
    tc = jnp.sum(both, axis=0, keepdims=True)
    fill = cnt - BM * jnp.floor(cnt * (1.0 / BM))
    need = jnp.logical_and(tc > 0, jnp.logical_or(fill == 0, fill + tc > BM))
    need_f = need.astype(F32)
    e_r = lax.broadcasted_iota(jnp.int32, (LANES, LANES), 0)
    e_c = lax.broadcasted_iota(jnp.int32, (LANES, LANES), 1)
    lower_e = (e_r < e_c).astype(BF16)
    excl = _dot(jnp.broadcast_to(need_f, (ROW_SUB, LANES)).astype(BF16), lower_e)[0:1, :]
    newpage = npg + excl
    pos = fill + prefix
    in_cur = jnp.logical_and(fill > 0, pos < BM)
    slotval = (jnp.where(in_cur, page, newpage) * BM + jnp.where(pos >= BM, pos - BM, pos))
    s1 = jnp.sum(oh1 * slotval, axis=-1, keepdims=True)
    s2 = jnp.sum(oh2 * slotval, axis=-1, keepdims=True)
    slots = jnp.where(lane_i == 0, s1, jnp.where(lane_i == 1, s2, 0.0))

    ordinal = jnp.floor((cnt + (BM - 1)) * (1.0 / BM))
    hit = jnp.logical_and(_row_to_col(need_f) > 0, e_c.astype(F32) == _row_to_col(ordinal))
    tbl = jnp.where(hit, _row_to_col(newpage), tbl)
    new_state = (cnt + tc, jnp.where(need, newpage, page),
                 npg + jnp.sum(need_f, axis=-1, keepdims=True), tbl)
    return slots, gate, new_state


def _mid_kernel(an_ref, m_ref, x_ref, wout_ref, goutm_ref, gxa_ref, wmq_ref, kmem_ref, vmem_ref,
                wmo_ref, gmoe_ref, wr_ref, br_ref, dflt_ref,
                x2_ref, gate_ref, cnt_ref, tbl_ref, xs_ref, rdst_ref,
                state_ref, tbl_s, hp_s, slot_v, slot_s, fin_v, fin_s, rsem, ssem, isem, zsem):
    i = pl.program_id(0)
    n_steps = pl.num_programs(0)
    tm = x_ref.shape[0]
    n_tok = n_steps * tm
    par = i % 2

    def rows_wait(p):
        for k in range(TOP_K):
            pltpu.make_async_copy(hp_s.at[p], xs_ref.at[pl.ds(0, tm * ROW_SUB), :], rsem.at[p]).wait()

    def slot_copy(p):
        return pltpu.make_async_copy(slot_v, slot_s.at[p], ssem.at[p])

    def issue_row(p, step_row8, t):
        for k in range(TOP_K):
            slot = slot_s[p, k, t]
            pltpu.make_async_copy(
                hp_s.at[p, pl.ds(t * ROW_SUB, ROW_SUB), :],
                xs_ref.at[pl.ds(pl.multiple_of(slot * ROW_SUB, ROW_SUB), ROW_SUB), :],
                rsem.at[p]).start()
            rdst_ref[BM + slot] = step_row8 + (k * n_tok + t) * ROW_SUB

    @pl.when(i == 0)
    def _():
        state_ref[...] = jnp.zeros(state_ref.shape, state_ref.dtype)
        tbl_s[...] = jnp.zeros(tbl_s.shape, tbl_s.dtype)
        preset = pltpu.make_async_copy(dflt_ref, rdst_ref, isem)
        preset.start()
        hp_s[1] = jnp.zeros(hp_s.shape[1:], hp_s.dtype)
        spare0 = xs_ref.shape[0] // ROW_SUB - TOP_K * tm

        def prime(t, c):
            for k in range(TOP_K):
                slot_s[1, k, t] = spare0 + k * tm + t
            return c

        lax.fori_loop(0, tm, prime, 0)
        preset.wait()

    @pl.when(i >= 1)
    def _():
        rows_wait(par)
        slot_copy(1 - par).wait()

    def step_body(p):
        prev_row8 = (i - 1) * (tm * ROW_SUB)
        for t in range(tm):
            issue_row(1 - p, prev_row8, t)

        subs = [pl.ds(r0, SUB_MID) for r0 in range(0, x_ref.shape[0], SUB_MID)]
        mn = [_rms(m_ref[r, :].astype(F32), goutm_ref[...]).astype(BF16) for r in subs]
        x1 = [x_ref[r, :] + _dot(an_ref[r, :], wout_ref[:GMLP_WIDTH, :]) + _dot(a, wout_ref[GMLP_WIDTH:, :])
              for r, a in zip(subs, mn)]
        h2 = [_rms(a, gxa_ref[...]).astype(BF16) for a in x1]
        q2 = [(_dot(a, wmq_ref[...]) * (1.0 / math.sqrt(MEM_HEAD_DIM))).astype(BF16) for a in h2]

        def mem_attention(q):
            outs = []
            for h in range(MEM_HEADS):
                c0 = h * MEM_HEAD_DIM
                s = _dot_nt(q[:, c0:c0 + MEM_HEAD_DIM], kmem_ref[:, c0:c0 + MEM_HEAD_DIM])
                m = jnp.max(s, axis=-1, keepdims=True)
                e = jnp.exp(s - m)
                l = jnp.sum(e, axis=-1, keepdims=True)
                outs.append(_dot(e.astype(BF16), vmem_ref[:, c0:c0 + MEM_HEAD_DIM]) / l)
            return jnp.concatenate(outs, axis=1).astype(BF16)

        o = [mem_attention(q) for q in q2]
        x2 = [a + _dot(b, wmo_ref[...]) for a, b in zip(x1, o)]
        for r, a in zip(subs, x2):
            x2_ref[r, :] = a
        h3 = [_rms(a, gmoe_ref[...]) for a in x2]
        half = D_MODEL // 2
        for n, a in enumerate(h3):
            _store_row_tiles(hp_s.at[p, pl.ds(n * SUB_MID * ROW_SUB, SUB_MID * ROW_SUB), :],
                             _pack_pair(a[:, :half], a[:, half:]))
        lg = [_dot(a.astype(BF16), wr_ref[...]) + br_ref[...] for a in h3]
        state = (state_ref[0:1, :], state_ref[1:2, :], state_ref[2:3, :], tbl_s[...])
        for n, (r, a) in enumerate(zip(subs, lg)):
            slots, gate, state = _route(a, state)
            gate_ref[r, :] = gate
            slot_v[:, n * SUB_MID:(n + 1) * SUB_MID] = slots.T[0:ROW_SUB, :].astype(jnp.int32)
        slot_copy(p).start()
        state_ref[0:1, :] = state[0]
        state_ref[1:2, :] = state[1]
        state_ref[2:3, :] = state[2]
        tbl_s[...] = state[3]
        cnt_ref[...] = jnp.broadcast_to(state[0], cnt_ref.shape).astype(jnp.int32)
        tbl_ref[...] = state[3].astype(jnp.int32)

    for p in range(2):
        pl.when(par == p)(lambda p=p: step_body(p))

    @pl.when(i == n_steps - 1)
    def _():
        slot_copy(par).wait()

        def last_rows(t, c):
            issue_row(par, i * (tm * ROW_SUB), t)
            return c

        lax.fori_loop(0, tm, last_rows, 0)
        rows_wait(1 - par)
        rows_wait(par)

        cnt_f, page_f, npg_f = state_ref[0:1, :], state_ref[1:2, :], state_ref[2:3, :]
        fill_f = cnt_f - BM * jnp.floor(cnt_f * (1.0 / BM))
        fin_v[...] = jnp.concatenate(
            [page_f * BM + fill_f, jnp.where(fill_f > 0, BM - fill_f, 0.0), npg_f,
             jnp.zeros((ROW_SUB - 3, LANES), F32)], axis=0).astype(jnp.int32)
        fin_cp = pltpu.make_async_copy(fin_v, fin_s, isem)
        fin_cp.start()
        zero_rows = hp_s.at[0, pl.ds(0, BM * ROW_SUB), :]
        zero_rows[...] = jnp.zeros((BM * ROW_SUB, LANES), hp_s.dtype)
        fin_cp.wait()
        _zero_page_tails(fin_s, zero_rows, xs_ref, zsem)


def _mid(an, m, x2d, wout, goutm, gxa, wmq, kmem, vmem, wmo, gmoe, wr, br, dflt, seq, p_rows):
    t = x2d.shape[0]
    tm = TM_MID
    xs_rows = p_rows + TOP_K * tm
    nseq = seq // tm
    mem_len = kmem.shape[1]
    row = lambda i: (i, 0)
    bmap = lambda i: (i // nseq, 0, 0)
    return pl.pallas_call(
        _mid_kernel,
        grid=(t // tm,),
        in_specs=[
            pl.BlockSpec((tm, GMLP_WIDTH), row),
            pl.BlockSpec((tm, MLA_WIDTH), row),
            pl.BlockSpec((tm, D_MODEL), row),
            _const_spec((D_MODEL, D_MODEL)),
            _const_spec((1, MLA_WIDTH)),
            _const_spec((1, D_MODEL)),
            _const_spec((D_MODEL, MEM_WIDTH)),
            pl.BlockSpec((None, mem_len, MEM_WIDTH), bmap),
            pl.BlockSpec((None, mem_len, MEM_WIDTH), bmap),
            _const_spec((MEM_WIDTH, D_MODEL)),
            _const_spec((1, D_MODEL)),
            _const_spec((D_MODEL, LANES)),
            _const_spec((1, LANES)),
            pl.BlockSpec(memory_space=pl.ANY),
        ],
        out_specs=[
            pl.BlockSpec((tm, D_MODEL), row),
            pl.BlockSpec((tm, LANES), row),
            pl.BlockSpec((ROW_SUB, LANES), lambda i: (0, 0)),
            pl.BlockSpec((LANES, LANES), lambda i: (0, 0)),
            pl.BlockSpec(memory_space=pl.ANY),
            pl.BlockSpec(memory_space=pltpu.SMEM),
        ],
        out_shape=[
            jax.ShapeDtypeStruct((t, D_MODEL), F32),
            jax.ShapeDtypeStruct((t, LANES), F32),
            jax.ShapeDtypeStruct((ROW_SUB, LANES), jnp.int32),
            jax.ShapeDtypeStruct((LANES, LANES), jnp.int32),
            jax.ShapeDtypeStruct((xs_rows * ROW_SUB, LANES), U32),
            jax.ShapeDtypeStruct(dflt.shape, jnp.int32),
        ],
        scratch_shapes=[
            pltpu.VMEM((ROW_SUB, LANES), F32),
            pltpu.VMEM((LANES, LANES), F32),
            pltpu.VMEM((2, tm * ROW_SUB, LANES), U32),
            pltpu.VMEM((ROW_SUB, tm), jnp.int32),
            pltpu.SMEM((2, ROW_SUB, tm), jnp.int32),
            pltpu.VMEM((ROW_SUB, LANES), jnp.int32),
            pltpu.SMEM((ROW_SUB, LANES), jnp.int32),
            pltpu.SemaphoreType.DMA((2,)),
            pltpu.SemaphoreType.DMA((2,)),
            pltpu.SemaphoreType.DMA(()),
            pltpu.SemaphoreType.DMA(()),
        ],
        compiler_params=_cparams(("arbitrary",)),
        name="mid",
    )(an, m, x2d, wout, goutm, gxa, wmq, kmem, vmem, wmo, gmoe, wr, br, dflt)


PAD_BITS = tuple(1 << b for b in reversed(range(BM.bit_length() - 1)))


def _zero_page_tails(fin_s, zero_buf, xs_ref, zsem):
    def pad_copy(e, bit):
        n = fin_s[1, e]
        off = fin_s[0, e] + (n & ~(2 * bit - 1))
        return n & bit, pltpu.make_async_copy(
            zero_buf.at[pl.ds(0, bit * ROW_SUB), :],
            xs_ref.at[pl.ds(pl.multiple_of(off * ROW_SUB, ROW_SUB), bit * ROW_SUB), :], zsem)

    def fill(e, c):
        for bit in PAD_BITS:
            on, cp = pad_copy(e, bit)

            @pl.when(on != 0)
            def _():
                cp.start()
        return c

    def fill_wait(e, c):
        for bit in PAD_BITS:
            on, cp = pad_copy(e, bit)

            @pl.when(on != 0)
            def _():
                cp.wait()
        return c

    def tail_copy(b):
        return pltpu.make_async_copy(
            zero_buf, xs_ref.at[pl.ds(pl.multiple_of(b * (BM * ROW_SUB), BM * ROW_SUB), BM * ROW_SUB), :],
            zsem)

    def tail(b, c):
        tail_copy(b).start()
        return c

    def tail_wait(b, c):
        tail_copy(b).wait()
        return c

    n_pages = (xs_ref.shape[0] // ROW_SUB - TOP_K * TM_MID) // BM
    lax.fori_loop(0, N_EXPERTS, fill, 0)
    lax.fori_loop(fin_s[2, 0], n_pages, tail, 0)
    lax.fori_loop(0, N_EXPERTS, fill_wait, 0)
    lax.fori_loop(fin_s[2, 0], n_pages, tail_wait, 0)


def _row_scatter_start(src_vmem, idx_ref, dst_hbm, sem):
    for r in range(BM):
        d0 = pl.multiple_of(idx_ref[0, 0, r], ROW_SUB)
        pltpu.make_async_copy(src_vmem.at[pl.ds(r * ROW_SUB, ROW_SUB), :],
                              dst_hbm.at[pl.ds(d0, ROW_SUB), :], sem).start()


def _scatter_wait(src_vmem, dst_hbm, sem):
    pltpu.make_async_copy(src_vmem, dst_hbm.at[pl.ds(0, BM * ROW_SUB), :], sem).wait()


def _cast_pair_rows(dst_s, src_st):
    half = D_MODEL // 2
    for j in range(ROW_SUB):
        a = j * LANES
        dst_s[2 * a:2 * a + LANES, :] = src_st[a:a + LANES, :].astype(BF16)
        dst_s[2 * a + LANES:2 * a + 2 * LANES, :] = src_st[half + a:half + a + LANES, :].astype(BF16)


def _expert_kernel(be_ref, nxt_ref, nused_ref, segpar_ref, xblk_ref, dblk_ref,
                   dstp_ref, xs_ref, wg_ref, wu_ref, wd_ref,
                   out_ref,
                   wg_st, wu_st, wd_st, wg_s, wu_s, wd_s, ybuf0, ybuf1,
                   wsem, ssem):
    i = pl.program_id(0)
    nused = nused_ref[0]
    used = i < nused
    par = i % 2
    ic = jnp.minimum(i, be_ref.shape[0] - 1)
    e = be_ref[ic]
    first = jnp.logical_or(i == 0, e != be_ref[jnp.maximum(i - 1, 0)])

    def weight_copies(ex, slot):
        return (pltpu.make_async_copy(wg_ref.at[ex], wg_st.at[slot], wsem.at[slot, 0]),
                pltpu.make_async_copy(wu_ref.at[ex], wu_st.at[slot], wsem.at[slot, 1]),
                pltpu.make_async_copy(wd_ref.at[ex], wd_st.at[slot], wsem.at[slot, 2]))

    @pl.when(i == 0)
    def _():
        for cp in weight_copies(e, 0):
            cp.start()
        for a in range(1, W_LOOKAHEAD):
            nxt = nxt_ref[a - 1, 0]

            @pl.when(nxt >= 0)
            def _():
                for cp in weight_copies(nxt, a):
                    cp.start()
        ybuf1[...] = jnp.zeros(ybuf1.shape, ybuf1.dtype)

    def load_weights(slot):
        nxt = nxt_ref[W_LOOKAHEAD - 1, ic]

        @pl.when(nxt >= 0)
        def _():
            for cp in weight_copies(nxt, (slot + W_LOOKAHEAD) % (W_LOOKAHEAD + 1)):
                cp.start()

        for cp in weight_copies(0, slot):
            cp.wait()
        _cast_pair_rows(wg_s, wg_st.at[slot])
        _cast_pair_rows(wu_s, wu_st.at[slot])
        wd_s[...] = wd_st[slot].astype(BF16)

    for slot in range(W_LOOKAHEAD + 1):
        @pl.when(jnp.logical_and(jnp.logical_and(first, used), segpar_ref[ic] == slot))
        def _():
            load_weights(slot)

    def step(yb, yo, s):
        o = 1 - s

        @pl.when(i >= 1)
        def _():
            _scatter_wait(yb, out_ref, ssem.at[s])

        _row_scatter_start(yo, dstp_ref, out_ref, ssem.at[o])
        half = D_MODEL // 2
        cols = []
        for w in _load_row_tiles(xs_ref):
            hi, lo = _unpack_pair(w)
            cols += [hi.astype(BF16), lo.astype(BF16)]
        xrow = jnp.concatenate(cols, axis=1)
        g = _dot(xrow, wg_s[...])
        u = _dot(xrow, wu_s[...])
        hm = (g * jax.nn.sigmoid(g) * u).astype(BF16)
        y = _dot(hm, wd_s[...])
        _store_row_tiles(yb, _pack_pair(y[:, :half], y[:, half:]))

    @pl.when(jnp.logical_and(used, par == 0))
    def _():
        step(ybuf0, ybuf1, 0)

    @pl.when(jnp.logical_and(used, par == 1))
    def _():
        step(ybuf1, ybuf0, 1)

    def drain(yb, yo, s):
        o = 1 - s
        _scatter_wait(yb, out_ref, ssem.at[s])
        _row_scatter_start(yo, dstp_ref, out_ref, ssem.at[o])
        _scatter_wait(yo, out_ref, ssem.at[o])

    @pl.when(jnp.logical_and(i == nused, par == 0))
    def _():
        drain(ybuf0, ybuf1, 0)

    @pl.when(jnp.logical_and(i == nused, par == 1))
    def _():
        drain(ybuf1, ybuf0, 1)


def _experts(block_expert, next_expert, n_used, seg_par, x_blk, d_blk, rows_dst, xs, wg, wu, wd, n_out_rows):
    nb = block_expert.shape[0]
    dst3 = rows_dst.reshape(nb + 1, 1, BM)
    grid_spec = pltpu.PrefetchScalarGridSpec(
        num_scalar_prefetch=6,
        grid=(nb + 1,),
        in_specs=[
            pl.BlockSpec((1, 1, BM), lambda i, be, nx, nu, sp, xb, db: (db[i], 0, 0), memory_space=pltpu.SMEM),
            pl.BlockSpec((BM * ROW_SUB, LANES), lambda i, be, nx, nu, sp, xb, db: (xb[i], 0)),
            pl.BlockSpec(memory_space=pl.ANY),
            pl.BlockSpec(memory_space=pl.ANY),
            pl.BlockSpec(memory_space=pl.ANY),
        ],
        out_specs=pl.BlockSpec(memory_space=pl.ANY),
        scratch_shapes=[
            pltpu.VMEM((W_LOOKAHEAD + 1, D_MODEL, D_EXPERT), F32),
            pltpu.VMEM((W_LOOKAHEAD + 1, D_MODEL, D_EXPERT), F32),
            pltpu.VMEM((W_LOOKAHEAD + 1, D_EXPERT, D_MODEL), F32),
            pltpu.VMEM((D_MODEL, D_EXPERT), BF16),
            pltpu.VMEM((D_MODEL, D_EXPERT), BF16),
            pltpu.VMEM((D_EXPERT, D_MODEL), BF16),
            pltpu.VMEM((BM * ROW_SUB, LANES), U32),
            pltpu.VMEM((BM * ROW_SUB, LANES), U32),
            pltpu.SemaphoreType.DMA((W_LOOKAHEAD + 1, 3)),
            pltpu.SemaphoreType.DMA((2,)),
        ],
    )
    return pl.pallas_call(
        _expert_kernel,
        grid_spec=grid_spec,
        out_shape=jax.ShapeDtypeStruct((n_out_rows * ROW_SUB, LANES), U32),
        compiler_params=_cparams(("arbitrary",)),
        name="experts",
    )(block_expert, next_expert, n_used, seg_par, x_blk, d_blk, dst3, xs, wg, wu, wd)


def _combine_kernel(x2_ref, gate_ref, gfin_ref, y0_ref, y1_ref, o_ref):
    gate = gate_ref[...]
    half = D_MODEL // 2
    y_hi = x2_ref[:, :half]
    y_lo = x2_ref[:, half:]
    for k, y_ref in enumerate((y0_ref, y1_ref)):
        pairs = [_unpack_pair(w) for w in _load_row_tiles(y_ref)]
        hi = jnp.concatenate([p[0] for p in pairs], axis=1)
        lo = jnp.concatenate([p[1] for p in pairs], axis=1)
        gk = gate[:, k:k + 1]
        y_hi = y_hi + gk * hi
        y_lo = y_lo + gk * lo
    ms = (jnp.sum(y_hi * y_hi, axis=-1, keepdims=True)
          + jnp.sum(y_lo * y_lo, axis=-1, keepdims=True)) * (1.0 / D_MODEL)
    r = lax.rsqrt(ms + EPS)
    o_ref[:, :half] = y_hi * r * gfin_ref[:, :half]
    o_ref[:, half:] = y_lo * r * gfin_ref[:, half:]


def _combine(x2, gate, gfin, out2):
    t, d = x2.shape
    tm = TM_ROW
    nb = t // tm
    row = lambda i: (i, 0)
    return pl.pallas_call(
        _combine_kernel,
        grid=(nb,),
        in_specs=[
            pl.BlockSpec((tm, d), row),
            pl.BlockSpec((tm, LANES), row),
            _const_spec((1, d)),
            pl.BlockSpec((tm * ROW_SUB, LANES), row),
            pl.BlockSpec((tm * ROW_SUB, LANES), lambda i: (nb + i, 0)),
        ],
        out_specs=pl.BlockSpec((tm, d), row),
        out_shape=jax.ShapeDtypeStruct((t, d), F32),
        compiler_params=_cparams(("arbitrary",)),
        name="combine",
    )(x2, gate, gfin, out2, out2)


def _rot_half_cols(w):
    half = QK_ROPE // 2
    return jnp.concatenate([-w[..., half:], w[..., :half]], axis=-1)


def kernel(x, mem, g_norm_mix, w_in, g_v, b_v, w_spatial, b_spatial, g_q_lora, w_uq, g_kv_lora, w_ukv, g_out_gmlp, g_out_mla, w_out, g_norm_xattn, g_norm_mem, w_mq, w_mk, w_mv, w_mo, g_norm_moe, w_router_group, b_router_group, w_router_expert, b_router_expert, w_exp_gate, w_exp_up, w_exp_down, g_final):
    batch, seq, d = x.shape
    t = batch * seq
    x2d = x.reshape(t, d)
    r2 = lambda a: a.reshape(1, -1)

    w_kr = w_in[:, C_KR:C_KR + QK_ROPE]
    win = w_in.T[:C_KR, :].astype(BF16)
    wkr = jnp.concatenate([w_kr, _rot_half_cols(w_kr)], axis=1).astype(BF16)
    wq3 = w_uq.reshape(Q_LORA, MLA_HEADS, QK_NOPE + QK_ROPE)
    wq_rope = wq3[..., QK_NOPE:]
    wuq_ext = jnp.concatenate([wq3, _rot_half_cols(wq_rope)], axis=-1)
    wuq_ext = wuq_ext.reshape(Q_LORA, MLA_HEADS * QK_PAD).astype(BF16)
    wukv = w_ukv.astype(BF16)
    causal = jnp.tril(jnp.ones((CHUNK, CHUNK), dtype=bool))
    ws = jnp.where(causal[None], w_spatial, 0.0).astype(BF16)
    bsp = jnp.repeat(b_spatial.T, LANES, axis=1)
    wr = jnp.concatenate(
        [w_router_group, w_router_expert,
         jnp.zeros((d, LANES - N_GROUPS - N_EXPERTS), F32)], axis=1).astype(BF16)
    br = jnp.concatenate(
        [b_router_group, b_router_expert, jnp.zeros((LANES - N_GROUPS - N_EXPERTS,), F32)]).reshape(1, LANES)

    pos = jnp.arange(seq, dtype=F32)
    inv_freq = ROPE_THETA ** (-jnp.arange(0, QK_ROPE, 2, dtype=F32) / QK_ROPE)
    ang = pos[:, None] * inv_freq[None, :]
    cos, sin = jnp.cos(ang), jnp.sin(ang)
    cs = jnp.concatenate([cos, cos, sin, sin], axis=1)

    an, q, k, v = _front(x2d, r2(g_norm_mix), win, wkr, r2(g_v), r2(b_v), ws, bsp, r2(g_out_gmlp),
                         r2(g_q_lora), wuq_ext, r2(g_kv_lora), wukv, cs, seq)
    m = _mla_attn(q, k, v, batch, seq)
    kmem, vmem = _mem_kv(mem, r2(g_norm_mem), w_mk.astype(BF16), w_mv.astype(BF16))
    n_assign = t * TOP_K
    p_rows = n_assign + N_EXPERTS * BM
    nb = p_rows // BM
    n_dst = -(-(BM + p_rows + TOP_K * TM_MID) // SMEM_1D_TILE) * SMEM_1D_TILE
    dflt = (n_assign + jnp.arange(n_dst, dtype=jnp.int32) % BM) * ROW_SUB

    x2, gate, cnt, tbl, xs, rows_dst = _mid(
        an, m, x2d, w_out.astype(BF16), r2(g_out_mla), r2(g_norm_xattn), w_mq.astype(BF16), kmem, vmem,
        w_mo.astype(BF16), r2(g_norm_moe), wr, br, dflt, seq, p_rows)

    ar_e = jnp.arange(N_EXPERTS, dtype=jnp.int32)
    counts = cnt[0, :N_EXPERTS]
    nblk = (counts + BM - 1) // BM
    blk_end = jnp.cumsum(nblk)
    blk_start = blk_end - nblk
    n_used = blk_end[-1:].astype(jnp.int32)
    blk = jnp.arange(nb, dtype=jnp.int32)
    block_expert = jnp.minimum(jnp.sum(blk_end[None, :] <= blk[:, None], axis=1), N_EXPERTS - 1).astype(jnp.int32)
    be_onehot = block_expert[:, None] == ar_e[None, :]
    pick = lambda v: jnp.sum(jnp.where(be_onehot, v[None, :], 0), axis=1)
    nxt_blk = pick(blk_end)
    hops = []
    for _ in range(W_LOOKAHEAD):
        valid = nxt_blk < n_used[0]
        onehot = jnp.minimum(nxt_blk, nb - 1)[:, None] == blk[None, :]
        exp_at = jnp.sum(jnp.where(onehot, block_expert[None, :], 0), axis=1)
        hops.append(jnp.where(valid, exp_at, -1))
        end_at = jnp.sum(jnp.where(exp_at[:, None] == ar_e[None, :], blk_end[None, :], 0), axis=1)
        nxt_blk = jnp.where(valid, end_at, nb)
    next_expert = jnp.stack(hops).astype(jnp.int32)
    seg_id = jnp.cumsum((counts > 0).astype(jnp.int32)) - 1
    seg_par = (pick(seg_id) % (W_LOOKAHEAD + 1)).astype(jnp.int32)
    ordinal = blk - pick(blk_start)
    tbl_rows = jnp.sum(jnp.where(be_onehot[:, :, None], tbl[None, :N_EXPERTS, :N_EXPERTS], 0), axis=1)
    page = jnp.sum(jnp.where(ordinal[:, None] == ar_e[None, :], tbl_rows, 0), axis=1)
    last_used = jnp.sum(jnp.where(blk == n_used[0] - 1, page, 0))
    page = jnp.where(blk < n_used[0], page, last_used)
    x_blk = jnp.concatenate([page, page[-1:]]).astype(jnp.int32)
    d_blk = jnp.concatenate([jnp.zeros((1,), jnp.int32), page + 1]).astype(jnp.int32)

    out2 = _experts(block_expert, next_expert, n_used, seg_par, x_blk, d_blk, rows_dst[:p_rows + BM], xs,
                    w_exp_gate, w_exp_up, w_exp_down, n_assign + BM)
    out = _combine(x2, gate, r2(g_final), out2)
    return out.reshape(batch, seq, d)
```
